```python
import jax
import jax.numpy as jnp
from jax import lax
import numpy as np

D_MODEL = 2048
BATCH = 2
SEQ = 4096
DEPTH = 2
DEC_BATCH = 128
DEC_SEQ = 4
PAST_LEN = 2048
PAGE_SIZE = 128

N_MIXERS = 2
N_ATTN_LAYERS = (DEPTH + 1) // 2
N_REC_LAYERS = DEPTH // 2
N_HEADS = 16
HEAD_DIM = D_MODEL // N_HEADS
N_KV_HEADS = 4
N_IDX_HEADS = 16
IDX_DIM = 64
INDEX_TOPK = 256
Q_BLOCK = 128
ROPE_THETA = 10000.0
Q_W = N_HEADS * HEAD_DIM
KV_W = N_KV_HEADS * HEAD_DIM
QI_W = N_IDX_HEADS * IDX_DIM
ATTN_SPLIT_POINTS = (Q_W, Q_W + KV_W, Q_W + 2 * KV_W, Q_W + 2 * KV_W + QI_W, Q_W + 2 * KV_W + QI_W + IDX_DIM)
ATTN_IN_DIM = Q_W + 2 * KV_W + QI_W + IDX_DIM + N_IDX_HEADS
REC_EXPAND = 128
N_REC_HEADS = D_MODEL // REC_EXPAND
REC_HEAD_V = D_MODEL // N_REC_HEADS
REC_CHUNK = 64
D_FF = 5632
EPS = 1e-6

kernel_name = "hybrid_dsa_hgrn2_macaron_adaln_step"

F32 = jnp.float32


def rmsnorm(x, gain):
    xf = x.astype(F32)
    y = xf * lax.rsqrt(jnp.mean(xf * xf, axis=-1, keepdims=True) + EPS) * gain.astype(F32)
    return y.astype(x.dtype)


def rope(x, pos):
    d = x.shape[-1]
    inv = ROPE_THETA ** (-jnp.arange(0, d, 2, dtype=F32) / d)
    ang = pos.astype(F32)[:, None] * inv[None, :]
    cos = jnp.cos(ang)[None, :, None, :]
    sin = jnp.sin(ang)[None, :, None, :]
    xf = x.astype(F32)
    x1, x2 = xf[..., : d // 2], xf[..., d // 2:]
    return jnp.concatenate([x1 * cos - x2 * sin, x2 * cos + x1 * sin], axis=-1).astype(x.dtype)


def swiglu(h, w_up, w_down):
    a, b = jnp.split(h @ w_up, 2, axis=-1)
    return (jax.nn.silu(a) * b) @ w_down


def ada_pre(h, gain, shift, scale):
    return rmsnorm(h, gain) * (1 + scale) + shift


def attn_project(h, w_in, q_gain, k_gain, pos):
    B, T, _ = h.shape
    q, k, v, qi, ki, wi = jnp.split(h @ w_in, ATTN_SPLIT_POINTS, axis=-1)
    q = rope(rmsnorm(q.reshape(B, T, N_HEADS, HEAD_DIM), q_gain), pos)
    k = rope(rmsnorm(k.reshape(B, T, N_KV_HEADS, HEAD_DIM), k_gain), pos)
    v = v.reshape(B, T, N_KV_HEADS, HEAD_DIM)
    qi = rope(qi.reshape(B, T, N_IDX_HEADS, IDX_DIM), pos)
    ki = rope(ki[:, :, None, :], pos)[:, :, 0]
    wi = wi * N_IDX_HEADS ** -0.5
    return q, k, v, qi, ki, wi


def indexer_select(qi, wi, ki, q_pos, topk):
    L = ki.shape[1]
    logits = jnp.einsum('bthd,bsd->bths', qi.astype(F32), ki.astype(F32)) * IDX_DIM ** -0.5
    score = jnp.einsum('bth,bths->bts', wi.astype(F32), jax.nn.relu(logits))
    admissible = jnp.arange(L)[None, None, :] <= q_pos[None, :, None]
    score = jnp.where(admissible, score, -jnp.inf)
    _, sel = lax.top_k(score, topk)
    valid = sel <= q_pos[None, :, None]
    return sel, valid


def sparse_attend(q, ks, vs, valid):
    B, T, H, Dh = q.shape
    qg = q.reshape(B, T, N_KV_HEADS, H // N_KV_HEADS, Dh)
    s = jnp.einsum('btgrd,btngd->btgrn', qg, ks).astype(F32) * Dh ** -0.5
    s = jnp.where(valid[:, :, None, None, :], s, -jnp.inf)
    p = jax.nn.softmax(s, axis=-1).astype(vs.dtype)
    o = jnp.einsum('btgrn,btngd->btgrd', p, vs)
    return o.reshape(B, T, H * Dh)


def gather_rows(a, idx):
    return jax.vmap(lambda ab, ib: ab[ib])(a, idx)


def dsa_prompt(h, w_in, w_out, q_gain, k_gain):
    B, S, _ = h.shape
    pos = jnp.arange(S)
    q, k, v, qi, ki, wi = attn_project(h, w_in, q_gain, k_gain, pos)
    topk = min(INDEX_TOPK, S // 4)
    nb = S // Q_BLOCK

    def blocks(a):
        return a.reshape((B, nb, Q_BLOCK) + a.shape[2:]).swapaxes(0, 1)

    def one_block(args):
        qb, qib, wib, pb = args
        sel, valid = indexer_select(qib, wib, ki, pb, topk)
        return sparse_attend(qb, gather_rows(k, sel), gather_rows(v, sel), valid)

    o = lax.map(one_block, (blocks(q), blocks(qi), blocks(wi), pos.reshape(nb, Q_BLOCK)))
    o = o.swapaxes(0, 1).reshape(B, S, Q_W)
    return o @ w_out, (k, v, ki)


def dsa_sample(h, w_in, w_out, q_gain, k_gain, ck, cv, cki, page_table):
    B, T, _ = h.shape
    page = ck.shape[1]
    past = page_table.shape[1] * page
    pos = past + jnp.arange(T)
    q, k, v, qi, ki, wi = attn_project(h, w_in, q_gain, k_gain, pos)
    ki_past = cki[page_table].reshape(B, past, IDX_DIM)
    ki_all = jnp.concatenate([ki_past, ki.astype(ki_past.dtype)], axis=1)
    topk = min(INDEX_TOPK, (past + T) // 4)
    sel, valid = indexer_select(qi, wi, ki_all, pos, topk)
    in_past = (sel < past)[..., None, None]
    sp = jnp.minimum(sel, past - 1)
    phys = jax.vmap(lambda pt, s: pt[s])(page_table, sp // page) * page + sp % page
    sn = jnp.clip(sel - past, 0, T - 1)
    ks = jnp.where(in_past, ck.reshape(-1, N_KV_HEADS, HEAD_DIM)[phys], gather_rows(k, sn).astype(ck.dtype))
    vs = jnp.where(in_past, cv.reshape(-1, N_KV_HEADS, HEAD_DIM)[phys], gather_rows(v, sn).astype(cv.dtype))
    o = sparse_attend(q, ks, vs, valid)
    return o @ w_out, (k, v, ki)


def hgrn_lower_bounds(lb_logits):
    p = jax.nn.softmax(lb_logits.astype(F32), axis=0)
    return jnp.cumsum(p, axis=0) - p[0]


def gla_chunked(q, k, v, log_f, S0, chunk):
    B, T, H, dk = q.shape
    dv = v.shape[-1]
    n = T // chunk

    def to_chunks(a):
        return a.astype(F32).reshape(B, n, chunk, H, a.shape[-1]).transpose(1, 0, 3, 2, 4)

    causal = jnp.tril(jnp.ones((chunk, chunk), bool))[None, None, :, :, None]

    def step(S, inp):
        qc, kc, vc, gc = inp
        b = jnp.cumsum(gc, axis=2)
        rel = jnp.exp(jnp.where(causal, b[:, :, :, None, :] - b[:, :, None, :, :], -jnp.inf))
        a = jnp.einsum('bhtd,bhsd,bhtsd->bhts', qc, kc, rel)
        o = jnp.einsum('bhts,bhsv->bhtv', a, vc) + jnp.einsum('bhtd,bhdv->bhtv', qc * jnp.exp(b), S)
        b_end = b[:, :, -1:, :]
        S = jnp.exp(b_end[:, :, 0, :, None]) * S + jnp.einsum('bhsd,bhsv->bhdv', kc * jnp.exp(b_end - b), vc)
        return S, o

    S, o = lax.scan(step, S0.astype(F32), (to_chunks(q), to_chunks(k), to_chunks(v), to_chunks(log_f)))
    return o.transpose(1, 0, 3, 2, 4).reshape(B, T, H, dv), S


def hgrn2_mix(h, w_in, w_out, out_gain, lb, S0, chunk):
    B, T, _ = h.shape
    q, f, i, g = jnp.split(h @ w_in, 4, axis=-1)
    log_f = jnp.logaddexp(jnp.log(lb), jnp.log1p(-lb) + jax.nn.log_sigmoid(f.astype(F32)))
    k = -jnp.expm1(log_f)

    def heads(a):
        return a.reshape(B, T, N_REC_HEADS, -1)

    o, S = gla_chunked(heads(jax.nn.silu(q)), heads(k), heads(i), heads(log_f), S0, chunk)
    o = rmsnorm(o, out_gain.reshape(N_REC_HEADS, REC_HEAD_V)) * heads(jax.nn.silu(g))
    return o.reshape(B, T, D_MODEL).astype(h.dtype) @ w_out, S


def trunk(x, c, mix, norm_gain, w_ada, b_ada, w_ff_up, w_ff_down):
    B = x.shape[0]
    cs = jax.nn.silu(c)
    states = []
    for i in range(DEPTH):
        mod = (cs @ w_ada[i] + b_ada[i]).reshape(B, 3, 3, 1, D_MODEL)
        h = ada_pre(x, norm_gain[i, 0], mod[:, 0, 0], mod[:, 0, 1])
        x = x + 0.5 * (1 + mod[:, 0, 2]) * swiglu(h, w_ff_up[i, 0], w_ff_down[i, 0])
        h = ada_pre(x, norm_gain[i, 1], mod[:, 1, 0], mod[:, 1, 1])
        y, st = mix(i, h)
        x = x + (1 + mod[:, 1, 2]) * y
        states.append(st)
        h = ada_pre(x, norm_gain[i, 2], mod[:, 2, 0], mod[:, 2, 1])
        x = x + 0.5 * (1 + mod[:, 2, 2]) * swiglu(h, w_ff_up[i, 1], w_ff_down[i, 1])
    return x, states


def setup_inputs(seed: int = 0) -> dict:
    key = jax.random.key(seed)
    ks = jax.random.split(key, 24)
    n_pages = PAST_LEN // PAGE_SIZE
    n_pool = (5 * DEC_BATCH * n_pages) // 4

    def nrm(k, shape, scale):
        return jax.random.normal(k, shape, F32) * scale

    page_table = jax.random.permutation(ks[8], n_pool)[: DEC_BATCH * n_pages].reshape(DEC_BATCH, n_pages).astype(jnp.int32)
    return {
        "x_prompt": nrm(ks[0], (BATCH, SEQ, D_MODEL), 1.0),
        "x_sample": nrm(ks[1], (DEC_BATCH, DEC_SEQ, D_MODEL), 1.0),
        "c_prompt": nrm(ks[2], (BATCH, D_MODEL), 1.0),
        "c_sample": nrm(ks[3], (DEC_BATCH, D_MODEL), 1.0),
        "cache_k": nrm(ks[4], (N_ATTN_LAYERS, n_pool, PAGE_SIZE, N_KV_HEADS, HEAD_DIM), 1.0),
        "cache_v": nrm(ks[5], (N_ATTN_LAYERS, n_pool, PAGE_SIZE, N_KV_HEADS, HEAD_DIM), 1.0),
        "cache_kidx": nrm(ks[6], (N_ATTN_LAYERS, n_pool, PAGE_SIZE, IDX_DIM), 1.0),
        "state_hgrn": nrm(ks[7], (N_REC_LAYERS, DEC_BATCH, N_REC_HEADS, REC_EXPAND, REC_HEAD_V), 0.3),
        "page_table": page_table,
        "norm_gain": 1.0 + nrm(ks[9], (DEPTH, 3, D_MODEL), 0.02),
        "w_ada": nrm(ks[10], (DEPTH, D_MODEL, 9 * D_MODEL), 0.1 * D_MODEL ** -0.5),
        "b_ada": nrm(ks[11], (DEPTH, 9 * D_MODEL), 0.02),
        "w_ff_up": nrm(ks[12], (DEPTH, 2, D_MODEL, 2 * D_FF), D_MODEL ** -0.5),
        "w_ff_down": nrm(ks[13], (DEPTH, 2, D_FF, D_MODEL), D_FF ** -0.5),
        "w_attn_in": nrm(ks[14], (N_ATTN_LAYERS, D_MODEL, ATTN_IN_DIM), D_MODEL ** -0.5),
        "w_attn_out": nrm(ks[15], (N_ATTN_LAYERS, Q_W, D_MODEL), Q_W ** -0.5),
        "q_norm": 1.0 + nrm(ks[16], (N_ATTN_LAYERS, HEAD_DIM), 0.02),
        "k_norm": 1.0 + nrm(ks[17], (N_ATTN_LAYERS, HEAD_DIM), 0.02),
        "w_rec_in": nrm(ks[18], (N_REC_LAYERS, D_MODEL, 4 * D_MODEL), D_MODEL ** -0.5),
        "w_rec_out": nrm(ks[19], (N_REC_LAYERS, D_MODEL, D_MODEL), D_MODEL ** -0.5),
        "rec_out_norm": 1.0 + nrm(ks[20], (N_REC_LAYERS, D_MODEL), 0.02),
        "lb_logits": nrm(ks[21], (DEPTH, D_MODEL), 0.5),
    }


def reference(x_prompt, x_sample, c_prompt, c_sample, cache_k, cache_v, cache_kidx, state_hgrn, page_table,
              norm_gain, w_ada, b_ada, w_ff_up, w_ff_down, w_attn_in, w_attn_out, q_norm, k_norm,
              w_rec_in, w_rec_out, rec_out_norm, lb_logits):
    lb_all = hgrn_lower_bounds(lb_logits)

    def mix_prompt(i, h):
        j = i // N_MIXERS
        if i % N_MIXERS == 0:
            return dsa_prompt(h, w_attn_in[j], w_attn_out[j], q_norm[j], k_norm[j])
        s0 = jnp.zeros((h.shape[0], N_REC_HEADS, REC_EXPAND, REC_HEAD_V), F32)
        o, s = hgrn2_mix(h, w_rec_in[j], w_rec_out[j], rec_out_norm[j], lb_all[i], s0, min(REC_CHUNK, h.shape[1]))
        return o, (s.astype(state_hgrn.dtype),)

    def mix_sample(i, h):
        j = i // N_MIXERS
        if i % N_MIXERS == 0:
            return dsa_sample(h, w_attn_in[j], w_attn_out[j], q_norm[j], k_norm[j],
                              cache_k[j], cache_v[j], cache_kidx[j], page_table)
        o, s = hgrn2_mix(h, w_rec_in[j], w_rec_out[j], rec_out_norm[j], lb_all[i], state_hgrn[j], h.shape[1])
        return o, (s.astype(state_hgrn.dtype),)

    y_prompt, st_p = trunk(x_prompt, c_prompt, mix_prompt, norm_gain, w_ada, b_ada, w_ff_up, w_ff_down)
    y_sample, st_s = trunk(x_sample, c_sample, mix_sample, norm_gain, w_ada, b_ada, w_ff_up, w_ff_down)

    attn_p = [st_p[i] for i in range(0, DEPTH, N_MIXERS)]
    rec_p = [st_p[i] for i in range(1, DEPTH, N_MIXERS)]
    attn_s = [st_s[i] for i in range(0, DEPTH, N_MIXERS)]
    rec_s = [st_s[i] for i in range(1, DEPTH, N_MIXERS)]
    bp, sp = x_prompt.shape[0], x_prompt.shape[1]
    pages_p = sp // PAGE_SIZE
    k_prompt = jnp.stack([s[0] for s in attn_p]).reshape(N_ATTN_LAYERS, bp, pages_p, PAGE_SIZE, N_KV_HEADS, HEAD_DIM)
    v_prompt = jnp.stack([s[1] for s in attn_p]).reshape(N_ATTN_LAYERS, bp, pages_p, PAGE_SIZE, N_KV_HEADS, HEAD_DIM)
    kidx_prompt = jnp.stack([s[2] for s in attn_p]).reshape(N_ATTN_LAYERS, bp, pages_p, PAGE_SIZE, IDX_DIM)
    state_prompt = jnp.stack([s[0] for s in rec_p])
    k_sample = jnp.stack([s[0] for s in attn_s])
    v_sample = jnp.stack([s[1] for s in attn_s])
    kidx_sample = jnp.stack([s[2] for s in attn_s])
    state_sample = jnp.stack([s[0] for s in rec_s])
    return (y_prompt, y_sample, k_prompt, v_prompt, kidx_prompt, state_prompt, k_sample, v_sample, kidx_sample, state_sample)
```

```python
import functools

import jax
import jax.numpy as jnp
from jax import lax
from jax.experimental import pallas as pl
from jax.experimental.pallas import tpu as pltpu

F32 = jnp.float32
BF16 = jnp.bfloat16
I32 = jnp.int32

LANES = 128
SUBLANES = 8
VMEM_LIMIT = 56 * 1024 * 1024

N_HEADS = 16
N_KV_HEADS = 4
N_IDX_HEADS = 16
IDX_DIM = 64
INDEX_TOPK = 256
ROPE_THETA = 10000.0
REC_CHUNK = 64
EPS = 1e-6
PAGE = 128
NEG = -1e30
IMIN = -2 ** 31


def _cparams(*sem):
    return pltpu.CompilerParams(dimension_semantics=sem, vmem_limit_bytes=VMEM_LIMIT)


def _rows(mod, tm):
    rm = mod.shape[0]
    if rm == 1 or rm == tm:
        return mod
    return jnp.concatenate([mod] * (tm // rm), axis=0)


def _ada_norm(x, gain, shift, scale):
    tm = x.shape[0]
    y = x * lax.rsqrt(jnp.mean(x * x, axis=-1, keepdims=True) + EPS) * gain
    return y * (1.0 + _rows(scale, tm)) + _rows(shift, tm)


def _silu(x):
    return x * jax.nn.sigmoid(x)


def _ada_mod_kernel(c_ref, w_ref, b_ref, o_ref):
    cs = _silu(c_ref[...]).astype(BF16)
    o_ref[...] = jnp.dot(cs, w_ref[...].astype(BF16), preferred_element_type=F32) + b_ref[...]


def _ada_mod(c_all, w_ada, b_ada):
    depth, d, n = w_ada.shape
    m = c_all.shape[0]
    tn = 2048
    return pl.pallas_call(
        _ada_mod_kernel,
        grid=(depth, n // tn),
        in_specs=[
            pl.BlockSpec((m, d), lambda i, j: (0, 0)),
            pl.BlockSpec((None, d, tn), lambda i, j: (i, 0, j)),
            pl.BlockSpec((None, 1, tn), lambda i, j: (i, 0, j)),
        ],
        out_specs=pl.BlockSpec((None, m, tn), lambda i, j: (i, 0, j)),
        out_shape=jax.ShapeDtypeStruct((depth, m, n), F32),
        compiler_params=_cparams("parallel", "parallel"),
        name="ada_mod",
    )(c_all, w_ada, b_ada.reshape(depth, 1, n))


class _Mod:
    def __init__(self, arr, per_row, tiles_per_seq):
        self.arr = arr
        self.per_row = per_row
        self.tps = tiles_per_seq

    def spec(self, layer, k, tn=None, n_axis=False):
        d = self.arr.shape[-1]
        tn = d if tn is None else tn
        if self.per_row:
            rows = self.arr.shape[2]
            if n_axis:
                return pl.BlockSpec((None, None, rows, tn), lambda m, n: (layer, k, 0, n))
            return pl.BlockSpec((None, None, rows, tn), lambda m, *_: (layer, k, 0, 0))
        tps = self.tps
        if n_axis:
            return pl.BlockSpec((None, None, None, 1, tn), lambda m, n: (layer, m // tps, k, 0, n))
        return pl.BlockSpec((None, None, None, 1, tn), lambda m, *_: (layer, m // tps, k, 0, 0))


def _ffn_kernel(x_ref, sh_ref, sc_ref, gt_ref, gain_ref, wa_ref, wb_ref, wd_ref, o_ref, h_ref):
    j = pl.program_id(1)

    @pl.when(j == 0)
    def _():
        h_ref[...] = _ada_norm(x_ref[...], gain_ref[...], sh_ref[...], sc_ref[...]).astype(BF16)
        o_ref[...] = jnp.zeros_like(o_ref)

    h = h_ref[...]
    a = jnp.dot(h, wa_ref[...].astype(BF16), preferred_element_type=F32)
    b = jnp.dot(h, wb_ref[...].astype(BF16), preferred_element_type=F32)
    g = (_silu(a) * b).astype(BF16)
    o_ref[...] += jnp.dot(g, wd_ref[...].astype(BF16), preferred_element_type=F32)

    @pl.when(j == pl.num_programs(1) - 1)
    def _():
        tm = o_ref.shape[0]
        o_ref[...] = x_ref[...] + 0.5 * (1.0 + _rows(gt_ref[...], tm)) * o_ref[...]


def _ffn(x, mod, layer, sub, gain, w_up, w_down, ffn_idx, tm, tf):
    m, d = x.shape
    f = w_down.shape[2]
    nf = f // tf
    return pl.pallas_call(
        _ffn_kernel,
        grid=(m // tm, nf),
        in_specs=[
            pl.BlockSpec((tm, d), lambda i, j: (i, 0), pipeline_mode=pl.Buffered(1)),
            mod.spec(layer, 3 * sub + 0),
            mod.spec(layer, 3 * sub + 1),
            mod.spec(layer, 3 * sub + 2),
            pl.BlockSpec((None, None, 1, d), lambda i, j: (layer, sub, 0, 0)),
            pl.BlockSpec((None, None, d, tf), lambda i, j: (layer, ffn_idx, 0, j)),
            pl.BlockSpec((None, None, d, tf), lambda i, j: (layer, ffn_idx, 0, j + nf)),
            pl.BlockSpec((None, None, tf, d), lambda i, j: (layer, ffn_idx, j, 0)),
        ],
        out_specs=pl.BlockSpec((tm, d), lambda i, j: (i, 0)),
        out_shape=jax.ShapeDtypeStruct((m, d), F32),
        scratch_shapes=[pltpu.VMEM((tm, d), BF16)],
        compiler_params=_cparams("parallel", "arbitrary"),
        name="ffn",
    )(x, mod.arr, mod.arr, mod.arr, gain, w_up, w_up, w_down)


def _prenorm_kernel(x_ref, sh_ref, sc_ref, gain_ref, h_ref):
    h_ref[...] = _ada_norm(x_ref[...], gain_ref[...], sh_ref[...], sc_ref[...]).astype(BF16)


def _prenorm(x, mod, layer, sub, gain, tm):
    m, d = x.shape
    return pl.pallas_call(
        _prenorm_kernel,
        grid=(m // tm,),
        in_specs=[
            pl.BlockSpec((tm, d), lambda i: (i, 0)),
            mod.spec(layer, 3 * sub + 0),
            mod.spec(layer, 3 * sub + 1),
            pl.BlockSpec((None, None, 1, d), lambda i: (layer, sub, 0, 0)),
        ],
        out_specs=pl.BlockSpec((tm, d), lambda i: (i, 0)),
        out_shape=jax.ShapeDtypeStruct((m, d), BF16),
        compiler_params=_cparams("parallel"),
        name="prenorm",
    )(x, mod.arr, mod.arr, gain)


def _rot_half(y, half):
    if 2 * half == LANES:
        return pltpu.roll(y, half, 1)
    lane = lax.broadcasted_iota(I32, y.shape, 1)
    return jnp.where(lane % (2 * half) < half, pltpu.roll(y, LANES - half, 1), pltpu.roll(y, half, 1))


def _proj_rope_kernel(a_ref, w_ref, g_ref, ca_ref, cb_ref, *o_refs, norm, half):
    y = jnp.dot(a_ref[...], w_ref[...].astype(BF16), preferred_element_type=F32)
    ca = ca_ref[...]
    cb = cb_ref[...]
    for c in range(y.shape[1] // LANES):
        yc = y[:, c * LANES:(c + 1) * LANES]
        if norm:
            yc = yc * lax.rsqrt(jnp.mean(yc * yc, axis=-1, keepdims=True) + EPS) * g_ref[...]
        r = yc * ca + _rot_half(yc, half) * cb
        for o_ref in o_refs:
            o_ref[:, c * LANES:(c + 1) * LANES] = r.astype(o_ref.dtype)


def _proj_rope(a, w, col0, n, gain, tab_a, tab_b, out_dtypes, *, norm, half, tm, tn):
    m, k = a.shape
    ntab = tab_a.shape[0] // tm
    cb0 = col0 // tn
    outs = pl.pallas_call(
        functools.partial(_proj_rope_kernel, norm=norm, half=half),
        grid=(m // tm, n // tn),
        in_specs=[
            pl.BlockSpec((tm, k), lambda i, j: (i, 0)),
            pl.BlockSpec((k, tn), lambda i, j: (0, cb0 + j)),
            pl.BlockSpec((1, LANES), lambda i, j: (0, 0)),
            pl.BlockSpec((tm, LANES), lambda i, j: (i % ntab, 0)),
            pl.BlockSpec((tm, LANES), lambda i, j: (i % ntab, 0)),
        ],
        out_specs=[pl.BlockSpec((tm, tn), lambda i, j: (i, j)) for _ in out_dtypes],
        out_shape=[jax.ShapeDtypeStruct((m, n), dt) for dt in out_dtypes],
        compiler_params=_cparams("parallel", "parallel"),
        name="proj_rope",
    )(a, w, gain, tab_a, tab_b)
    return outs


def _proj_plain_kernel(a_ref, w_ref, *o_refs, act):
    y = jnp.dot(a_ref[...], w_ref[...].astype(BF16), preferred_element_type=F32)
    if act == "silu":
        y = _silu(y)
    for o_ref in o_refs:
        o_ref[...] = y.astype(o_ref.dtype)


def _proj_plain(a, w, col0, n, out_dtypes, *, act, tm, tn):
    m, k = a.shape
    cb0 = col0 // tn
    return pl.pallas_call(
        functools.partial(_proj_plain_kernel, act=act),
        grid=(m // tm, n // tn),
        in_specs=[
            pl.BlockSpec((tm, k), lambda i, j: (i, 0)),
            pl.BlockSpec((k, tn), lambda i, j: (0, cb0 + j)),
        ],
        out_specs=[pl.BlockSpec((tm, tn), lambda i, j: (i, j)) for _ in out_dtypes],
        out_shape=[jax.ShapeDtypeStruct((m, n), dt) for dt in out_dtypes],
        compiler_params=_cparams("parallel", "parallel"),
        name="proj_plain",
    )(a, w)


def _proj_forget_kernel(a_ref, w_ref, loglb_ref, log1mlb_ref, lf_ref, k_ref):
    fr = jnp.dot(a_ref[...], w_ref[...].astype(BF16), preferred_element_type=F32)
    t = jnp.log1p(jnp.exp(-jnp.abs(fr)))
    ls_pos = jnp.minimum(fr, 0.0) - t
    ls_neg = jnp.minimum(-fr, 0.0) - t
    a = loglb_ref[...]
    c = log1mlb_ref[...] + ls_pos
    lf_ref[...] = jnp.maximum(a, c) + jnp.log1p(jnp.exp(-jnp.abs(a - c)))
    k_ref[...] = jnp.exp(log1mlb_ref[...] + ls_neg)


def _proj_forget(a, w, col0, n, log_lb, log_1mlb, *, tm, tn):
    m, k = a.shape
    cb0 = col0 // tn
    return pl.pallas_call(
        _proj_forget_kernel,
        grid=(m // tm, n // tn),
        in_specs=[
            pl.BlockSpec((tm, k), lambda i, j: (i, 0)),
            pl.BlockSpec((k, tn), lambda i, j: (0, cb0 + j)),
            pl.BlockSpec((1, tn), lambda i, j: (0, j)),
            pl.BlockSpec((1, tn), lambda i, j: (0, j)),
        ],
        out_specs=[pl.BlockSpec((tm, tn), lambda i, j: (i, j)) for _ in range(2)],
        out_shape=[jax.ShapeDtypeStruct((m, n), F32) for _ in range(2)],
        compiler_params=_cparams("parallel", "parallel"),
        name="proj_forget",
    )(a, w, log_lb, log_1mlb)


def _mm_res_kernel(a_ref, w_ref, x_ref, gt_ref, o_ref):
    y = jnp.dot(a_ref[...], w_ref[...].astype(BF16), preferred_element_type=F32)
    o_ref[...] = x_ref[...] + (1.0 + _rows(gt_ref[...], y.shape[0])) * y


def _mm_res(a, w, x, mod, layer, k_gate, *, tm, tn):
    m, k = a.shape
    n = w.shape[1]
    return pl.pallas_call(
        _mm_res_kernel,
        grid=(m // tm, n // tn),
        in_specs=[
            pl.BlockSpec((tm, k), lambda i, j: (i, 0)),
            pl.BlockSpec((k, tn), lambda i, j: (0, j)),
            pl.BlockSpec((tm, tn), lambda i, j: (i, j)),
            mod.spec(layer, k_gate, tn=tn, n_axis=True),
        ],
        out_specs=pl.BlockSpec((tm, tn), lambda i, j: (i, j)),
        out_shape=jax.ShapeDtypeStruct((m, n), F32),
        compiler_params=_cparams("parallel", "parallel"),
        name="mm_res",
    )(a, w, x, mod.arr)


def _sort_key(score):
    bits = pltpu.bitcast(score + 0.0, I32)
    return jnp.where(bits < 0, bits ^ jnp.int32(0x7FFFFFFF), bits)


def _attn_prompt_kernel(q_ref, qi_ref, wi_ref, k_ref, v_ref, ki_ref, o_ref,
                        key_ref, bias_ref, wb_ref, j_ref, *, topk, idx_bits):
    tq = q_ref.shape[0]
    dh = LANES
    rep = N_HEADS // N_KV_HEADS
    qb = pl.program_id(1)
    nkb = qb + 1
    row = lax.broadcasted_iota(I32, (tq, LANES), 0)
    lane = lax.broadcasted_iota(I32, (tq, LANES), 1)
    imin = jnp.int32(IMIN)

    wi = wi_ref[...]
    for h in range(N_IDX_HEADS):
        wcol = wi[:, IDX_DIM + h:IDX_DIM + h + 1] * (IDX_DIM ** -0.5)
        wb_ref[h] = jnp.broadcast_to(wcol, (tq, LANES))

    def score_body(kb, c):
        kib = ki_ref[kb, :, 0:IDX_DIM]
        acc = jnp.zeros((tq, LANES), F32)
        for h in range(N_IDX_HEADS):
            lg = lax.dot_general(qi_ref[:, h * IDX_DIM:(h + 1) * IDX_DIM], kib,
                                 (((1,), (1,)), ((), ())), preferred_element_type=F32)
            acc = acc + wb_ref[h] * jnp.maximum(lg, 0.0)
        adm = jnp.logical_or(kb < qb, lane <= row)
        key_ref[kb] = jnp.where(adm, _sort_key(acc), imin)
        return c

    lax.fori_loop(0, nkb, score_body, 0)

    def count(pred):
        def body(kb, part):
            return part + jnp.where(pred(kb, key_ref[kb]), 1.0, 0.0)
        part = lax.fori_loop(0, nkb, body, jnp.zeros((tq, LANES), F32))
        return jnp.sum(part, axis=1, keepdims=True)

    def bit_body(i, tu):
        cu = tu | jnp.left_shift(jnp.int32(1), 31 - i)
        cand = cu ^ imin
        cnt = count(lambda kb, k: k >= cand)
        return jnp.where(cnt >= topk, cu, tu)

    thr = lax.fori_loop(0, 32, bit_body, jnp.zeros((tq, 1), I32)) ^ imin
    cnt_gt = count(lambda kb, k: k > thr)
    cnt_ge = count(lambda kb, k: k >= thr)
    need = topk - cnt_gt
    tie = jnp.logical_and(cnt_ge > topk, thr > imin)

    j_ref[...] = jnp.full(j_ref.shape, 2 ** idx_bits, I32)

    @pl.when(jnp.max(jnp.where(tie, 1.0, 0.0)) > 0.0)
    def _():
        def jbit_body(i, jv):
            cand = jv | jnp.left_shift(jnp.int32(1), idx_bits - 1 - i)
            c = count(lambda kb, k: jnp.logical_and(k == thr, kb * LANES + lane < cand))
            return jnp.where(c < need, cand, jv)
        jv = lax.fori_loop(0, idx_bits, jbit_body, jnp.zeros((tq, 1), I32))
        j_ref[...] = jnp.broadcast_to(jv, j_ref.shape)

    jv = j_ref[...]

    def bias_body(kb, c):
        k = key_ref[kb]
        sel = jnp.logical_or(k > thr, jnp.logical_and(k == thr, kb * LANES + lane <= jv))
        sel = jnp.logical_and(sel, k > imin)
        bias_ref[kb] = jnp.where(sel, 0.0, NEG)
        return c

    lax.fori_loop(0, nkb, bias_body, 0)

    scale = dh ** -0.5
    for g in range(N_KV_HEADS):
        qg = jnp.concatenate([q_ref[:, (g * rep + j) * dh:(g * rep + j + 1) * dh] for j in range(rep)], axis=0)

        def kv_body(kb, carry, g=g, qg=qg):
            m, l, acc = carry
            kblk = k_ref[kb, :, g * dh:(g + 1) * dh]
            vblk = v_ref[kb, :, g * dh:(g + 1) * dh]
            s = lax.dot_general(qg, kblk, (((1,), (1,)), ((), ())), preferred_element_type=F32) * scale
            s = s + jnp.concatenate([bias_ref[kb]] * rep, axis=0)
            m_new = jnp.maximum(m, jnp.max(s, axis=1, keepdims=True))
            alpha = jnp.exp(m - m_new)
            p = jnp.exp(s - m_new)
            l = alpha * l + jnp.sum(p, axis=1, keepdims=True)
            acc = alpha * acc + jnp.dot(p.astype(BF16), vblk, preferred_element_type=F32)
            return m_new, l, acc

        m0 = jnp.full((rep * tq, 1), NEG, F32)
        l0 = jnp.zeros((rep * tq, 1), F32)
        a0 = jnp.zeros((rep * tq, dh), F32)
        _, l, acc = lax.fori_loop(0, nkb, kv_body, (m0, l0, a0))
        o = acc / l
        for j in range(rep):
            o_ref[:, (g * rep + j) * dh:(g * rep + j + 1) * dh] = o[j * tq:(j + 1) * tq].astype(o_ref.dtype)


def _attn_prompt(q, qi, kiwi_f32, k_bf, v_bf, kiwi_bf, batch, seq):
    m = q.shape[0]
    tq = PAGE
    nq = seq // tq
    topk = min(INDEX_TOPK, seq // 4)
    kvw = k_bf.shape[1]
    k4 = k_bf.reshape(batch, nq, tq, kvw)
    v4 = v_bf.reshape(batch, nq, tq, kvw)
    ki4 = kiwi_bf.reshape(batch, nq, tq, LANES)
    return pl.pallas_call(
        functools.partial(_attn_prompt_kernel, topk=topk, idx_bits=max(1, (seq - 1).bit_length())),
        grid=(batch, nq),
        in_specs=[
            pl.BlockSpec((tq, q.shape[1]), lambda b, i: (b * nq + i, 0)),
            pl.BlockSpec((tq, qi.shape[1]), lambda b, i: (b * nq + i, 0)),
            pl.BlockSpec((tq, LANES), lambda b, i: (b * nq + i, 0)),
            pl.BlockSpec((None, nq, tq, kvw), lambda b, i: (b, 0, 0, 0)),
            pl.BlockSpec((None, nq, tq, kvw), lambda b, i: (b, 0, 0, 0)),
            pl.BlockSpec((None, nq, tq, LANES), lambda b, i: (b, 0, 0, 0)),
        ],
        out_specs=pl.BlockSpec((tq, q.shape[1]), lambda b, i: (b * nq + i, 0)),
        out_shape=jax.ShapeDtypeStruct((m, q.shape[1]), BF16),
        scratch_shapes=[
            pltpu.VMEM((nq, tq, LANES), I32),
            pltpu.VMEM((nq, tq, LANES), F32),
            pltpu.VMEM((N_IDX_HEADS, tq, LANES), F32),
            pltpu.VMEM((tq, LANES), I32),
        ],
        compiler_params=_cparams("parallel", "arbitrary"),
        name="attn_prompt",
    )(q, qi, kiwi_f32, k4, v4, ki4)


def _idx_sample_kernel(pt_ref, qit_ref, a_ref, kinew_ref, *refs, n_pages, n_tok):
    page_refs = refs[:n_pages]
    o_ref = refs[n_pages]
    qt = qit_ref[...]
    a = a_ref[...]
    rows = a.shape[0]
    lane = lax.broadcasted_iota(I32, (rows, LANES), 1)
    row = lax.broadcasted_iota(I32, (rows, LANES), 0)
    for p in range(n_pages + 1):
        if p < n_pages:
            kip = page_refs[p][...].astype(BF16)
        else:
            kip = jnp.concatenate([kinew_ref[...], jnp.zeros((PAGE - kinew_ref.shape[0], IDX_DIM), BF16)], axis=0)
        lg = jnp.dot(kip, qt, preferred_element_type=F32)
        sc = lax.dot_general(a, jnp.maximum(lg, 0.0), (((1,), (1,)), ((), ())),
                             precision=lax.Precision.HIGHEST, preferred_element_type=F32)
        key = _sort_key(sc)
        if p == n_pages:
            key = jnp.where(lane <= row % n_tok, key, jnp.int32(IMIN))
        o_ref[:, p * LANES:(p + 1) * LANES] = key


def _idx_sample(page_table, qit, amat, ki_new, cache_kidx_l):
    bs, n_pages = page_table.shape
    n_tok = qit.shape[2] // N_IDX_HEADS
    rows = amat.shape[1]
    width = (n_pages + 1) * LANES
    page_specs = [pl.BlockSpec((None, PAGE, IDX_DIM), lambda b, pt, p=p: (pt[b, p], 0, 0)) for p in range(n_pages)]
    grid_spec = pltpu.PrefetchScalarGridSpec(
        num_scalar_prefetch=1,
        grid=(bs,),
        in_specs=[
            pl.BlockSpec((None,) + qit.shape[1:], lambda b, pt: (b, 0, 0)),
            pl.BlockSpec((None,) + amat.shape[1:], lambda b, pt: (b, 0, 0)),
            pl.BlockSpec((None,) + ki_new.shape[1:], lambda b, pt: (b, 0, 0)),
        ] + page_specs,
        out_specs=pl.BlockSpec((None, rows, width), lambda b, pt: (b, 0, 0)),
    )
    return pl.pallas_call(
        functools.partial(_idx_sample_kernel, n_pages=n_pages, n_tok=n_tok),
        grid_spec=grid_spec,
        out_shape=jax.ShapeDtypeStruct((bs, rows, width), I32),
        compiler_params=_cparams("arbitrary"),
        name="idx_sample",
    )(page_table, qit, amat, ki_new, *([cache_kidx_l] * n_pages))


def _topk_bias_kernel(key_ref, bias_ref, *, topk, idx_bits):
    key = key_ref[...]
    rows, width = key.shape
    imin = jnp.int32(IMIN)
    idx = lax.broadcasted_iota(I32, (rows, width), 1)

    def count(pred):
        return jnp.sum(jnp.where(pred, 1.0, 0.0), axis=1, keepdims=True)

    def bit_body(i, tu):
        cu = tu | jnp.left_shift(jnp.int32(1), 31 - i)
        cnt = count(key >= (cu ^ imin))
        return jnp.where(cnt >= topk, cu, tu)

    thr = lax.fori_loop(0, 32, bit_body, jnp.zeros((rows, 1), I32)) ^ imin
    need = topk - count(key > thr)
    eq = key == thr

    def jbit_body(i, jv):
        cand = jv | jnp.left_shift(jnp.int32(1), idx_bits - 1 - i)
        c = count(jnp.logical_and(eq, idx < cand))
        return jnp.where(c < need, cand, jv)

    jv = lax.fori_loop(0, idx_bits, jbit_body, jnp.zeros((rows, 1), I32))
    sel = jnp.logical_or(key > thr, jnp.logical_and(eq, idx <= jv))
    sel = jnp.logical_and(sel, key > imin)
    bias_ref[...] = jnp.where(sel, 0.0, NEG)


def _topk_bias(keys2d, topk, tr):
    rows, width = keys2d.shape
    return pl.pallas_call(
        functools.partial(_topk_bias_kernel, topk=topk, idx_bits=max(1, (width - 1).bit_length())),
        grid=(rows // tr,),
        in_specs=[pl.BlockSpec((tr, width), lambda i: (i, 0))],
        out_specs=pl.BlockSpec((tr, width), lambda i: (i, 0)),
        out_shape=jax.ShapeDtypeStruct((rows, width), F32),
        compiler_params=_cparams("parallel"),
        name="topk_bias",
    )(keys2d)


def _attn_sample_kernel(pt_ref, q_ref, bias_ref, knew_ref, vnew_ref, *refs, n_pages):
    k_refs = refs[:n_pages]
    v_refs = refs[n_pages:2 * n_pages]
    o_ref = refs[2 * n_pages]
    q = q_ref[...]
    rows, dh = q.shape
    kvw = N_KV_HEADS * dh
    rows_per_group = rows // N_KV_HEADS
    rg = lax.broadcasted_iota(I32, (rows, kvw), 0) // rows_per_group
    cg = lax.broadcasted_iota(I32, (rows, kvw), 1) // dh
    qbd = jnp.where(rg == cg, jnp.concatenate([q] * N_KV_HEADS, axis=1), jnp.zeros((), q.dtype))
    bias8 = bias_ref[...]
    bias = jnp.concatenate([bias8] * (rows // bias8.shape[0]), axis=0)
    pad = PAGE - knew_ref.shape[0]

    s_blocks = []
    for p in range(n_pages + 1):
        if p < n_pages:
            kp = k_refs[p][...].astype(BF16)
        else:
            kp = jnp.concatenate([knew_ref[...], jnp.zeros((pad, kvw), BF16)], axis=0)
        s_blocks.append(lax.dot_general(qbd, kp, (((1,), (1,)), ((), ())), preferred_element_type=F32))
    s = jnp.concatenate(s_blocks, axis=1) * (dh ** -0.5) + bias
    m = jnp.max(s, axis=1, keepdims=True)
    pexp = jnp.exp(s - m)
    l = jnp.sum(pexp, axis=1, keepdims=True)
    pb = pexp.astype(BF16)
    o_all = jnp.zeros((rows, kvw), F32)
    for p in range(n_pages + 1):
        if p < n_pages:
            vp = v_refs[p][...].astype(BF16)
        else:
            vp = jnp.concatenate([vnew_ref[...], jnp.zeros((pad, kvw), BF16)], axis=0)
        o_all = o_all + jnp.dot(pb[:, p * LANES:(p + 1) * LANES], vp, preferred_element_type=F32)
    rg1 = lax.broadcasted_iota(I32, (rows, dh), 0) // rows_per_group
    o = jnp.zeros((rows, dh), F32)
    for g in range(N_KV_HEADS):
        o = o + jnp.where(rg1 == g, o_all[:, g * dh:(g + 1) * dh], 0.0)
    o_ref[...] = (o / l).astype(o_ref.dtype)


def _attn_sample(page_table, q64, bias, k_new, v_new, cache_k_l, cache_v_l):
    bs, n_pages = page_table.shape
    kvw = cache_k_l.shape[-1]
    kspecs = [pl.BlockSpec((None, PAGE, kvw), lambda b, pt, p=p: (pt[b, p], 0, 0)) for p in range(n_pages)]
    vspecs = [pl.BlockSpec((None, PAGE, kvw), lambda b, pt, p=p: (pt[b, p], 0, 0)) for p in range(n_pages)]
    grid_spec = pltpu.PrefetchScalarGridSpec(
        num_scalar_prefetch=1,
        grid=(bs,),
        in_specs=[
            pl.BlockSpec((None,) + q64.shape[1:], lambda b, pt: (b, 0, 0)),
            pl.BlockSpec((None,) + bias.shape[1:], lambda b, pt: (b, 0, 0)),
            pl.BlockSpec((None,) + k_new.shape[1:], lambda b, pt: (b, 0, 0)),
            pl.BlockSpec((None,) + v_new.shape[1:], lambda b, pt: (b, 0, 0)),
        ] + kspecs + vspecs,
        out_specs=pl.BlockSpec((None,) + q64.shape[1:], lambda b, pt: (b, 0, 0)),
    )
    return pl.pallas_call(
        functools.partial(_attn_sample_kernel, n_pages=n_pages),
        grid_spec=grid_spec,
        out_shape=jax.ShapeDtypeStruct(q64.shape, BF16),
        compiler_params=_cparams("arbitrary"),
        name="attn_sample",
    )(page_table, q64, bias, k_new, v_new, *([cache_k_l] * n_pages), *([cache_v_l] * n_pages))


def _gla_chunk(q, k, v, g, state, oacc_ref, n_src):
    c = q.shape[0]
    tri = (lax.broadcasted_iota(I32, (c, c), 0) >= lax.broadcasted_iota(I32, (c, c), 1)).astype(F32)
    b = jnp.dot(tri, g, precision=lax.Precision.HIGHEST, preferred_element_type=F32)
    oacc_ref[...] = jnp.dot((q * jnp.exp(b)).astype(BF16), state.astype(BF16), preferred_element_type=F32)
    ones = jnp.ones((LANES, LANES), BF16)
    t_idx = lax.broadcasted_iota(I32, (c, LANES), 0)
    for s in range(n_src):
        t0 = (s // SUBLANES) * SUBLANES
        msk = t_idx[t0:] >= s
        decay = jnp.exp(jnp.where(msk, b[t0:] - b[s:s + 1], 0.0))
        x = jnp.where(msk, q[t0:] * k[s:s + 1] * decay, 0.0)
        a_s = jnp.dot(x.astype(BF16), ones, preferred_element_type=F32)
        oacc_ref[t0:, :] += a_s * v[s:s + 1]
    b_end = b[c - 1:c]
    eye = lax.broadcasted_iota(I32, (LANES, LANES), 0) == lax.broadcasted_iota(I32, (LANES, LANES), 1)
    decay_col = jnp.sum(jnp.where(eye, jnp.broadcast_to(jnp.exp(b_end), (LANES, LANES)), 0.0), axis=1, keepdims=True)
    kd = (k * jnp.exp(b_end - b)).astype(BF16)
    upd = lax.dot_general(kd, v.astype(BF16), (((0,), (0,)), ((), ())), preferred_element_type=F32)
    return decay_col * state + upd


def _gla_out(o, sg, gain):
    return (o * lax.rsqrt(jnp.mean(o * o, axis=-1, keepdims=True) + EPS) * gain * sg)


def _gla_prompt_kernel(q_ref, k_ref, v_ref, g_ref, sg_ref, gain_ref, y_ref, s_ref, state_ref, oacc_ref, *, chunk):
    ci = pl.program_id(2)

    @pl.when(ci == 0)
    def _():
        state_ref[...] = jnp.zeros_like(state_ref)

    def body(c, carry):
        r = pl.ds(pl.multiple_of(c * chunk, chunk), chunk)
        st = _gla_chunk(q_ref[r, :], k_ref[r, :], v_ref[r, :], g_ref[r, :], state_ref[...], oacc_ref, chunk)
        state_ref[...] = st
        y_ref[r, :] = _gla_out(oacc_ref[...], sg_ref[r, :], gain_ref[...]).astype(y_ref.dtype)
        return carry

    lax.fori_loop(0, q_ref.shape[0] // chunk, body, 0)

    @pl.when(ci == pl.num_programs(2) - 1)
    def _():
        s_ref[...] = state_ref[...]


def _gla_prompt(qs, kk, vv, lf, sg, gain, batch, seq, chunk, tb):
    m, d = qs.shape
    nh = d // LANES
    nc = seq // tb
    tok = pl.BlockSpec((tb, LANES), lambda b, h, c: (b * nc + c, h))
    return pl.pallas_call(
        functools.partial(_gla_prompt_kernel, chunk=chunk),
        grid=(batch, nh, nc),
        in_specs=[tok, tok, tok, tok, tok, pl.BlockSpec((1, LANES), lambda b, h, c: (0, h))],
        out_specs=[tok, pl.BlockSpec((None, None, LANES, LANES), lambda b, h, c: (b, h, 0, 0))],
        out_shape=[jax.ShapeDtypeStruct((m, d), BF16), jax.ShapeDtypeStruct((batch, nh, LANES, LANES), F32)],
        scratch_shapes=[pltpu.VMEM((LANES, LANES), F32), pltpu.VMEM((chunk, LANES), F32)],
        compiler_params=_cparams("parallel", "parallel", "arbitrary"),
        name="gla_prompt",
    )(qs, kk, vv, lf, sg, gain)


def _gla_sample_kernel(q_ref, k_ref, v_ref, g_ref, sg_ref, gain_ref, s0_ref, y_ref, s_ref, oacc_ref, *, n_tok):
    nh = s0_ref.shape[0]
    for h in range(nh):
        c = slice(h * LANES, (h + 1) * LANES)
        st = _gla_chunk(q_ref[:, c], k_ref[:, c], v_ref[:, c], g_ref[:, c], s0_ref[h], oacc_ref, n_tok)
        s_ref[h] = st
        y_ref[:, c] = _gla_out(oacc_ref[...], sg_ref[:, c], gain_ref[:, c]).astype(y_ref.dtype)


def _gla_sample(qs, kk, vv, lf, sg, gain, state0, n_tok):
    bs, tp, d = qs.shape
    nh = state0.shape[1]
    tok = pl.BlockSpec((None, tp, d), lambda b: (b, 0, 0))
    st = pl.BlockSpec((None, nh, LANES, LANES), lambda b: (b, 0, 0, 0))
    return pl.pallas_call(
        functools.partial(_gla_sample_kernel, n_tok=n_tok),
        grid=(bs,),
        in_specs=[tok, tok, tok, tok, tok, pl.BlockSpec((1, d), lambda b: (0, 0)), st],
        out_specs=[tok, st],
        out_shape=[jax.ShapeDtypeStruct((bs, tp, d), BF16), jax.ShapeDtypeStruct(state0.shape, F32)],
        scratch_shapes=[pltpu.VMEM((tp, LANES), F32)],
        compiler_params=_cparams("parallel"),
        name="gla_sample",
    )(qs, kk, vv, lf, sg, gain, state0)


def _rope_tables(pos, d):
    inv = ROPE_THETA ** (-jnp.arange(0, d, 2, dtype=F32) / d)
    ang = pos.astype(F32)[:, None] * inv[None, :]
    cos, sin = jnp.cos(ang), jnp.sin(ang)
    reps = LANES // d
    a = jnp.tile(jnp.concatenate([cos, cos], axis=1), (1, reps))
    b = jnp.tile(jnp.concatenate([-sin, sin], axis=1), (1, reps))
    return a, b


def _kiwi_tables(pos):
    a64, b64 = _rope_tables(pos, IDX_DIM)
    p = pos.shape[0]
    lane = jnp.arange(IDX_DIM)
    wi_scale = jnp.where(lane < N_IDX_HEADS, N_IDX_HEADS ** -0.5, 0.0).astype(F32)
    a = jnp.concatenate([a64[:, :IDX_DIM], jnp.broadcast_to(wi_scale, (p, IDX_DIM))], axis=1)
    b = jnp.concatenate([b64[:, :IDX_DIM], jnp.zeros((p, IDX_DIM), F32)], axis=1)
    return a, b


def _attn_projections(h, w_in, w_kiwi, q_gain, k_gain, pos, tm):
    dh = LANES
    qw = N_HEADS * dh
    kvw = N_KV_HEADS * dh
    qiw = N_IDX_HEADS * IDX_DIM
    a128, b128 = _rope_tables(pos, dh)
    a64, b64 = _rope_tables(pos, IDX_DIM)
    akw, bkw = _kiwi_tables(pos)
    tn = 512
    (q,) = _proj_rope(h, w_in, 0, qw, q_gain, a128, b128, [BF16], norm=True, half=dh // 2, tm=tm, tn=tn)
    k32, k16 = _proj_rope(h, w_in, qw, kvw, k_gain, a128, b128, [F32, BF16], norm=True, half=dh // 2, tm=tm, tn=tn)
    v32, v16 = _proj_plain(h, w_in, qw + kvw, kvw, [F32, BF16], act=None, tm=tm, tn=tn)
    (qi,) = _proj_rope(h, w_in, qw + 2 * kvw, qiw, q_gain, a64, b64, [BF16], norm=False, half=IDX_DIM // 2, tm=tm, tn=tn)
    kw32, kw16 = _proj_rope(h, w_kiwi, 0, LANES, q_gain, akw, bkw, [F32, BF16], norm=False, half=IDX_DIM // 2,
                            tm=tm, tn=LANES)
    return q, qi, k32, k16, v32, v16, kw32, kw16


def _hgrn_projections(h, w_in, log_lb, log_1mlb, tm):
    d = h.shape[1]
    tn = 512
    (qs,) = _proj_plain(h, w_in, 0, d, [F32], act="silu", tm=tm, tn=tn)
    lf, kk = _proj_forget(h, w_in, d, d, log_lb, log_1mlb, tm=tm, tn=tn)
    (vv,) = _proj_plain(h, w_in, 2 * d, d, [F32], act=None, tm=tm, tn=tn)
    (sg,) = _proj_plain(h, w_in, 3 * d, d, [F32], act="silu", tm=tm, tn=tn)
    return qs, kk, vv, lf, sg


def kernel(x_prompt, x_sample, c_prompt, c_sample, cache_k, cache_v, cache_kidx, state_hgrn, page_table, norm_gain, w_ada, b_ada, w_ff_up, w_ff_down, w_attn_in, w_attn_out, q_norm, k_norm, w_rec_in, w_rec_out, rec_out_norm, lb_logits):
    batch, seq, d = x_prompt.shape
    bs, n_tok, _ = x_sample.shape
    depth = w_ada.shape[0]
    n_pages = page_table.shape[1]
    past = n_pages * PAGE
    dh = LANES
    kvw = N_KV_HEADS * dh

    c_all = jnp.concatenate([c_prompt, jnp.zeros((SUBLANES - batch, d), F32), c_sample], axis=0)
    mod = _ada_mod(c_all, w_ada, b_ada)
    tm_p = min(1024, seq)
    mod_p = _Mod(mod[:, :batch].reshape(depth, batch, 9, 1, d), False, seq // tm_p)
    mod_s = _Mod(mod[:, SUBLANES:].reshape(depth, bs, 9, d).transpose(0, 2, 1, 3), True, None)

    gains = norm_gain.reshape(depth, 3, 1, d)
    p = jnp.exp(lb_logits - jnp.max(lb_logits, axis=0, keepdims=True))
    p = p / jnp.sum(p, axis=0, keepdims=True)
    lb_all = jnp.cumsum(p, axis=0) - p[0]

    xp = x_prompt.reshape(batch * seq, d)
    xs = x_sample.transpose(1, 0, 2).reshape(n_tok * bs, d)
    pos_p = jnp.arange(seq)
    pos_s = jnp.repeat(past + jnp.arange(n_tok), bs)
    ms = n_tok * bs

    outs_p, outs_s = {}, {}
    for i in range(depth):
        j = i // 2
        xp = _ffn(xp, mod_p, i, 0, gains, w_ff_up, w_ff_down, 0, tm_p, 256)
        xs = _ffn(xs, mod_s, i, 0, gains, w_ff_up, w_ff_down, 0, ms, 512)
        hp = _prenorm(xp, mod_p, i, 1, gains, tm_p)
        hs = _prenorm(xs, mod_s, i, 1, gains, bs)
        if i % 2 == 0:
            w_in = w_attn_in[j]
            col = N_HEADS * dh + 2 * kvw + N_IDX_HEADS * IDX_DIM
            w_kiwi = jnp.pad(w_in[:, col:], ((0, 0), (0, LANES - (w_in.shape[1] - col))))
            qg = q_norm[j].reshape(1, dh)
            kg = k_norm[j].reshape(1, dh)
            q, qi, k32, k16, v32, v16, kw32, kw16 = _attn_projections(hp, w_in, w_kiwi, qg, kg, pos_p, tm_p // 2)
            o = _attn_prompt(q, qi, kw32, k16, v16, kw16, batch, seq)
            xp = _mm_res(o, w_attn_out[j], xp, mod_p, i, 5, tm=tm_p, tn=512)
            outs_p[i] = (k32, v32, kw32[:, :IDX_DIM])
            q, qi, k32, k16, v32, v16, kw32, kw16 = _attn_projections(hs, w_in, w_kiwi, qg, kg, pos_s, bs)

            def bmajor(a):
                return a.reshape(n_tok, bs, a.shape[1]).transpose(1, 0, 2)

            qi_b = bmajor(qi).reshape(bs, n_tok, N_IDX_HEADS, IDX_DIM)
            qit = qi_b.transpose(0, 3, 2, 1).reshape(bs, IDX_DIM, N_IDX_HEADS * n_tok)
            wi_b = bmajor(kw32)[:, :, IDX_DIM:IDX_DIM + N_IDX_HEADS] * (IDX_DIM ** -0.5)
            amat = (wi_b[:, :, :, None] * jnp.eye(n_tok, dtype=F32)[None, :, None, :]).reshape(bs, n_tok, -1)
            amat = jnp.concatenate([amat, amat], axis=1)
            rows_pad = 2 * n_tok

            def pad_rows(a):
                return jnp.pad(a, ((0, 0), (0, rows_pad - a.shape[1]), (0, 0)))

            ki_new = pad_rows(bmajor(kw16)[:, :, :IDX_DIM])
            keys = _idx_sample(page_table, qit, amat, ki_new, cache_kidx[j])
            width = keys.shape[2]
            topk = min(INDEX_TOPK, (past + n_tok) // 4)
            bias = _topk_bias(keys.reshape(bs * rows_pad, width), topk, min(256, bs * rows_pad)).reshape(bs, rows_pad, width)
            q64 = bmajor(q).reshape(bs, n_tok, N_HEADS, dh).transpose(0, 2, 1, 3).reshape(bs, N_HEADS * n_tok, dh)
            n_pool = cache_k.shape[1]
            o64 = _attn_sample(page_table, q64, bias, pad_rows(bmajor(k16)), pad_rows(bmajor(v16)),
                               cache_k[j].reshape(n_pool, PAGE, kvw), cache_v[j].reshape(n_pool, PAGE, kvw))
            o = o64.reshape(bs, N_HEADS, n_tok, dh).transpose(2, 0, 1, 3).reshape(ms, N_HEADS * dh)
            xs = _mm_res(o, w_attn_out[j], xs, mod_s, i, 5, tm=ms, tn=512)
            outs_s[i] = (bmajor(k32), bmajor(v32), bmajor(kw32)[:, :, :IDX_DIM])
        else:
            lb = lb_all[i].reshape(1, d)
            log_lb, log_1mlb = jnp.log(lb), jnp.log1p(-lb)
            gain = rec_out_norm[j].reshape(1, d)
            qs, kk, vv, lf, sg = _hgrn_projections(hp, w_rec_in[j], log_lb, log_1mlb, tm_p // 2)
            y, st = _gla_prompt(qs, kk, vv, lf, sg, gain, batch, seq, min(REC_CHUNK, seq), min(512, seq))
            xp = _mm_res(y, w_rec_out[j], xp, mod_p, i, 5, tm=tm_p, tn=512)
            outs_p[i] = (st,)
            qs, kk, vv, lf, sg = _hgrn_projections(hs, w_rec_in[j], log_lb, log_1mlb, bs)

            def bpad(a):
                a = a.reshape(n_tok, bs, d).transpose(1, 0, 2)
                return jnp.pad(a, ((0, 0), (0, SUBLANES - n_tok), (0, 0)))

            y, st = _gla_sample(bpad(qs), bpad(kk), bpad(vv), bpad(lf), bpad(sg), gain, state_hgrn[j], n_tok)
            y = y[:, :n_tok].transpose(1, 0, 2).reshape(ms, d)
            xs = _mm_res(y, w_rec_out[j], xs, mod_s, i, 5, tm=ms, tn=512)
            outs_s[i] = (st,)
        xp = _ffn(xp, mod_p, i, 2, gains, w_ff_up, w_ff_down, 1, tm_p, 256)
        xs = _ffn(xs, mod_s, i, 2, gains, w_ff_up, w_ff_down, 1, ms, 512)

    attn_layers = [i for i in range(depth) if i % 2 == 0]
    rec_layers = [i for i in range(depth) if i % 2 == 1]
    na, pages_p = len(attn_layers), seq // PAGE
    y_prompt = xp.reshape(batch, seq, d)
    y_sample = xs.reshape(n_tok, bs, d).transpose(1, 0, 2)
    k_prompt = jnp.stack([outs_p[i][0] for i in attn_layers]).reshape(na, batch, pages_p, PAGE, N_KV_HEADS, dh)
    v_prompt = jnp.stack([outs_p[i][1] for i in attn_layers]).reshape(na, batch, pages_p, PAGE, N_KV_HEADS, dh)
    kidx_prompt = jnp.stack([outs_p[i][2] for i in attn_layers]).reshape(na, batch, pages_p, PAGE, IDX_DIM)
    state_prompt = jnp.stack([outs_p[i][0] for i in rec_layers]).astype(state_hgrn.dtype)
    k_sample = jnp.stack([outs_s[i][0] for i in attn_layers]).reshape(na, bs, n_tok, N_KV_HEADS, dh)
    v_sample = jnp.stack([outs_s[i][1] for i in attn_layers]).reshape(na, bs, n_tok, N_KV_HEADS, dh)
    kidx_sample = jnp.stack([outs_s[i][2] for i in attn_layers])
    state_sample = jnp.stack([outs_s[i][0] for i in rec_layers]).astype(state_hgrn.dtype)
    return (y_prompt, y_sample, k_prompt, v_prompt, kidx_prompt, state_prompt, k_sample, v_sample, kidx_sample, state_sample)
```

```python
import functools

import jax
import jax.numpy as jnp
from jax import lax
from jax.experimental import pallas as pl
from jax.experimental.pallas import tpu as pltpu

F32 = jnp.float32
BF16 = jnp.bfloat16
I32 = jnp.int32

LANES = 128
SUBLANES = 8
VMEM_LIMIT = 60 * 1024 * 1024

N_HEADS = 16
N_KV_HEADS = 4
N_IDX_HEADS = 16
IDX_DIM = 64
INDEX_TOPK = 256
ROPE_THETA = 10000.0
REC_CHUNK = 64
EPS = 1e-6
PAGE = 128
NEG = -1e30
IMIN = -2 ** 31


def _cparams(*sem):
    return pltpu.CompilerParams(dimension_semantics=sem, vmem_limit_bytes=VMEM_LIMIT)


def _rows(mod, tm):
    rm = mod.shape[0]
    if rm == 1 or rm == tm:
        return mod
    return jnp.concatenate([mod] * (tm // rm), axis=0)


def _ada_norm(x, gain, shift, scale):
    tm = x.shape[0]
    y = x * lax.rsqrt(jnp.mean(x * x, axis=-1, keepdims=True) + EPS) * gain
    return y * (1.0 + _rows(scale, tm)) + _rows(shift, tm)


def _silu(x):
    return x * jax.nn.sigmoid(x)


def _ada_mod_kernel(c_ref, w_ref, b_ref, o_ref):
    cs = _silu(c_ref[...]).astype(BF16)
    o_ref[...] = jnp.dot(cs, w_ref[...].astype(BF16), preferred_element_type=F32) + b_ref[...]


def _ada_mod(c_all, w_ada, b_ada):
    depth, d, n = w_ada.shape
    m = c_all.shape[0]
    tn = 2048
    return pl.pallas_call(
        _ada_mod_kernel,
        grid=(depth, n // tn),
        in_specs=[
            pl.BlockSpec((m, d), lambda i, j: (0, 0)),
            pl.BlockSpec((None, d, tn), lambda i, j: (i, 0, j)),
            pl.BlockSpec((None, 1, tn), lambda i, j: (i, 0, j)),
        ],
        out_specs=pl.BlockSpec((None, m, tn), lambda i, j: (i, 0, j)),
        out_shape=jax.ShapeDtypeStruct((depth, m, n), F32),
        compiler_params=_cparams("parallel", "parallel"),
        name="ada_mod",
    )(c_all, w_ada, b_ada.reshape(depth, 1, n))


class _Mod:
    def __init__(self, arr, per_row, tiles_per_seq):
        self.arr = arr
        self.per_row = per_row
        self.tps = tiles_per_seq

    def spec(self, layer, k, tn=None, n_axis=False):
        d = self.arr.shape[-1]
        tn = d if tn is None else tn
        if self.per_row:
            rows = self.arr.shape[2]
            if n_axis:
                return pl.BlockSpec((None, None, rows, tn), lambda m, n: (layer, k, 0, n))
            return pl.BlockSpec((None, None, rows, tn), lambda m, *_: (layer, k, 0, 0))
        tps = self.tps
        if n_axis:
            return pl.BlockSpec((None, None, None, 1, tn), lambda m, n: (layer, m // tps, k, 0, n))
        return pl.BlockSpec((None, None, None, 1, tn), lambda m, *_: (layer, m // tps, k, 0, 0))


def _ffn_kernel(x_ref, sh_ref, sc_ref, gt_ref, gain_ref, wa_ref, wb_ref, wd_ref, o_ref, h_ref):
    j = pl.program_id(1)

    @pl.when(j == 0)
    def _():
        h_ref[...] = _ada_norm(x_ref[...], gain_ref[...], sh_ref[...], sc_ref[...]).astype(BF16)
        o_ref[...] = jnp.zeros_like(o_ref)

    h = h_ref[...]
    tf = wa_ref.shape[1]
    nsplit = 2 if tf % (2 * LANES) == 0 else 1
    gs = []
    for c in range(nsplit):
        cols = slice(c * tf // nsplit, (c + 1) * tf // nsplit)
        a = jnp.dot(h, wa_ref[:, cols].astype(BF16), preferred_element_type=F32)
        b = jnp.dot(h, wb_ref[:, cols].astype(BF16), preferred_element_type=F32)
        gs.append((_silu(a) * b).astype(BF16))
    g = gs[0] if nsplit == 1 else jnp.concatenate(gs, axis=1)
    o_ref[...] += jnp.dot(g, wd_ref[...].astype(BF16), preferred_element_type=F32)

    @pl.when(j == pl.num_programs(1) - 1)
    def _():
        tm = o_ref.shape[0]
        o_ref[...] = x_ref[...] + 0.5 * (1.0 + _rows(gt_ref[...], tm)) * o_ref[...]


def _ffn(x, mod, layer, sub, gain, w_up, w_down, ffn_idx, tm, tf):
    m, d = x.shape
    f = w_down.shape[2]
    nf = f // tf
    return pl.pallas_call(
        _ffn_kernel,
        grid=(m // tm, nf),
        in_specs=[
            pl.BlockSpec((tm, d), lambda i, j: (i, 0), pipeline_mode=pl.Buffered(1)),
            mod.spec(layer, 3 * sub + 0),
            mod.spec(layer, 3 * sub + 1),
            mod.spec(layer, 3 * sub + 2),
            pl.BlockSpec((None, None, 1, d), lambda i, j: (layer, sub, 0, 0)),
            pl.BlockSpec((None, None, d, tf), lambda i, j: (layer, ffn_idx, 0, j)),
            pl.BlockSpec((None, None, d, tf), lambda i, j: (layer, ffn_idx, 0, j + nf)),
            pl.BlockSpec((None, None, tf, d), lambda i, j: (layer, ffn_idx, j, 0)),
        ],
        out_specs=pl.BlockSpec((tm, d), lambda i, j: (i, 0), pipeline_mode=pl.Buffered(1)),
        out_shape=jax.ShapeDtypeStruct((m, d), F32),
        scratch_shapes=[pltpu.VMEM((tm, d), BF16)],
        compiler_params=_cparams("parallel", "arbitrary"),
        name="ffn",
    )(x, mod.arr, mod.arr, mod.arr, gain, w_up, w_up, w_down)


def _prenorm_kernel(x_ref, sh_ref, sc_ref, gain_ref, h_ref):
    h_ref[...] = _ada_norm(x_ref[...], gain_ref[...], sh_ref[...], sc_ref[...]).astype(BF16)


def _prenorm(x, mod, layer, sub, gain, tm):
    m, d = x.shape
    return pl.pallas_call(
        _prenorm_kernel,
        grid=(m // tm,),
        in_specs=[
            pl.BlockSpec((tm, d), lambda i: (i, 0)),
            mod.spec(layer, 3 * sub + 0),
            mod.spec(layer, 3 * sub + 1),
            pl.BlockSpec((None, None, 1, d), lambda i: (layer, sub, 0, 0)),
        ],
        out_specs=pl.BlockSpec((tm, d), lambda i: (i, 0)),
        out_shape=jax.ShapeDtypeStruct((m, d), BF16),
        compiler_params=_cparams("parallel"),
        name="prenorm",
    )(x, mod.arr, mod.arr, gain)


def _rot_half(y, half):
    if 2 * half == LANES:
        return pltpu.roll(y, half, 1)
    lane = lax.broadcasted_iota(I32, y.shape, 1)
    return jnp.where(lane % (2 * half) < half, pltpu.roll(y, LANES - half, 1), pltpu.roll(y, half, 1))


def _proj_rope_kernel(a_ref, w_ref, g_ref, ca_ref, cb_ref, *o_refs, norm, half, layout):
    y = jnp.dot(a_ref[...], w_ref[...].astype(BF16), preferred_element_type=F32)
    ca = ca_ref[...]
    cb = cb_ref[...]
    tm = y.shape[0]
    ncol = y.shape[1] // LANES
    for c in range(ncol):
        yc = y[:, c * LANES:(c + 1) * LANES]
        if norm:
            yc = yc * lax.rsqrt(jnp.mean(yc * yc, axis=-1, keepdims=True) + EPS) * g_ref[...]
        r = yc * ca + _rot_half(yc, half) * cb
        if layout == "ki_split":
            lane = lax.broadcasted_iota(I32, r.shape, 1)
            lo = jnp.where(lane < IDX_DIM, r, 0.0)
            hi = pltpu.roll(lo, IDX_DIM, 1)
            o_refs[0][...] = r
            for qb in range(tm // PAGE):
                rows = slice(qb * PAGE, (qb + 1) * PAGE)
                o_refs[1][2 * qb * PAGE:(2 * qb + 1) * PAGE, :] = lo[rows].astype(BF16)
                o_refs[1][(2 * qb + 1) * PAGE:(2 * qb + 2) * PAGE, :] = hi[rows].astype(BF16)
        elif layout == "stack":
            for qb in range(tm // PAGE):
                o_refs[0][(qb * ncol + c) * PAGE:(qb * ncol + c + 1) * PAGE, :] = (
                    r[qb * PAGE:(qb + 1) * PAGE].astype(o_refs[0].dtype))
        else:
            for o_ref in o_refs:
                o_ref[:, c * LANES:(c + 1) * LANES] = r.astype(o_ref.dtype)


def _proj_rope(a, w, col0, n, gain, tab_a, tab_b, out_dtypes, *, norm, half, tm, tn, layout="cols"):
    m, k = a.shape
    ntab = tab_a.shape[0] // tm
    cb0 = col0 // tn
    if layout == "stack":
        assert tn == n and tm % PAGE == 0
        out_specs = [pl.BlockSpec((tm * (n // LANES), LANES), lambda i, j: (i, 0))]
        out_shape = [jax.ShapeDtypeStruct((m * (n // LANES), LANES), out_dtypes[0])]
    elif layout == "ki_split":
        assert tn == n == LANES and tm % PAGE == 0
        out_specs = [pl.BlockSpec((tm, LANES), lambda i, j: (i, 0)), pl.BlockSpec((2 * tm, LANES), lambda i, j: (i, 0))]
        out_shape = [jax.ShapeDtypeStruct((m, LANES), F32), jax.ShapeDtypeStruct((2 * m, LANES), BF16)]
    else:
        out_specs = [pl.BlockSpec((tm, tn), lambda i, j: (i, j)) for _ in out_dtypes]
        out_shape = [jax.ShapeDtypeStruct((m, n), dt) for dt in out_dtypes]
    outs = pl.pallas_call(
        functools.partial(_proj_rope_kernel, norm=norm, half=half, layout=layout),
        grid=(m // tm, n // tn),
        in_specs=[
            pl.BlockSpec((tm, k), lambda i, j: (i, 0)),
            pl.BlockSpec((k, tn), lambda i, j: (0, cb0 + j)),
            pl.BlockSpec((1, LANES), lambda i, j: (0, 0)),
            pl.BlockSpec((tm, LANES), lambda i, j: (i % ntab, 0)),
            pl.BlockSpec((tm, LANES), lambda i, j: (i % ntab, 0)),
        ],
        out_specs=out_specs,
        out_shape=out_shape,
        compiler_params=_cparams("parallel", "parallel"),
        name="proj_rope",
    )(a, w, gain, tab_a, tab_b)
    return outs


def _proj_plain_kernel(a_ref, w_ref, *o_refs, act):
    y = jnp.dot(a_ref[...], w_ref[...].astype(BF16), preferred_element_type=F32)
    if act == "silu":
        y = _silu(y)
    for o_ref in o_refs:
        o_ref[...] = y.astype(o_ref.dtype)


def _proj_plain(a, w, col0, n, out_dtypes, *, act, tm, tn):
    m, k = a.shape
    cb0 = col0 // tn
    return pl.pallas_call(
        functools.partial(_proj_plain_kernel, act=act),
        grid=(m // tm, n // tn),
        in_specs=[
            pl.BlockSpec((tm, k), lambda i, j: (i, 0)),
            pl.BlockSpec((k, tn), lambda i, j: (0, cb0 + j)),
        ],
        out_specs=[pl.BlockSpec((tm, tn), lambda i, j: (i, j)) for _ in out_dtypes],
        out_shape=[jax.ShapeDtypeStruct((m, n), dt) for dt in out_dtypes],
        compiler_params=_cparams("parallel", "parallel"),
        name="proj_plain",
    )(a, w)


def _proj_forget_kernel(a_ref, w_ref, loglb_ref, log1mlb_ref, lf_ref, k_ref):
    fr = jnp.dot(a_ref[...], w_ref[...].astype(BF16), preferred_element_type=F32)
    t = jnp.log1p(jnp.exp(-jnp.abs(fr)))
    ls_pos = jnp.minimum(fr, 0.0) - t
    ls_neg = jnp.minimum(-fr, 0.0) - t
    a = loglb_ref[...]
    c = log1mlb_ref[...] + ls_pos
    lf_ref[...] = jnp.maximum(a, c) + jnp.log1p(jnp.exp(-jnp.abs(a - c)))
    k_ref[...] = jnp.exp(log1mlb_ref[...] + ls_neg)


def _proj_forget(a, w, col0, n, log_lb, log_1mlb, *, tm, tn):
    m, k = a.shape
    cb0 = col0 // tn
    return pl.pallas_call(
        _proj_forget_kernel,
        grid=(m // tm, n // tn),
        in_specs=[
            pl.BlockSpec((tm, k), lambda i, j: (i, 0)),
            pl.BlockSpec((k, tn), lambda i, j: (0, cb0 + j)),
            pl.BlockSpec((1, tn), lambda i, j: (0, j)),
            pl.BlockSpec((1, tn), lambda i, j: (0, j)),
        ],
        out_specs=[pl.BlockSpec((tm, tn), lambda i, j: (i, j)) for _ in range(2)],
        out_shape=[jax.ShapeDtypeStruct((m, n), F32) for _ in range(2)],
        compiler_params=_cparams("parallel", "parallel"),
        name="proj_forget",
    )(a, w, log_lb, log_1mlb)


def _mm_res_kernel(a_ref, w_ref, x_ref, gt_ref, o_ref):
    y = jnp.dot(a_ref[...], w_ref[...].astype(BF16), preferred_element_type=F32)
    o_ref[...] = x_ref[...] + (1.0 + _rows(gt_ref[...], y.shape[0])) * y


def _mm_res(a, w, x, mod, layer, k_gate, *, tm, tn):
    m, k = a.shape
    n = w.shape[1]
    return pl.pallas_call(
        _mm_res_kernel,
        grid=(m // tm, n // tn),
        in_specs=[
            pl.BlockSpec((tm, k), lambda i, j: (i, 0)),
            pl.BlockSpec((k, tn), lambda i, j: (0, j)),
            pl.BlockSpec((tm, tn), lambda i, j: (i, j)),
            mod.spec(layer, k_gate, tn=tn, n_axis=True),
        ],
        out_specs=pl.BlockSpec((tm, tn), lambda i, j: (i, j)),
        out_shape=jax.ShapeDtypeStruct((m, n), F32),
        compiler_params=_cparams("parallel", "parallel"),
        name="mm_res",
    )(a, w, x, mod.arr)


def _sort_key(score):
    bits = pltpu.bitcast(score + 0.0, I32)
    return jnp.where(bits < 0, bits ^ jnp.int32(0x7FFFFFFF), bits)


COUNT_UNROLL = 4


def _attn_prompt_kernel(q_ref, qi_ref, wi_ref, k_ref, v_ref, kiab_ref, o_ref,
                        key_ref, bias_ref, wb_ref, j_ref, m_ref, lp_ref, acc_ref, *, topk, idx_bits, kv_unroll):
    tq = q_ref.shape[0]
    dh = LANES
    rep = N_HEADS // N_KV_HEADS
    qb = pl.program_id(1)
    nkb = qb + 1
    row = lax.broadcasted_iota(I32, (tq, LANES), 0)
    lane = lax.broadcasted_iota(I32, (tq, LANES), 1)
    imin = jnp.int32(IMIN)
    nt = (((1,), (1,)), ((), ()))

    wi = wi_ref[...]
    for h in range(N_IDX_HEADS):
        wcol = wi[:, IDX_DIM + h:IDX_DIM + h + 1] * (IDX_DIM ** -0.5)
        wb_ref[h] = jnp.broadcast_to(wcol, (tq, LANES))

    qi_all = qi_ref[...]

    def score_body(kb, c):
        lg = lax.dot_general(qi_all, kiab_ref[kb], nt, preferred_element_type=F32)
        acc = jnp.zeros((tq, LANES), F32)
        for c2 in range(N_IDX_HEADS // 2):
            r = slice(c2 * tq, (c2 + 1) * tq)
            acc = acc + (wb_ref[2 * c2] * jnp.maximum(lg[r, 0:tq], 0.0)
                         + wb_ref[2 * c2 + 1] * jnp.maximum(lg[r, tq:2 * tq], 0.0))
        adm = jnp.logical_or(kb < qb, lane <= row)
        key_ref[kb] = jnp.where(adm, _sort_key(acc), imin)
        return c

    lax.fori_loop(0, nkb, score_body, 0)
    for u in range(COUNT_UNROLL - 1):
        key_ref[nkb + u] = jnp.full((tq, LANES), IMIN, I32)

    def count(pred):
        def body(it, part):
            w = [jnp.where(pred(it * COUNT_UNROLL + u, key_ref[it * COUNT_UNROLL + u]), 1.0, 0.0)
                 for u in range(COUNT_UNROLL)]
            while len(w) > 1:
                w = [w[i] + w[i + 1] for i in range(0, len(w) - 1, 2)] + ([w[-1]] if len(w) % 2 else [])
            return part + w[0]
        n_it = (nkb + COUNT_UNROLL - 1) // COUNT_UNROLL
        part = lax.fori_loop(0, n_it, body, jnp.zeros((tq, LANES), F32))
        return jnp.sum(part, axis=1, keepdims=True)

    def bit_body(i, tu):
        cu = tu | jnp.left_shift(jnp.int32(1), 31 - i)
        cand = cu ^ imin
        cnt = count(lambda kb, k: k >= cand)
        return jnp.where(cnt >= topk, cu, tu)

    thr = lax.fori_loop(0, 32, bit_body, jnp.zeros((tq, 1), I32)) ^ imin
    cnt_gt = count(lambda kb, k: k > thr)
    cnt_ge = count(lambda kb, k: k >= thr)
    need = topk - cnt_gt
    tie = jnp.logical_and(cnt_ge > topk, thr > imin)

    j_ref[...] = jnp.full(j_ref.shape, 2 ** idx_bits, I32)

    @pl.when(jnp.max(jnp.where(tie, 1.0, 0.0)) > 0.0)
    def _():
        def jbit_body(i, jv):
            cand = jv | jnp.left_shift(jnp.int32(1), idx_bits - 1 - i)
            c = count(lambda kb, k: jnp.logical_and(k == thr, kb * LANES + lane < cand))
            return jnp.where(c < need, cand, jv)
        jv = lax.fori_loop(0, idx_bits, jbit_body, jnp.zeros((tq, 1), I32))
        j_ref[...] = jnp.broadcast_to(jv, j_ref.shape)

    jv = j_ref[...]

    def bias_body(kb, c):
        k = key_ref[kb]
        sel = jnp.logical_or(k > thr, jnp.logical_and(k == thr, kb * LANES + lane <= jv))
        sel = jnp.logical_and(sel, k > imin)
        bias_ref[kb] = jnp.where(sel, 0.0, NEG)
        return c

    lax.fori_loop(0, nkb, bias_body, 0)

    for u in range(kv_unroll - 1):
        bias_ref[nkb + u] = jnp.full((tq, LANES), NEG, F32)
    n_it = (nkb + kv_unroll - 1) // kv_unroll
    span = kv_unroll * tq
    c2 = (dh ** -0.5) * 1.4426950408889634
    for g in range(N_KV_HEADS):
        cols = slice(g * dh, (g + 1) * dh)
        qg = jnp.concatenate([q_ref[:, (g * rep + j) * dh:(g * rep + j + 1) * dh] for j in range(rep)], axis=0)

        def scores(it, qg=qg, cols=cols):
            keys = pl.ds(pl.multiple_of(it * span, span), span)
            s = lax.dot_general(qg, k_ref[keys, cols], nt, preferred_element_type=F32)
            b = jnp.concatenate([bias_ref[it * kv_unroll + u] for u in range(kv_unroll)], axis=1)
            return keys, s, b

        m_ref[...] = jnp.full(m_ref.shape, NEG, F32)

        def pass1(it, c):
            _, s, b = scores(it)
            for j in range(rep):
                sj = s[j * tq:(j + 1) * tq] + b
                t = sj[:, 0:LANES]
                for u in range(1, kv_unroll):
                    t = jnp.maximum(t, sj[:, u * LANES:(u + 1) * LANES])
                m_ref[j * tq:(j + 1) * tq, :] = jnp.maximum(m_ref[j * tq:(j + 1) * tq, :], t)
            return c

        lax.fori_loop(0, n_it, pass1, 0)
        mc = jnp.max(m_ref[...], axis=1, keepdims=True) * c2
        lp_ref[...] = jnp.zeros_like(lp_ref)
        acc_ref[...] = jnp.zeros_like(acc_ref)

        def pass2(it, c, cols=cols, mc=mc):
            keys, s, b = scores(it)
            ps = []
            for j in range(rep):
                p = jnp.exp2((s[j * tq:(j + 1) * tq] + b) * c2 - mc[j * tq:(j + 1) * tq])
                t = p[:, 0:LANES]
                for u in range(1, kv_unroll):
                    t = t + p[:, u * LANES:(u + 1) * LANES]
                lp_ref[j * tq:(j + 1) * tq, :] += t
                ps.append(p.astype(BF16))
            acc_ref[...] += jnp.dot(jnp.concatenate(ps, axis=0), v_ref[keys, cols], preferred_element_type=F32)
            return c

        lax.fori_loop(0, n_it, pass2, 0)
        o = acc_ref[...] / jnp.sum(lp_ref[...], axis=1, keepdims=True)
        for j in range(rep):
            h = g * rep + j
            o_ref[:, h * dh:(h + 1) * dh] = o[j * tq:(j + 1) * tq].astype(o_ref.dtype)


def _attn_prompt(q, qi_stack, kiwi_f32, k_bf, v_bf, kiab, batch, seq):
    m = q.shape[0]
    tq = PAGE
    nq = seq // tq
    topk = min(INDEX_TOPK, seq // 4)
    kvw = k_bf.shape[1]
    npair = N_IDX_HEADS // 2
    rep = N_HEADS // N_KV_HEADS
    kv_unroll = next(u for u in (4, 2, 1) if nq % u == 0)
    kv_spec = pl.BlockSpec((None, seq, kvw), lambda b, i: (b, 0, 0))
    return pl.pallas_call(
        functools.partial(_attn_prompt_kernel, topk=topk, idx_bits=max(1, (seq - 1).bit_length()), kv_unroll=kv_unroll),
        grid=(batch, nq),
        in_specs=[
            pl.BlockSpec((tq, q.shape[1]), lambda b, i: (b * nq + i, 0)),
            pl.BlockSpec((npair * tq, LANES), lambda b, i: (b * nq + i, 0)),
            pl.BlockSpec((tq, LANES), lambda b, i: (b * nq + i, 0)),
            kv_spec, kv_spec,
            pl.BlockSpec((None, nq, 2 * tq, LANES), lambda b, i: (b, 0, 0, 0)),
        ],
        out_specs=pl.BlockSpec((tq, q.shape[1]), lambda b, i: (b * nq + i, 0)),
        out_shape=jax.ShapeDtypeStruct((m, q.shape[1]), BF16),
        scratch_shapes=[
            pltpu.VMEM((nq + COUNT_UNROLL - 1, tq, LANES), I32),
            pltpu.VMEM((nq + kv_unroll - 1, tq, LANES), F32),
            pltpu.VMEM((N_IDX_HEADS, tq, LANES), F32),
            pltpu.VMEM((tq, LANES), I32),
            pltpu.VMEM((rep * tq, LANES), F32),
            pltpu.VMEM((rep * tq, LANES), F32),
            pltpu.VMEM((rep * tq, LANES), F32),
        ],
        compiler_params=_cparams("parallel", "arbitrary"),
        name="attn_prompt",
    )(q, qi_stack, kiwi_f32, k_bf.reshape(batch, seq, kvw), v_bf.reshape(batch, seq, kvw),
      kiab.reshape(batch, nq, 2 * tq, LANES))


def _idx_sample_kernel(pt_ref, qit_ref, a_ref, kinew_ref, *refs, n_pages, n_tok):
    page_refs = refs[:n_pages]
    o_ref = refs[n_pages]
    qt = qit_ref[...]
    a = a_ref[...]
    rows = a.shape[0]
    lane = lax.broadcasted_iota(I32, (rows, LANES), 1)
    row = lax.broadcasted_iota(I32, (rows, LANES), 0)
    for p in range(n_pages + 1):
        if p < n_pages:
            kip = page_refs[p][...].astype(BF16)
        else:
            kip = jnp.concatenate([kinew_ref[...], jnp.zeros((PAGE - kinew_ref.shape[0], IDX_DIM), BF16)], axis=0)
        lg = jnp.dot(kip, qt, preferred_element_type=F32)
        sc = lax.dot_general(a, jnp.maximum(lg, 0.0), (((1,), (1,)), ((), ())),
                             precision=lax.Precision.HIGHEST, preferred_element_type=F32)
        key = _sort_key(sc)
        if p == n_pages:
            key = jnp.where(lane <= row % n_tok, key, jnp.int32(IMIN))
        o_ref[:, p * LANES:(p + 1) * LANES] = key


def _idx_sample(page_table, qit, amat, ki_new, pool_kidx, page0):
    bs, n_pages = page_table.shape
    n_tok = qit.shape[2] // N_IDX_HEADS
    rows = amat.shape[1]
    width = (n_pages + 1) * LANES
    page_specs = [pl.BlockSpec((None, PAGE, IDX_DIM), lambda b, pt, p=p: (page0 + pt[b, p], 0, 0))
                  for p in range(n_pages)]
    grid_spec = pltpu.PrefetchScalarGridSpec(
        num_scalar_prefetch=1,
        grid=(bs,),
        in_specs=[
            pl.BlockSpec((None,) + qit.shape[1:], lambda b, pt: (b, 0, 0)),
            pl.BlockSpec((None,) + amat.shape[1:], lambda b, pt: (b, 0, 0)),
            pl.BlockSpec((None,) + ki_new.shape[1:], lambda b, pt: (b, 0, 0)),
        ] + page_specs,
        out_specs=pl.BlockSpec((None, rows, width), lambda b, pt: (b, 0, 0)),
    )
    return pl.pallas_call(
        functools.partial(_idx_sample_kernel, n_pages=n_pages, n_tok=n_tok),
        grid_spec=grid_spec,
        out_shape=jax.ShapeDtypeStruct((bs, rows, width), I32),
        compiler_params=_cparams("arbitrary"),
        name="idx_sample",
    )(page_table, qit, amat, ki_new, *([pool_kidx] * n_pages))


def _topk_bias_kernel(key_ref, bias_ref, *, topk, idx_bits):
    key = key_ref[...]
    rows, width = key.shape
    imin = jnp.int32(IMIN)
    idx = lax.broadcasted_iota(I32, (rows, width), 1)

    def count(pred):
        return jnp.sum(jnp.where(pred, 1.0, 0.0), axis=1, keepdims=True)

    def bit_body(i, tu):
        cu = tu | jnp.left_shift(jnp.int32(1), 31 - i)
        cnt = count(key >= (cu ^ imin))
        return jnp.where(cnt >= topk, cu, tu)

    thr = lax.fori_loop(0, 32, bit_body, jnp.zeros((rows, 1), I32)) ^ imin
    need = topk - count(key > thr)
    eq = key == thr

    def jbit_body(i, jv):
        cand = jv | jnp.left_shift(jnp.int32(1), idx_bits - 1 - i)
        c = count(jnp.logical_and(eq, idx < cand))
        return jnp.where(c < need, cand, jv)

    jv = lax.fori_loop(0, idx_bits, jbit_body, jnp.zeros((rows, 1), I32))
    sel = jnp.logical_or(key > thr, jnp.logical_and(eq, idx <= jv))
    sel = jnp.logical_and(sel, key > imin)
    bias_ref[...] = jnp.where(sel, 0.0, NEG)


def _topk_bias(keys2d, topk, tr):
    rows, width = keys2d.shape
    return pl.pallas_call(
        functools.partial(_topk_bias_kernel, topk=topk, idx_bits=max(1, (width - 1).bit_length())),
        grid=(rows // tr,),
        in_specs=[pl.BlockSpec((tr, width), lambda i: (i, 0))],
        out_specs=pl.BlockSpec((tr, width), lambda i: (i, 0)),
        out_shape=jax.ShapeDtypeStruct((rows, width), F32),
        compiler_params=_cparams("parallel"),
        name="topk_bias",
    )(keys2d)


def _attn_sample_kernel(pt_ref, q_ref, bias_ref, knew_ref, vnew_ref, *refs, n_pages):
    k_refs = refs[:n_pages]
    v_refs = refs[n_pages:2 * n_pages]
    o_ref = refs[2 * n_pages]
    rows, dh = q_ref.shape
    rpg = rows // N_KV_HEADS
    bias8 = bias_ref[...]
    bias = jnp.concatenate([bias8] * (rpg // bias8.shape[0]), axis=0)
    pad = PAGE - knew_ref.shape[0]
    zpad = jnp.zeros((pad, dh), BF16)

    for g in range(N_KV_HEADS):
        qg = q_ref[g * rpg:(g + 1) * rpg, :]
        cols = slice(g * dh, (g + 1) * dh)
        s_blocks = []
        for p in range(n_pages + 1):
            if p < n_pages:
                kp = k_refs[p][:, g, :].astype(BF16)
            else:
                kp = jnp.concatenate([knew_ref[:, cols], zpad], axis=0)
            s_blocks.append(lax.dot_general(qg, kp, (((1,), (1,)), ((), ())), preferred_element_type=F32))
        s = jnp.concatenate(s_blocks, axis=1) * (dh ** -0.5) + bias
        m = jnp.max(s, axis=1, keepdims=True)
        pexp = jnp.exp(s - m)
        l = jnp.sum(pexp, axis=1, keepdims=True)
        pb = pexp.astype(BF16)
        o = jnp.zeros((rpg, dh), F32)
        for p in range(n_pages + 1):
            if p < n_pages:
                vp = v_refs[p][:, g, :].astype(BF16)
            else:
                vp = jnp.concatenate([vnew_ref[:, cols], zpad], axis=0)
            o = o + jnp.dot(pb[:, p * LANES:(p + 1) * LANES], vp, preferred_element_type=F32)
        o_ref[g * rpg:(g + 1) * rpg, :] = (o / l).astype(o_ref.dtype)


def _attn_sample(page_table, q64, bias, k_new, v_new, pool_k, pool_v, page0):
    bs, n_pages = page_table.shape
    pshape = (None,) + pool_k.shape[1:]
    kspecs = [pl.BlockSpec(pshape, lambda b, pt, p=p: (page0 + pt[b, p], 0, 0, 0)) for p in range(n_pages)]
    vspecs = [pl.BlockSpec(pshape, lambda b, pt, p=p: (page0 + pt[b, p], 0, 0, 0)) for p in range(n_pages)]
    grid_spec = pltpu.PrefetchScalarGridSpec(
        num_scalar_prefetch=1,
        grid=(bs,),
        in_specs=[
            pl.BlockSpec((None,) + q64.shape[1:], lambda b, pt: (b, 0, 0)),
            pl.BlockSpec((None,) + bias.shape[1:], lambda b, pt: (b, 0, 0)),
            pl.BlockSpec((None,) + k_new.shape[1:], lambda b, pt: (b, 0, 0)),
            pl.BlockSpec((None,) + v_new.shape[1:], lambda b, pt: (b, 0, 0)),
        ] + kspecs + vspecs,
        out_specs=pl.BlockSpec((None,) + q64.shape[1:], lambda b, pt: (b, 0, 0)),
    )
    return pl.pallas_call(
        functools.partial(_attn_sample_kernel, n_pages=n_pages),
        grid_spec=grid_spec,
        out_shape=jax.ShapeDtypeStruct(q64.shape, BF16),
        compiler_params=_cparams("arbitrary"),
        name="attn_sample",
    )(page_table, q64, bias, k_new, v_new, *([pool_k] * n_pages), *([pool_v] * n_pages))


def _gla_chunk(q, k, v, g, state, oacc_ref, n_src):
    c = q.shape[0]
    tri = (lax.broadcasted_iota(I32, (c, c), 0) >= lax.broadcasted_iota(I32, (c, c), 1)).astype(F32)
    b = jnp.dot(tri, g, precision=lax.Precision.HIGHEST, preferred_element_type=F32)
    oacc_ref[...] = jnp.dot((q * jnp.exp(b)).astype(BF16), state.astype(BF16), preferred_element_type=F32)
    ones = jnp.ones((LANES, LANES), BF16)
    t_idx = lax.broadcasted_iota(I32, (c, LANES), 0)
    for s in range(n_src):
        t0 = (s // SUBLANES) * SUBLANES
        msk = t_idx[t0:] >= s
        decay = jnp.exp(jnp.where(msk, b[t0:] - b[s:s + 1], 0.0))
        x = jnp.where(msk, q[t0:] * k[s:s + 1] * decay, 0.0)
        a_s = jnp.dot(x.astype(BF16), ones, preferred_element_type=F32)
        oacc_ref[t0:, :] += a_s * v[s:s + 1]
    b_end = b[c - 1:c]
    eye = lax.broadcasted_iota(I32, (LANES, LANES), 0) == lax.broadcasted_iota(I32, (LANES, LANES), 1)
    decay_col = jnp.sum(jnp.where(eye, jnp.broadcast_to(jnp.exp(b_end), (LANES, LANES)), 0.0), axis=1, keepdims=True)
    kd = (k * jnp.exp(b_end - b)).astype(BF16)
    upd = lax.dot_general(kd, v.astype(BF16), (((0,), (0,)), ((), ())), preferred_element_type=F32)
    return decay_col * state + upd


def _gla_out(o, sg, gain):
    return (o * lax.rsqrt(jnp.mean(o * o, axis=-1, keepdims=True) + EPS) * gain * sg)


GLA_CHUNK = 16


def _gla_prompt_kernel(q_ref, k_ref, v_ref, g_ref, sg_ref, gain_ref, y_ref, s_ref, st_ref, *, blk):
    ci = pl.program_id(2)
    c = GLA_CHUNK

    @pl.when(ci == 0)
    def _():
        st_ref[...] = jnp.zeros_like(st_ref)

    tri = (lax.broadcasted_iota(I32, (blk, blk), 0) >= lax.broadcasted_iota(I32, (blk, blk), 1)).astype(F32)
    ones = jnp.ones((LANES, LANES), BF16)
    nsub = c // SUBLANES
    nch = blk // c
    gain = gain_ref[...]
    r8 = lax.broadcasted_iota(I32, (SUBLANES, LANES), 0)
    mask_bias = [jnp.where(r8 >= j, 0.0, NEG) for j in range(SUBLANES)]
    log2e = 1.4426950408889634

    def body(bi, carry):
        r0 = pl.multiple_of(bi * blk, blk)
        bb = jnp.dot(tri, g_ref[pl.ds(r0, blk), :] * log2e, precision=lax.Precision.HIGHEST,
                     preferred_element_type=F32)

        qs, vs, bs, xs = [], [], [], []
        for ch in range(nch):
            lo = ch * c
            rows = pl.ds(r0 + lo, c)
            q, k, v = q_ref[rows, :], k_ref[rows, :], v_ref[rows, :]
            b = bb[lo:lo + c] - bb[lo - 1:lo] if ch > 0 else bb[lo:lo + c]
            qs.append(q)
            vs.append(v)
            bs.append((b, k))
            for s in range(c):
                i0 = s // SUBLANES
                qk = q[i0 * SUBLANES:] * k[s:s + 1]
                diff = b[i0 * SUBLANES:] - b[s:s + 1]
                for i in range(i0, nsub):
                    d = diff[(i - i0) * SUBLANES:(i - i0 + 1) * SUBLANES]
                    if i == i0:
                        d = d + mask_bias[s % SUBLANES]
                    xs.append(qk[(i - i0) * SUBLANES:(i - i0 + 1) * SUBLANES] * jnp.exp2(d))
        a_all = jnp.dot(jnp.concatenate(xs, axis=0).astype(BF16), ones, preferred_element_type=F32)
        upds, decs, qes = [], [], []
        for ch in range(nch):
            b, k = bs[ch]
            b_end = b[c - 1:c]
            kd = (k * jnp.exp2(b_end - b)).astype(BF16)
            upds.append(lax.dot_general(vs[ch].astype(BF16), kd, (((0,), (0,)), ((), ())), preferred_element_type=F32))
            decs.append(jnp.exp2(b_end))
            qes.append((qs[ch] * jnp.exp2(b)).astype(BF16))

        st = st_ref[...]
        off = 0
        for ch in range(nch):
            o = lax.dot_general(qes[ch], st.astype(BF16), (((1,), (1,)), ((), ())), preferred_element_type=F32)
            st = st * decs[ch] + upds[ch]
            o_parts = [o[i * SUBLANES:(i + 1) * SUBLANES] for i in range(nsub)]
            v = vs[ch]
            for s in range(c):
                for i in range(s // SUBLANES, nsub):
                    o_parts[i] = o_parts[i] + a_all[off:off + SUBLANES] * v[s:s + 1]
                    off += SUBLANES
            rows = pl.ds(r0 + ch * c, c)
            y_ref[rows, :] = _gla_out(jnp.concatenate(o_parts, axis=0), sg_ref[rows, :], gain).astype(y_ref.dtype)
        st_ref[...] = st
        return carry

    lax.fori_loop(0, q_ref.shape[0] // blk, body, 0)

    @pl.when(ci == pl.num_programs(2) - 1)
    def _():
        s_ref[...] = st_ref[...].T


def _gla_prompt(qs, kk, vv, lf, sg, gain, batch, seq, tb):
    m, d = qs.shape
    nh = d // LANES
    nc = seq // tb
    tok = pl.BlockSpec((tb, LANES), lambda b, h, c: (b * nc + c, h))
    return pl.pallas_call(
        functools.partial(_gla_prompt_kernel, blk=min(LANES, tb)),
        grid=(batch, nh, nc),
        in_specs=[tok, tok, tok, tok, tok, pl.BlockSpec((1, LANES), lambda b, h, c: (0, h))],
        out_specs=[tok, pl.BlockSpec((None, None, LANES, LANES), lambda b, h, c: (b, h, 0, 0))],
        out_shape=[jax.ShapeDtypeStruct((m, d), BF16), jax.ShapeDtypeStruct((batch, nh, LANES, LANES), F32)],
        scratch_shapes=[pltpu.VMEM((LANES, LANES), F32)],
        compiler_params=_cparams("parallel", "parallel", "arbitrary"),
        name="gla_prompt",
    )(qs, kk, vv, lf, sg, gain)


def _gla_sample_kernel(q_ref, k_ref, v_ref, g_ref, sg_ref, gain_ref, s0_ref, y_ref, s_ref, oacc_ref, *, n_tok):
    nh = s0_ref.shape[0]
    for h in range(nh):
        c = slice(h * LANES, (h + 1) * LANES)
        st = _gla_chunk(q_ref[:, c], k_ref[:, c], v_ref[:, c], g_ref[:, c], s0_ref[h], oacc_ref, n_tok)
        s_ref[h] = st
        y_ref[:, c] = _gla_out(oacc_ref[...], sg_ref[:, c], gain_ref[:, c]).astype(y_ref.dtype)


def _gla_sample(qs, kk, vv, lf, sg, gain, state_pool, seq0, n_tok):
    bs, tp, d = qs.shape
    nh = state_pool.shape[1]
    tok = pl.BlockSpec((None, tp, d), lambda b: (b, 0, 0))
    st_in = pl.BlockSpec((None, nh, LANES, LANES), lambda b: (seq0 + b, 0, 0, 0))
    st = pl.BlockSpec((None, nh, LANES, LANES), lambda b: (b, 0, 0, 0))
    return pl.pallas_call(
        functools.partial(_gla_sample_kernel, n_tok=n_tok),
        grid=(bs,),
        in_specs=[tok, tok, tok, tok, tok, pl.BlockSpec((1, d), lambda b: (0, 0)), st_in],
        out_specs=[tok, st],
        out_shape=[jax.ShapeDtypeStruct((bs, tp, d), BF16), jax.ShapeDtypeStruct((bs, nh, LANES, LANES), F32)],
        scratch_shapes=[pltpu.VMEM((tp, LANES), F32)],
        compiler_params=_cparams("parallel"),
        name="gla_sample",
    )(qs, kk, vv, lf, sg, gain, state_pool)


def _rope_tables(pos, d):
    inv = ROPE_THETA ** (-jnp.arange(0, d, 2, dtype=F32) / d)
    ang = pos.astype(F32)[:, None] * inv[None, :]
    cos, sin = jnp.cos(ang), jnp.sin(ang)
    reps = LANES // d
    a = jnp.tile(jnp.concatenate([cos, cos], axis=1), (1, reps))
    b = jnp.tile(jnp.concatenate([-sin, sin], axis=1), (1, reps))
    return a, b


def _kiwi_tables(pos):
    a64, b64 = _rope_tables(pos, IDX_DIM)
    p = pos.shape[0]
    lane = jnp.arange(IDX_DIM)
    wi_scale = jnp.where(lane < N_IDX_HEADS, N_IDX_HEADS ** -0.5, 0.0).astype(F32)
    a = jnp.concatenate([a64[:, :IDX_DIM], jnp.broadcast_to(wi_scale, (p, IDX_DIM))], axis=1)
    b = jnp.concatenate([b64[:, :IDX_DIM], jnp.zeros((p, IDX_DIM), F32)], axis=1)
    return a, b


def _attn_projections(h, w_in, w_kiwi, q_gain, k_gain, pos, tm, stack_qi):
    dh = LANES
    qw = N_HEADS * dh
    kvw = N_KV_HEADS * dh
    qiw = N_IDX_HEADS * IDX_DIM
    a128, b128 = _rope_tables(pos, dh)
    a64, b64 = _rope_tables(pos, IDX_DIM)
    akw, bkw = _kiwi_tables(pos)
    tn = 512
    (q,) = _proj_rope(h, w_in, 0, qw, q_gain, a128, b128, [BF16], norm=True, half=dh // 2, tm=tm, tn=tn)
    k32, k16 = _proj_rope(h, w_in, qw, kvw, k_gain, a128, b128, [F32, BF16], norm=True, half=dh // 2, tm=tm, tn=tn)
    v32, v16 = _proj_plain(h, w_in, qw + kvw, kvw, [F32, BF16], act=None, tm=tm, tn=tn)
    if stack_qi:
        (qi,) = _proj_rope(h, w_in, qw + 2 * kvw, qiw, q_gain, a64, b64, [BF16], norm=False, half=IDX_DIM // 2,
                           tm=tm, tn=qiw, layout="stack")
    else:
        (qi,) = _proj_rope(h, w_in, qw + 2 * kvw, qiw, q_gain, a64, b64, [BF16], norm=False, half=IDX_DIM // 2,
                           tm=tm, tn=tn)
    if stack_qi:
        kw32, kiab = _proj_rope(h, w_kiwi, 0, LANES, q_gain, akw, bkw, [F32, BF16], norm=False,
                                half=IDX_DIM // 2, tm=tm, tn=LANES, layout="ki_split")
    else:
        (kw32,), kiab = _proj_rope(h, w_kiwi, 0, LANES, q_gain, akw, bkw, [F32], norm=False,
                                   half=IDX_DIM // 2, tm=tm, tn=LANES), None
    return q, qi, k32, k16, v32, v16, kw32, kiab


def _hgrn_projections(h, w_in, log_lb, log_1mlb, tm):
    d = h.shape[1]
    tn = 512
    (qs,) = _proj_plain(h, w_in, 0, d, [F32], act="silu", tm=tm, tn=tn)
    lf, kk = _proj_forget(h, w_in, d, d, log_lb, log_1mlb, tm=tm, tn=tn)
    (vv,) = _proj_plain(h, w_in, 2 * d, d, [F32], act=None, tm=tm, tn=tn)
    (sg,) = _proj_plain(h, w_in, 3 * d, d, [F32], act="silu", tm=tm, tn=tn)
    return qs, kk, vv, lf, sg


def _stack(arrs):
    return arrs[0][None] if len(arrs) == 1 else jnp.stack(arrs)


def kernel(x_prompt, x_sample, c_prompt, c_sample, cache_k, cache_v, cache_kidx, state_hgrn, page_table, norm_gain, w_ada, b_ada, w_ff_up, w_ff_down, w_attn_in, w_attn_out, q_norm, k_norm, w_rec_in, w_rec_out, rec_out_norm, lb_logits):
    batch, seq, d = x_prompt.shape
    bs, n_tok, _ = x_sample.shape
    depth = w_ada.shape[0]
    n_pages = page_table.shape[1]
    past = n_pages * PAGE
    dh = LANES
    kvw = N_KV_HEADS * dh

    c_all = jnp.concatenate([c_prompt, jnp.zeros((SUBLANES - batch, d), F32), c_sample], axis=0)
    mod = _ada_mod(c_all, w_ada, b_ada)
    tm_p = min(1024, seq)
    mod_p = _Mod(mod[:, :batch].reshape(depth, batch, 9, 1, d), False, seq // tm_p)
    mod_s = _Mod(mod[:, SUBLANES:].reshape(depth, bs, 9, d).transpose(0, 2, 1, 3), True, None)

    gains = norm_gain.reshape(depth, 3, 1, d)
    p = jnp.exp(lb_logits - jnp.max(lb_logits, axis=0, keepdims=True))
    p = p / jnp.sum(p, axis=0, keepdims=True)
    lb_all = jnp.cumsum(p, axis=0) - p[0]

    xp = x_prompt.reshape(batch * seq, d)
    xs = x_sample.transpose(1, 0, 2).reshape(n_tok * bs, d)
    pos_p = jnp.arange(seq)
    pos_s = jnp.repeat(past + jnp.arange(n_tok), bs)
    ms = n_tok * bs

    outs_p, outs_s = {}, {}
    for i in range(depth):
        j = i // 2
        xp = _ffn(xp, mod_p, i, 0, gains, w_ff_up, w_ff_down, 0, tm_p, 512)
        xs = _ffn(xs, mod_s, i, 0, gains, w_ff_up, w_ff_down, 0, ms, 512)
        hp = _prenorm(xp, mod_p, i, 1, gains, tm_p)
        hs = _prenorm(xs, mod_s, i, 1, gains, bs)
        if i % 2 == 0:
            w_in = w_attn_in[j]
            col = N_HEADS * dh + 2 * kvw + N_IDX_HEADS * IDX_DIM
            w_kiwi = jnp.pad(w_in[:, col:], ((0, 0), (0, LANES - (w_in.shape[1] - col))))
            qg = q_norm[j].reshape(1, dh)
            kg = k_norm[j].reshape(1, dh)
            q, qi, k32, k16, v32, v16, kw32, kiab = _attn_projections(hp, w_in, w_kiwi, qg, kg, pos_p, tm_p // 2, True)
            o = _attn_prompt(q, qi, kw32, k16, v16, kiab, batch, seq)
            xp = _mm_res(o, w_attn_out[j], xp, mod_p, i, 5, tm=tm_p, tn=512)
            outs_p[i] = (k32, v32, kw32[:, :IDX_DIM])
            q, qi, k32, k16, v32, v16, kw32, _ = _attn_projections(hs, w_in, w_kiwi, qg, kg, pos_s, bs, False)

            def bmajor(a):
                return a.reshape(n_tok, bs, a.shape[1]).transpose(1, 0, 2)

            qi_b = bmajor(qi).reshape(bs, n_tok, N_IDX_HEADS, IDX_DIM)
            qit = qi_b.transpose(0, 3, 2, 1).reshape(bs, IDX_DIM, N_IDX_HEADS * n_tok)
            wi_b = bmajor(kw32)[:, :, IDX_DIM:IDX_DIM + N_IDX_HEADS] * (IDX_DIM ** -0.5)
            amat = (wi_b[:, :, :, None] * jnp.eye(n_tok, dtype=F32)[None, :, None, :]).reshape(bs, n_tok, -1)
            amat = jnp.concatenate([amat, amat], axis=1)
            rows_pad = 2 * n_tok

            def pad_rows(a):
                return jnp.pad(a, ((0, 0), (0, rows_pad - a.shape[1]), (0, 0)))

            ki_new = pad_rows(bmajor(kw32)[:, :, :IDX_DIM].astype(BF16))
            n_pool = cache_k.shape[1]
            keys = _idx_sample(page_table, qit, amat, ki_new, cache_kidx.reshape((-1,) + cache_kidx.shape[2:]), j * n_pool)
            width = keys.shape[2]
            topk = min(INDEX_TOPK, (past + n_tok) // 4)
            bias = _topk_bias(keys.reshape(bs * rows_pad, width), topk, min(256, bs * rows_pad)).reshape(bs, rows_pad, width)
            q64 = bmajor(q).reshape(bs, n_tok, N_HEADS, dh).transpose(0, 2, 1, 3).reshape(bs, N_HEADS * n_tok, dh)
            o64 = _attn_sample(page_table, q64, bias, pad_rows(bmajor(k16)), pad_rows(bmajor(v16)),
                               cache_k.reshape((-1,) + cache_k.shape[2:]), cache_v.reshape((-1,) + cache_v.shape[2:]),
                               j * n_pool)
            o = o64.reshape(bs, N_HEADS, n_tok, dh).transpose(2, 0, 1, 3).reshape(ms, N_HEADS * dh)
            xs = _mm_res(o, w_attn_out[j], xs, mod_s, i, 5, tm=ms, tn=512)
            outs_s[i] = (bmajor(k32), bmajor(v32), bmajor(kw32)[:, :, :IDX_DIM])
        else:
            lb = lb_all[i].reshape(1, d)
            log_lb, log_1mlb = jnp.log(lb), jnp.log1p(-lb)
            gain = rec_out_norm[j].reshape(1, d)
            qs, kk, vv, lf, sg = _hgrn_projections(hp, w_rec_in[j], log_lb, log_1mlb, tm_p // 2)
            y, st = _gla_prompt(qs, kk, vv, lf, sg, gain, batch, seq, min(512, seq))
            xp = _mm_res(y, w_rec_out[j], xp, mod_p, i, 5, tm=tm_p, tn=512)
            outs_p[i] = (st,)
            qs, kk, vv, lf, sg = _hgrn_projections(hs, w_rec_in[j], log_lb, log_1mlb, bs)

            def bpad(a):
                a = a.reshape(n_tok, bs, d).transpose(1, 0, 2)
                return jnp.pad(a, ((0, 0), (0, SUBLANES - n_tok), (0, 0)))

            y, st = _gla_sample(bpad(qs), bpad(kk), bpad(vv), bpad(lf), bpad(sg), gain,
                                state_hgrn.reshape((-1,) + state_hgrn.shape[2:]), j * bs, n_tok)
            y = y[:, :n_tok].transpose(1, 0, 2).reshape(ms, d)
            xs = _mm_res(y, w_rec_out[j], xs, mod_s, i, 5, tm=ms, tn=512)
            outs_s[i] = (st,)
        xp = _ffn(xp, mod_p, i, 2, gains, w_ff_up, w_ff_down, 1, tm_p, 512)
        xs = _ffn(xs, mod_s, i, 2, gains, w_ff_up, w_ff_down, 1, ms, 512)

    attn_layers = [i for i in range(depth) if i % 2 == 0]
    rec_layers = [i for i in range(depth) if i % 2 == 1]
    na, pages_p = len(attn_layers), seq // PAGE
    y_prompt = xp.reshape(batch, seq, d)
    y_sample = xs.reshape(n_tok, bs, d).transpose(1, 0, 2)
    k_prompt = _stack([outs_p[i][0] for i in attn_layers]).reshape(na, batch, pages_p, PAGE, N_KV_HEADS, dh)
    v_prompt = _stack([outs_p[i][1] for i in attn_layers]).reshape(na, batch, pages_p, PAGE, N_KV_HEADS, dh)
    kidx_prompt = _stack([outs_p[i][2] for i in attn_layers]).reshape(na, batch, pages_p, PAGE, IDX_DIM)
    state_prompt = _stack([outs_p[i][0] for i in rec_layers]).astype(state_hgrn.dtype)
    k_sample = _stack([outs_s[i][0] for i in attn_layers]).reshape(na, bs, n_tok, N_KV_HEADS, dh)
    v_sample = _stack([outs_s[i][1] for i in attn_layers]).reshape(na, bs, n_tok, N_KV_HEADS, dh)
    kidx_sample = _stack([outs_s[i][2] for i in attn_layers])
    state_sample = _stack([outs_s[i][0] for i in rec_layers]).astype(state_hgrn.dtype)
    return (y_prompt, y_sample, k_prompt, v_prompt, kidx_prompt, state_prompt, k_sample, v_sample, kidx_sample, state_sample)
```

```python
import functools

import jax
import jax.numpy as jnp
from jax import lax
from jax.experimental import pallas as pl
from jax.experimental.pallas import tpu as pltpu

F32 = jnp.float32
BF16 = jnp.bfloat16
I32 = jnp.int32

LANES = 128
SUBLANES = 8
VMEM_LIMIT = 60 * 1024 * 1024

N_HEADS = 16
N_KV_HEADS = 4
N_IDX_HEADS = 16
IDX_DIM = 64
INDEX_TOPK = 256
ROPE_THETA = 10000.0
REC_CHUNK = 64
EPS = 1e-6
PAGE = 128
NEG = -1e30
IMIN = -2 ** 31


def _cparams(*sem):
    return pltpu.CompilerParams(dimension_semantics=sem, vmem_limit_bytes=VMEM_LIMIT)


def _rows(mod, tm):
    rm = mod.shape[0]
    if rm == 1 or rm == tm:
        return mod
    return jnp.concatenate([mod] * (tm // rm), axis=0)


def _ada_norm(x, gain, shift, scale):
    tm = x.shape[0]
    y = x * lax.rsqrt(jnp.mean(x * x, axis=-1, keepdims=True) + EPS) * gain
    return y * (1.0 + _rows(scale, tm)) + _rows(shift, tm)


def _silu(x):
    return x * jax.nn.sigmoid(x)


def _ada_mod_kernel(c_ref, w_ref, b_ref, o_ref):
    cs = _silu(c_ref[...]).astype(BF16)
    o_ref[...] = jnp.dot(cs, w_ref[...].astype(BF16), preferred_element_type=F32) + b_ref[...]


def _ada_mod(c_all, w_ada, b_ada):
    depth, d, n = w_ada.shape
    m = c_all.shape[0]
    tn = 2048
    return pl.pallas_call(
        _ada_mod_kernel,
        grid=(depth, n // tn),
        in_specs=[
            pl.BlockSpec((m, d), lambda i, j: (0, 0)),
            pl.BlockSpec((None, d, tn), lambda i, j: (i, 0, j)),
            pl.BlockSpec((None, 1, tn), lambda i, j: (i, 0, j)),
        ],
        out_specs=pl.BlockSpec((None, m, tn), lambda i, j: (i, 0, j)),
        out_shape=jax.ShapeDtypeStruct((depth, m, n), F32),
        compiler_params=_cparams("parallel", "parallel"),
        name="ada_mod",
    )(c_all, w_ada, b_ada.reshape(depth, 1, n))


class _Mod:
    def __init__(self, arr, per_row, tiles_per_seq):
        self.arr = arr
        self.per_row = per_row
        self.tps = tiles_per_seq

    def spec(self, layer, k, tn=None, n_axis=False):
        d = self.arr.shape[-1]
        tn = d if tn is None else tn
        if self.per_row:
            rows = self.arr.shape[2]
            if n_axis:
                return pl.BlockSpec((None, None, rows, tn), lambda m, n: (layer, k, 0, n))
            return pl.BlockSpec((None, None, rows, tn), lambda m, *_: (layer, k, 0, 0))
        tps = self.tps
        if n_axis:
            return pl.BlockSpec((None, None, None, 1, tn), lambda m, n: (layer, m // tps, k, 0, n))
        return pl.BlockSpec((None, None, None, 1, tn), lambda m, *_: (layer, m // tps, k, 0, 0))


def _ffn_kernel(x_ref, sh_ref, sc_ref, gt_ref, gain_ref, wa_ref, wb_ref, wd_ref, o_ref, h_ref):
    j = pl.program_id(1)

    @pl.when(j == 0)
    def _():
        h_ref[...] = _ada_norm(x_ref[...], gain_ref[...], sh_ref[...], sc_ref[...]).astype(BF16)
        o_ref[...] = jnp.zeros_like(o_ref)

    h = h_ref[...]
    tf = wa_ref.shape[1]
    nsplit = 2 if tf % (2 * LANES) == 0 else 1
    gs = []
    for c in range(nsplit):
        cols = slice(c * tf // nsplit, (c + 1) * tf // nsplit)
        a = jnp.dot(h, wa_ref[:, cols].astype(BF16), preferred_element_type=F32)
        b = jnp.dot(h, wb_ref[:, cols].astype(BF16), preferred_element_type=F32)
        gs.append((_silu(a) * b).astype(BF16))
    g = gs[0] if nsplit == 1 else jnp.concatenate(gs, axis=1)
    o_ref[...] += jnp.dot(g, wd_ref[...].astype(BF16), preferred_element_type=F32)

    @pl.when(j == pl.num_programs(1) - 1)
    def _():
        tm = o_ref.shape[0]
        o_ref[...] = x_ref[...] + 0.5 * (1.0 + _rows(gt_ref[...], tm)) * o_ref[...]


def _ffn(x, mod, layer, sub, gain, w_up, w_down, ffn_idx, tm, tf):
    m, d = x.shape
    f = w_down.shape[2]
    nf = f // tf
    return pl.pallas_call(
        _ffn_kernel,
        grid=(m // tm, nf),
        in_specs=[
            pl.BlockSpec((tm, d), lambda i, j: (i, 0), pipeline_mode=pl.Buffered(1)),
            mod.spec(layer, 3 * sub + 0),
            mod.spec(layer, 3 * sub + 1),
            mod.spec(layer, 3 * sub + 2),
            pl.BlockSpec((None, None, 1, d), lambda i, j: (layer, sub, 0, 0)),
            pl.BlockSpec((None, None, d, tf), lambda i, j: (layer, ffn_idx, 0, j)),
            pl.BlockSpec((None, None, d, tf), lambda i, j: (layer, ffn_idx, 0, j + nf)),
            pl.BlockSpec((None, None, tf, d), lambda i, j: (layer, ffn_idx, j, 0)),
        ],
        out_specs=pl.BlockSpec((tm, d), lambda i, j: (i, 0), pipeline_mode=pl.Buffered(1)),
        out_shape=jax.ShapeDtypeStruct((m, d), F32),
        scratch_shapes=[pltpu.VMEM((tm, d), BF16)],
        compiler_params=_cparams("parallel", "arbitrary"),
        name="ffn",
    )(x, mod.arr, mod.arr, mod.arr, gain, w_up, w_up, w_down)


def _prenorm_kernel(x_ref, sh_ref, sc_ref, gain_ref, h_ref):
    h_ref[...] = _ada_norm(x_ref[...], gain_ref[...], sh_ref[...], sc_ref[...]).astype(BF16)


def _prenorm(x, mod, layer, sub, gain, tm):
    m, d = x.shape
    return pl.pallas_call(
        _prenorm_kernel,
        grid=(m // tm,),
        in_specs=[
            pl.BlockSpec((tm, d), lambda i: (i, 0)),
            mod.spec(layer, 3 * sub + 0),
            mod.spec(layer, 3 * sub + 1),
            pl.BlockSpec((None, None, 1, d), lambda i: (layer, sub, 0, 0)),
        ],
        out_specs=pl.BlockSpec((tm, d), lambda i: (i, 0)),
        out_shape=jax.ShapeDtypeStruct((m, d), BF16),
        compiler_params=_cparams("parallel"),
        name="prenorm",
    )(x, mod.arr, mod.arr, gain)


def _rot_half(y, half):
    if 2 * half == LANES:
        return pltpu.roll(y, half, 1)
    lane = lax.broadcasted_iota(I32, y.shape, 1)
    return jnp.where(lane % (2 * half) < half, pltpu.roll(y, LANES - half, 1), pltpu.roll(y, half, 1))


def _cached_bf16(w_ref, wbf_ref):
    @pl.when(pl.program_id(1) == 0)
    def _():
        wbf_ref[...] = w_ref[...].astype(BF16)
    return wbf_ref[...]


def _proj_rope_kernel(a_ref, w_ref, g_ref, ca_ref, cb_ref, *refs, norm, half, layout):
    o_refs, wbf_ref = refs[:-1], refs[-1]
    y = jnp.dot(a_ref[...], _cached_bf16(w_ref, wbf_ref), preferred_element_type=F32)
    ca = ca_ref[...]
    cb = cb_ref[...]
    tm = y.shape[0]
    ncol = y.shape[1] // LANES
    for c in range(ncol):
        yc = y[:, c * LANES:(c + 1) * LANES]
        if norm:
            yc = yc * lax.rsqrt(jnp.mean(yc * yc, axis=-1, keepdims=True) + EPS) * g_ref[...]
        r = yc * ca + _rot_half(yc, half) * cb
        if layout == "ki_split":
            lane = lax.broadcasted_iota(I32, r.shape, 1)
            lo = jnp.where(lane < IDX_DIM, r, 0.0)
            hi = pltpu.roll(lo, IDX_DIM, 1)
            o_refs[0][...] = r
            for qb in range(tm // PAGE):
                rows = slice(qb * PAGE, (qb + 1) * PAGE)
                o_refs[1][2 * qb * PAGE:(2 * qb + 1) * PAGE, :] = lo[rows].astype(BF16)
                o_refs[1][(2 * qb + 1) * PAGE:(2 * qb + 2) * PAGE, :] = hi[rows].astype(BF16)
        elif layout == "stack":
            for qb in range(tm // PAGE):
                o_refs[0][(qb * ncol + c) * PAGE:(qb * ncol + c + 1) * PAGE, :] = (
                    r[qb * PAGE:(qb + 1) * PAGE].astype(o_refs[0].dtype))
        else:
            for o_ref in o_refs:
                o_ref[:, c * LANES:(c + 1) * LANES] = r.astype(o_ref.dtype)


def _proj_rope(a, w, col0, n, gain, tab_a, tab_b, out_dtypes, *, norm, half, tm, tn, layout="cols"):
    m, k = a.shape
    ntab = tab_a.shape[0] // tm
    cb0 = col0 // tn
    if layout == "stack":
        assert tn == n and tm % PAGE == 0
        out_specs = [pl.BlockSpec((tm * (n // LANES), LANES), lambda j, i: (i, 0))]
        out_shape = [jax.ShapeDtypeStruct((m * (n // LANES), LANES), out_dtypes[0])]
    elif layout == "ki_split":
        assert tn == n == LANES and tm % PAGE == 0
        out_specs = [pl.BlockSpec((tm, LANES), lambda j, i: (i, 0)), pl.BlockSpec((2 * tm, LANES), lambda j, i: (i, 0))]
        out_shape = [jax.ShapeDtypeStruct((m, LANES), F32), jax.ShapeDtypeStruct((2 * m, LANES), BF16)]
    else:
        out_specs = [pl.BlockSpec((tm, tn), lambda j, i: (i, j)) for _ in out_dtypes]
        out_shape = [jax.ShapeDtypeStruct((m, n), dt) for dt in out_dtypes]
    outs = pl.pallas_call(
        functools.partial(_proj_rope_kernel, norm=norm, half=half, layout=layout),
        grid=(n // tn, m // tm),
        in_specs=[
            pl.BlockSpec((tm, k), lambda j, i: (i, 0)),
            pl.BlockSpec((k, tn), lambda j, i: (0, cb0 + j)),
            pl.BlockSpec((1, LANES), lambda j, i: (0, 0)),
            pl.BlockSpec((tm, LANES), lambda j, i: (i % ntab, 0)),
            pl.BlockSpec((tm, LANES), lambda j, i: (i % ntab, 0)),
        ],
        out_specs=out_specs,
        out_shape=out_shape,
        scratch_shapes=[pltpu.VMEM((k, tn), BF16)],
        compiler_params=_cparams("parallel", "arbitrary"),
        name="proj_rope",
    )(a, w, gain, tab_a, tab_b)
    return outs


def _proj_plain_kernel(a_ref, w_ref, *refs, act):
    o_refs, wbf_ref = refs[:-1], refs[-1]
    y = jnp.dot(a_ref[...], _cached_bf16(w_ref, wbf_ref), preferred_element_type=F32)
    if act == "silu":
        y = _silu(y)
    for o_ref in o_refs:
        o_ref[...] = y.astype(o_ref.dtype)


def _proj_plain(a, w, col0, n, out_dtypes, *, act, tm, tn):
    m, k = a.shape
    cb0 = col0 // tn
    return pl.pallas_call(
        functools.partial(_proj_plain_kernel, act=act),
        grid=(n // tn, m // tm),
        in_specs=[
            pl.BlockSpec((tm, k), lambda j, i: (i, 0)),
            pl.BlockSpec((k, tn), lambda j, i: (0, cb0 + j)),
        ],
        out_specs=[pl.BlockSpec((tm, tn), lambda j, i: (i, j)) for _ in out_dtypes],
        out_shape=[jax.ShapeDtypeStruct((m, n), dt) for dt in out_dtypes],
        scratch_shapes=[pltpu.VMEM((k, tn), BF16)],
        compiler_params=_cparams("parallel", "arbitrary"),
        name="proj_plain",
    )(a, w)


def _proj_forget_kernel(a_ref, w_ref, loglb_ref, log1mlb_ref, lf_ref, k_ref, wbf_ref):
    fr = jnp.dot(a_ref[...], _cached_bf16(w_ref, wbf_ref), preferred_element_type=F32)
    t = jnp.log1p(jnp.exp(-jnp.abs(fr)))
    ls_pos = jnp.minimum(fr, 0.0) - t
    ls_neg = jnp.minimum(-fr, 0.0) - t
    a = loglb_ref[...]
    c = log1mlb_ref[...] + ls_pos
    lf_ref[...] = jnp.maximum(a, c) + jnp.log1p(jnp.exp(-jnp.abs(a - c)))
    k_ref[...] = jnp.exp(log1mlb_ref[...] + ls_neg)


def _proj_forget(a, w, col0, n, log_lb, log_1mlb, *, tm, tn):
    m, k = a.shape
    cb0 = col0 // tn
    return pl.pallas_call(
        _proj_forget_kernel,
        grid=(n // tn, m // tm),
        in_specs=[
            pl.BlockSpec((tm, k), lambda j, i: (i, 0)),
            pl.BlockSpec((k, tn), lambda j, i: (0, cb0 + j)),
            pl.BlockSpec((1, tn), lambda j, i: (0, j)),
            pl.BlockSpec((1, tn), lambda j, i: (0, j)),
        ],
        out_specs=[pl.BlockSpec((tm, tn), lambda j, i: (i, j)) for _ in range(2)],
        out_shape=[jax.ShapeDtypeStruct((m, n), F32) for _ in range(2)],
        scratch_shapes=[pltpu.VMEM((k, tn), BF16)],
        compiler_params=_cparams("parallel", "arbitrary"),
        name="proj_forget",
    )(a, w, log_lb, log_1mlb)


def _mm_res_kernel(a_ref, w_ref, x_ref, gt_ref, o_ref):
    y = jnp.dot(a_ref[...], w_ref[...].astype(BF16), preferred_element_type=F32)
    o_ref[...] = x_ref[...] + (1.0 + _rows(gt_ref[...], y.shape[0])) * y


def _mm_res(a, w, x, mod, layer, k_gate, *, tm, tn):
    m, k = a.shape
    n = w.shape[1]
    return pl.pallas_call(
        _mm_res_kernel,
        grid=(m // tm, n // tn),
        in_specs=[
            pl.BlockSpec((tm, k), lambda i, j: (i, 0)),
            pl.BlockSpec((k, tn), lambda i, j: (0, j)),
            pl.BlockSpec((tm, tn), lambda i, j: (i, j)),
            mod.spec(layer, k_gate, tn=tn, n_axis=True),
        ],
        out_specs=pl.BlockSpec((tm, tn), lambda i, j: (i, j)),
        out_shape=jax.ShapeDtypeStruct((m, n), F32),
        compiler_params=_cparams("parallel", "parallel"),
        name="mm_res",
    )(a, w, x, mod.arr)


def _sort_key(score):
    bits = pltpu.bitcast(score + 0.0, I32)
    return jnp.where(bits < 0, bits ^ jnp.int32(0x7FFFFFFF), bits)


COUNT_UNROLL = 4


def _attn_prompt_kernel(q_ref, qi_ref, wi_ref, k_ref, v_ref, kiab_ref, o_ref,
                        key_ref, bias_ref, wb_ref, j_ref, m_ref, lp_ref, acc_ref, *, topk, idx_bits, kv_unroll):
    tq = q_ref.shape[0]
    dh = LANES
    rep = N_HEADS // N_KV_HEADS
    qb = pl.program_id(1)
    nkb = qb + 1
    row = lax.broadcasted_iota(I32, (tq, LANES), 0)
    lane = lax.broadcasted_iota(I32, (tq, LANES), 1)
    imin = jnp.int32(IMIN)
    nt = (((1,), (1,)), ((), ()))

    wi = wi_ref[...]
    for h in range(N_IDX_HEADS):
        wcol = wi[:, IDX_DIM + h:IDX_DIM + h + 1] * (IDX_DIM ** -0.5)
        wb_ref[h] = jnp.broadcast_to(wcol, (tq, LANES))

    qi_all = qi_ref[...]

    def score_body(kb, c):
        lg = lax.dot_general(qi_all, kiab_ref[kb], nt, preferred_element_type=F32)
        acc = jnp.zeros((tq, LANES), F32)
        for c2 in range(N_IDX_HEADS // 2):
            r = slice(c2 * tq, (c2 + 1) * tq)
            acc = acc + (wb_ref[2 * c2] * jnp.maximum(lg[r, 0:tq], 0.0)
                         + wb_ref[2 * c2 + 1] * jnp.maximum(lg[r, tq:2 * tq], 0.0))
        adm = jnp.logical_or(kb < qb, lane <= row)
        key_ref[kb] = jnp.where(adm, _sort_key(acc), imin)
        return c

    lax.fori_loop(0, nkb, score_body, 0)
    for u in range(COUNT_UNROLL - 1):
        key_ref[nkb + u] = jnp.full((tq, LANES), IMIN, I32)

    def count(pred):
        def body(it, part):
            w = [jnp.where(pred(it * COUNT_UNROLL + u, key_ref[it * COUNT_UNROLL + u]), 1.0, 0.0)
                 for u in range(COUNT_UNROLL)]
            while len(w) > 1:
                w = [w[i] + w[i + 1] for i in range(0, len(w) - 1, 2)] + ([w[-1]] if len(w) % 2 else [])
            return part + w[0]
        n_it = (nkb + COUNT_UNROLL - 1) // COUNT_UNROLL
        part = lax.fori_loop(0, n_it, body, jnp.zeros((tq, LANES), F32))
        return jnp.sum(part, axis=1, keepdims=True)

    def bit_body(i, tu):
        cu = tu | jnp.left_shift(jnp.int32(1), 31 - i)
        cand = cu ^ imin
        cnt = count(lambda kb, k: k >= cand)
        return jnp.where(cnt >= topk, cu, tu)

    thr = lax.fori_loop(0, 32, bit_body, jnp.zeros((tq, 1), I32)) ^ imin
    cnt_gt = count(lambda kb, k: k > thr)
    cnt_ge = count(lambda kb, k: k >= thr)
    need = topk - cnt_gt
    tie = jnp.logical_and(cnt_ge > topk, thr > imin)

    j_ref[...] = jnp.full(j_ref.shape, 2 ** idx_bits, I32)

    @pl.when(jnp.max(jnp.where(tie, 1.0, 0.0)) > 0.0)
    def _():
        def jbit_body(i, jv):
            cand = jv | jnp.left_shift(jnp.int32(1), idx_bits - 1 - i)
            c = count(lambda kb, k: jnp.logical_and(k == thr, kb * LANES + lane < cand))
            return jnp.where(c < need, cand, jv)
        jv = lax.fori_loop(0, idx_bits, jbit_body, jnp.zeros((tq, 1), I32))
        j_ref[...] = jnp.broadcast_to(jv, j_ref.shape)

    jv = j_ref[...]

    def bias_body(kb, c):
        k = key_ref[kb]
        sel = jnp.logical_or(k > thr, jnp.logical_and(k == thr, kb * LANES + lane <= jv))
        sel = jnp.logical_and(sel, k > imin)
        bias_ref[kb] = jnp.where(sel, 0.0, NEG)
        return c

    lax.fori_loop(0, nkb, bias_body, 0)

    for u in range(kv_unroll - 1):
        bias_ref[nkb + u] = jnp.full((tq, LANES), NEG, F32)
    n_it = (nkb + kv_unroll - 1) // kv_unroll
    span = kv_unroll * tq
    c2 = (dh ** -0.5) * 1.4426950408889634
    for g in range(N_KV_HEADS):
        cols = slice(g * dh, (g + 1) * dh)
        qg = jnp.concatenate([q_ref[:, (g * rep + j) * dh:(g * rep + j + 1) * dh] for j in range(rep)], axis=0)

        def scores(it, qg=qg, cols=cols):
            keys = pl.ds(pl.multiple_of(it * span, span), span)
            s = lax.dot_general(qg, k_ref[keys, cols], nt, preferred_element_type=F32)
            b = jnp.concatenate([bias_ref[it * kv_unroll + u] for u in range(kv_unroll)], axis=1)
            return keys, s, b

        m_ref[...] = jnp.full(m_ref.shape, NEG, F32)

        def pass1(it, c):
            _, s, b = scores(it)
            for j in range(rep):
                sj = s[j * tq:(j + 1) * tq] + b
                t = sj[:, 0:LANES]
                for u in range(1, kv_unroll):
                    t = jnp.maximum(t, sj[:, u * LANES:(u + 1) * LANES])
                m_ref[j * tq:(j + 1) * tq, :] = jnp.maximum(m_ref[j * tq:(j + 1) * tq, :], t)
            return c

        lax.fori_loop(0, n_it, pass1, 0)
        mc = jnp.max(m_ref[...], axis=1, keepdims=True) * c2
        lp_ref[...] = jnp.zeros_like(lp_ref)
        acc_ref[...] = jnp.zeros_like(acc_ref)

        def pass2(it, c, cols=cols, mc=mc):
            keys, s, b = scores(it)
            ps = []
            for j in range(rep):
                p = jnp.exp2((s[j * tq:(j + 1) * tq] + b) * c2 - mc[j * tq:(j + 1) * tq])
                t = p[:, 0:LANES]
                for u in range(1, kv_unroll):
                    t = t + p[:, u * LANES:(u + 1) * LANES]
                lp_ref[j * tq:(j + 1) * tq, :] += t
                ps.append(p.astype(BF16))
            acc_ref[...] += jnp.dot(jnp.concatenate(ps, axis=0), v_ref[keys, cols], preferred_element_type=F32)
            return c

        lax.fori_loop(0, n_it, pass2, 0)
        o = acc_ref[...] / jnp.sum(lp_ref[...], axis=1, keepdims=True)
        for j in range(rep):
            h = g * rep + j
            o_ref[:, h * dh:(h + 1) * dh] = o[j * tq:(j + 1) * tq].astype(o_ref.dtype)


def _attn_prompt(q, qi_stack, kiwi_f32, k_bf, v_bf, kiab, batch, seq):
    m = q.shape[0]
    tq = PAGE
    nq = seq // tq
    topk = min(INDEX_TOPK, seq // 4)
    kvw = k_bf.shape[1]
    npair = N_IDX_HEADS // 2
    rep = N_HEADS // N_KV_HEADS
    kv_unroll = next(u for u in (4, 2, 1) if nq % u == 0)
    kv_spec = pl.BlockSpec((None, seq, kvw), lambda b, i: (b, 0, 0))
    return pl.pallas_call(
        functools.partial(_attn_prompt_kernel, topk=topk, idx_bits=max(1, (seq - 1).bit_length()), kv_unroll=kv_unroll),
        grid=(batch, nq),
        in_specs=[
            pl.BlockSpec((tq, q.shape[1]), lambda b, i: (b * nq + i, 0)),
            pl.BlockSpec((npair * tq, LANES), lambda b, i: (b * nq + i, 0)),
            pl.BlockSpec((tq, LANES), lambda b, i: (b * nq + i, 0)),
            kv_spec, kv_spec,
            pl.BlockSpec((None, nq, 2 * tq, LANES), lambda b, i: (b, 0, 0, 0)),
        ],
        out_specs=pl.BlockSpec((tq, q.shape[1]), lambda b, i: (b * nq + i, 0)),
        out_shape=jax.ShapeDtypeStruct((m, q.shape[1]), BF16),
        scratch_shapes=[
            pltpu.VMEM((nq + COUNT_UNROLL - 1, tq, LANES), I32),
            pltpu.VMEM((nq + kv_unroll - 1, tq, LANES), F32),
            pltpu.VMEM((N_IDX_HEADS, tq, LANES), F32),
            pltpu.VMEM((tq, LANES), I32),
            pltpu.VMEM((rep * tq, LANES), F32),
            pltpu.VMEM((rep * tq, LANES), F32),
            pltpu.VMEM((rep * tq, LANES), F32),
        ],
        compiler_params=_cparams("parallel", "arbitrary"),
        name="attn_prompt",
    )(q, qi_stack, kiwi_f32, k_bf.reshape(batch, seq, kvw), v_bf.reshape(batch, seq, kvw),
      kiab.reshape(batch, nq, 2 * tq, LANES))


def _idx_sample_kernel(pt_ref, qi_ref, a_ref, kinew_ref, *refs, n_pages, n_tok):
    page_refs = refs[:n_pages]
    o_ref = refs[n_pages]
    a = a_ref[...]
    rows = a.shape[0]
    width = (n_pages + 1) * PAGE
    kit = jnp.concatenate([page_refs[p][...].astype(BF16) for p in range(n_pages)] + [kinew_ref[...]], axis=1)
    lg = jnp.dot(qi_ref[...], kit, preferred_element_type=F32)
    sc = jnp.dot(a, jnp.maximum(lg, 0.0), precision=lax.Precision.HIGHEST, preferred_element_type=F32)
    idx = lax.broadcasted_iota(I32, (rows, width), 1)
    row = lax.broadcasted_iota(I32, (rows, width), 0)
    adm = idx - n_pages * PAGE <= row % n_tok
    o_ref[...] = jnp.where(adm, _sort_key(sc), jnp.int32(IMIN))


def _idx_sample(page_table, qi_ht, amat, ki_new_t, pool_kidx_t, page0):
    bs, n_pages = page_table.shape
    n_tok = qi_ht.shape[1] // N_IDX_HEADS
    rows = amat.shape[1]
    width = (n_pages + 1) * LANES
    page_specs = [pl.BlockSpec((None, IDX_DIM, PAGE), lambda b, pt, p=p: (page0 + pt[b, p], 0, 0))
                  for p in range(n_pages)]
    grid_spec = pltpu.PrefetchScalarGridSpec(
        num_scalar_prefetch=1,
        grid=(bs,),
        in_specs=[
            pl.BlockSpec((None,) + qi_ht.shape[1:], lambda b, pt: (b, 0, 0)),
            pl.BlockSpec((None,) + amat.shape[1:], lambda b, pt: (b, 0, 0)),
            pl.BlockSpec((None,) + ki_new_t.shape[1:], lambda b, pt: (b, 0, 0)),
        ] + page_specs,
        out_specs=pl.BlockSpec((None, rows, width), lambda b, pt: (b, 0, 0)),
    )
    return pl.pallas_call(
        functools.partial(_idx_sample_kernel, n_pages=n_pages, n_tok=n_tok),
        grid_spec=grid_spec,
        out_shape=jax.ShapeDtypeStruct((bs, rows, width), I32),
        compiler_params=_cparams("arbitrary"),
        name="idx_sample",
    )(page_table, qi_ht, amat, ki_new_t, *([pool_kidx_t] * n_pages))


def _topk_bias_kernel(key_ref, bias_ref, *, topk, idx_bits):
    key = key_ref[...]
    rows, width = key.shape
    imin = jnp.int32(IMIN)
    idx = lax.broadcasted_iota(I32, (rows, width), 1)

    def count(pred):
        return jnp.sum(jnp.where(pred, 1.0, 0.0), axis=1, keepdims=True)

    def bit_body(i, tu):
        cu = tu | jnp.left_shift(jnp.int32(1), 31 - i)
        cnt = count(key >= (cu ^ imin))
        return jnp.where(cnt >= topk, cu, tu)

    thr = lax.fori_loop(0, 32, bit_body, jnp.zeros((rows, 1), I32)) ^ imin
    need = topk - count(key > thr)
    eq = key == thr

    def jbit_body(i, jv):
        cand = jv | jnp.left_shift(jnp.int32(1), idx_bits - 1 - i)
        c = count(jnp.logical_and(eq, idx < cand))
        return jnp.where(c < need, cand, jv)

    jv = lax.fori_loop(0, idx_bits, jbit_body, jnp.zeros((rows, 1), I32))
    sel = jnp.logical_or(key > thr, jnp.logical_and(eq, idx <= jv))
    sel = jnp.logical_and(sel, key > imin)
    bias_ref[...] = jnp.where(sel, 0.0, NEG)


def _topk_bias(keys2d, topk, tr):
    rows, width = keys2d.shape
    return pl.pallas_call(
        functools.partial(_topk_bias_kernel, topk=topk, idx_bits=max(1, (width - 1).bit_length())),
        grid=(rows // tr,),
        in_specs=[pl.BlockSpec((tr, width), lambda i: (i, 0))],
        out_specs=pl.BlockSpec((tr, width), lambda i: (i, 0)),
        out_shape=jax.ShapeDtypeStruct((rows, width), F32),
        compiler_params=_cparams("parallel"),
        name="topk_bias",
    )(keys2d)


def _attn_sample_kernel(pt_ref, q_ref, bias_ref, knew_ref, vnew_ref, *refs, n_pages):
    k_refs = refs[:n_pages]
    v_refs = refs[n_pages:2 * n_pages]
    o_ref = refs[2 * n_pages]
    rows, dh = q_ref.shape
    rpg = rows // N_KV_HEADS
    bias8 = bias_ref[...]
    bias = jnp.concatenate([bias8] * (rpg // bias8.shape[0]), axis=0)
    pad = PAGE - knew_ref.shape[0]
    zpad = jnp.zeros((pad, dh), BF16)

    for g in range(N_KV_HEADS):
        qg = q_ref[g * rpg:(g + 1) * rpg, :]
        cols = slice(g * dh, (g + 1) * dh)
        s_blocks = []
        for p in range(n_pages + 1):
            if p < n_pages:
                kp = k_refs[p][pl.ds(g, PAGE, stride=N_KV_HEADS), :].astype(BF16)
            else:
                kp = jnp.concatenate([knew_ref[:, cols], zpad], axis=0)
            s_blocks.append(lax.dot_general(qg, kp, (((1,), (1,)), ((), ())), preferred_element_type=F32))
        s = jnp.concatenate(s_blocks, axis=1) * (dh ** -0.5) + bias
        m = jnp.max(s, axis=1, keepdims=True)
        pexp = jnp.exp(s - m)
        l = jnp.sum(pexp, axis=1, keepdims=True)
        pb = pexp.astype(BF16)
        o = jnp.zeros((rpg, dh), F32)
        for p in range(n_pages + 1):
            if p < n_pages:
                vp = v_refs[p][pl.ds(g, PAGE, stride=N_KV_HEADS), :].astype(BF16)
            else:
                vp = jnp.concatenate([vnew_ref[:, cols], zpad], axis=0)
            o = o + jnp.dot(pb[:, p * LANES:(p + 1) * LANES], vp, preferred_element_type=F32)
        o_ref[g * rpg:(g + 1) * rpg, :] = (o / l).astype(o_ref.dtype)


def _attn_sample(page_table, q64, bias, k_new, v_new, pool_k, pool_v, page0):
    bs, n_pages = page_table.shape
    pspecs = [pl.BlockSpec((None,) + pool_k.shape[1:], lambda b, pt, p=p: (page0 + pt[b, p], 0, 0))
              for p in range(n_pages)]
    npg = len(pspecs)
    grid_spec = pltpu.PrefetchScalarGridSpec(
        num_scalar_prefetch=1,
        grid=(bs,),
        in_specs=[
            pl.BlockSpec((None,) + q64.shape[1:], lambda b, pt: (b, 0, 0)),
            pl.BlockSpec((None,) + bias.shape[1:], lambda b, pt: (b, 0, 0)),
            pl.BlockSpec((None,) + k_new.shape[1:], lambda b, pt: (b, 0, 0)),
            pl.BlockSpec((None,) + v_new.shape[1:], lambda b, pt: (b, 0, 0)),
        ] + pspecs + pspecs,
        out_specs=pl.BlockSpec((None,) + q64.shape[1:], lambda b, pt: (b, 0, 0)),
    )
    return pl.pallas_call(
        functools.partial(_attn_sample_kernel, n_pages=n_pages),
        grid_spec=grid_spec,
        out_shape=jax.ShapeDtypeStruct(q64.shape, BF16),
        compiler_params=_cparams("arbitrary"),
        name="attn_sample",
    )(page_table, q64, bias, k_new, v_new, *([pool_k] * npg), *([pool_v] * npg))


def _gla_out(o, sg, gain):
    return (o * lax.rsqrt(jnp.mean(o * o, axis=-1, keepdims=True) + EPS) * gain * sg)


GLA_CHUNK = 16


def _gla_prompt_kernel(q_ref, k_ref, v_ref, g_ref, sg_ref, gain_ref, y_ref, s_ref, st_ref, *, blk):
    ci = pl.program_id(2)
    c = GLA_CHUNK

    @pl.when(ci == 0)
    def _():
        st_ref[...] = jnp.zeros_like(st_ref)

    tri = (lax.broadcasted_iota(I32, (blk, blk), 0) >= lax.broadcasted_iota(I32, (blk, blk), 1)).astype(F32)
    ones = jnp.ones((LANES, LANES), BF16)
    nsub = c // SUBLANES
    nch = blk // c
    gain = gain_ref[...]
    r8 = lax.broadcasted_iota(I32, (SUBLANES, LANES), 0)
    mask_bias = [jnp.where(r8 >= j, 0.0, NEG) for j in range(SUBLANES)]
    log2e = 1.4426950408889634

    def body(bi, carry):
        r0 = pl.multiple_of(bi * blk, blk)
        bb = jnp.dot(tri, g_ref[pl.ds(r0, blk), :] * log2e, precision=lax.Precision.HIGHEST,
                     preferred_element_type=F32)

        qs, vs, bs, xs = [], [], [], []
        for ch in range(nch):
            lo = ch * c
            rows = pl.ds(r0 + lo, c)
            q, k, v = q_ref[rows, :], k_ref[rows, :], v_ref[rows, :]
            b = bb[lo:lo + c] - bb[lo - 1:lo] if ch > 0 else bb[lo:lo + c]
            qs.append(q)
            vs.append(v)
            bs.append((b, k))
            for s in range(c):
                i0 = s // SUBLANES
                qk = q[i0 * SUBLANES:] * k[s:s + 1]
                diff = b[i0 * SUBLANES:] - b[s:s + 1]
                for i in range(i0, nsub):
                    d = diff[(i - i0) * SUBLANES:(i - i0 + 1) * SUBLANES]
                    if i == i0:
                        d = d + mask_bias[s % SUBLANES]
                    xs.append(qk[(i - i0) * SUBLANES:(i - i0 + 1) * SUBLANES] * jnp.exp2(d))
        a_all = jnp.dot(jnp.concatenate(xs, axis=0).astype(BF16), ones, preferred_element_type=F32)
        upds, decs, qes = [], [], []
        for ch in range(nch):
            b, k = bs[ch]
            b_end = b[c - 1:c]
            kd = (k * jnp.exp2(b_end - b)).astype(BF16)
            upds.append(lax.dot_general(vs[ch].astype(BF16), kd, (((0,), (0,)), ((), ())), preferred_element_type=F32))
            decs.append(jnp.exp2(b_end))
            qes.append((qs[ch] * jnp.exp2(b)).astype(BF16))

        st = st_ref[...]
        off = 0
        for ch in range(nch):
            o = lax.dot_general(qes[ch], st.astype(BF16), (((1,), (1,)), ((), ())), preferred_element_type=F32)
            st = st * decs[ch] + upds[ch]
            o_parts = [o[i * SUBLANES:(i + 1) * SUBLANES] for i in range(nsub)]
            v = vs[ch]
            for s in range(c):
                for i in range(s // SUBLANES, nsub):
                    o_parts[i] = o_parts[i] + a_all[off:off + SUBLANES] * v[s:s + 1]
                    off += SUBLANES
            rows = pl.ds(r0 + ch * c, c)
            y_ref[rows, :] = _gla_out(jnp.concatenate(o_parts, axis=0), sg_ref[rows, :], gain).astype(y_ref.dtype)
        st_ref[...] = st
        return carry

    lax.fori_loop(0, q_ref.shape[0] // blk, body, 0)

    @pl.when(ci == pl.num_programs(2) - 1)
    def _():
        s_ref[...] = st_ref[...].T


def _gla_prompt(qs, kk, vv, lf, sg, gain, batch, seq, tb):
    m, d = qs.shape
    nh = d // LANES
    nc = seq // tb
    tok = pl.BlockSpec((tb, LANES), lambda b, h, c: (b * nc + c, h))
    return pl.pallas_call(
        functools.partial(_gla_prompt_kernel, blk=min(LANES, tb)),
        grid=(batch, nh, nc),
        in_specs=[tok, tok, tok, tok, tok, pl.BlockSpec((1, LANES), lambda b, h, c: (0, h))],
        out_specs=[tok, pl.BlockSpec((None, None, LANES, LANES), lambda b, h, c: (b, h, 0, 0))],
        out_shape=[jax.ShapeDtypeStruct((m, d), BF16), jax.ShapeDtypeStruct((batch, nh, LANES, LANES), F32)],
        scratch_shapes=[pltpu.VMEM((LANES, LANES), F32)],
        compiler_params=_cparams("parallel", "parallel", "arbitrary"),
        name="gla_prompt",
    )(qs, kk, vv, lf, sg, gain)


def _gla_sample_kernel(q_ref, k_ref, v_ref, g_ref, sg_ref, gain_ref, s0_ref, y_ref, s_ref, *, n_tok):
    nh = s0_ref.shape[0]
    tp = q_ref.shape[0]
    log2e = 1.4426950408889634
    tri = (lax.broadcasted_iota(I32, (tp, tp), 0) >= lax.broadcasted_iota(I32, (tp, tp), 1)).astype(F32)
    b_all = jnp.dot(tri, g_ref[...] * log2e, precision=lax.Precision.HIGHEST, preferred_element_type=F32)
    b_end_all = b_all[tp - 1:tp]
    r8 = lax.broadcasted_iota(I32, (tp, LANES), 0)
    mask_bias = [jnp.where(r8 >= j, 0.0, NEG) for j in range(n_tok)]
    row0 = lax.broadcasted_iota(I32, b_all.shape, 0) == 0
    dec_cols = lax.dot_general(jnp.where(row0, jnp.exp2(b_end_all), 0.0), jnp.ones((tp, LANES), F32),
                               (((0,), (0,)), ((), ())), precision=lax.Precision.HIGHEST,
                               preferred_element_type=F32)
    xs = []
    for h in range(nh):
        c = slice(h * LANES, (h + 1) * LANES)
        q, k, b = q_ref[:, c], k_ref[:, c], b_all[:, c]
        for s in range(n_tok):
            xs.append(q * k[s:s + 1] * jnp.exp2(b - b[s:s + 1] + mask_bias[s]))
    a_all = jnp.dot(jnp.concatenate(xs, axis=0).astype(BF16), jnp.ones((LANES, LANES), BF16),
                    preferred_element_type=F32)
    for h in range(nh):
        c = slice(h * LANES, (h + 1) * LANES)
        q, k, v, b = q_ref[:, c], k_ref[:, c], v_ref[:, c], b_all[:, c]
        st0 = s0_ref[h]
        o = jnp.dot((q * jnp.exp2(b)).astype(BF16), st0.astype(BF16), preferred_element_type=F32)
        for s in range(n_tok):
            o = o + a_all[(h * n_tok + s) * tp:(h * n_tok + s + 1) * tp] * v[s:s + 1]
        kd = (k * jnp.exp2(b_end_all[:, c] - b)).astype(BF16)
        upd = lax.dot_general(kd, v.astype(BF16), (((0,), (0,)), ((), ())), preferred_element_type=F32)
        s_ref[h] = dec_cols[h * LANES:(h + 1) * LANES] * st0 + upd
        y_ref[:, c] = _gla_out(o, sg_ref[:, c], gain_ref[:, c]).astype(y_ref.dtype)


def _gla_sample(qs, kk, vv, lf, sg, gain, state_pool, seq0, n_tok):
    bs, tp, d = qs.shape
    nh = state_pool.shape[1]
    tok = pl.BlockSpec((None, tp, d), lambda b: (b, 0, 0))
    st_in = pl.BlockSpec((None, nh, LANES, LANES), lambda b: (seq0 + b, 0, 0, 0))
    st = pl.BlockSpec((None, nh, LANES, LANES), lambda b: (b, 0, 0, 0))
    return pl.pallas_call(
        functools.partial(_gla_sample_kernel, n_tok=n_tok),
        grid=(bs,),
        in_specs=[tok, tok, tok, tok, tok, pl.BlockSpec((1, d), lambda b: (0, 0)), st_in],
        out_specs=[tok, st],
        out_shape=[jax.ShapeDtypeStruct((bs, tp, d), BF16), jax.ShapeDtypeStruct((bs, nh, LANES, LANES), F32)],
        compiler_params=_cparams("parallel"),
        name="gla_sample",
    )(qs, kk, vv, lf, sg, gain, state_pool)


def _rope_tables(pos, d):
    inv = ROPE_THETA ** (-jnp.arange(0, d, 2, dtype=F32) / d)
    ang = pos.astype(F32)[:, None] * inv[None, :]
    cos, sin = jnp.cos(ang), jnp.sin(ang)
    reps = LANES // d
    a = jnp.tile(jnp.concatenate([cos, cos], axis=1), (1, reps))
    b = jnp.tile(jnp.concatenate([-sin, sin], axis=1), (1, reps))
    return a, b


def _kiwi_tables(pos):
    a64, b64 = _rope_tables(pos, IDX_DIM)
    p = pos.shape[0]
    lane = jnp.arange(IDX_DIM)
    wi_scale = jnp.where(lane < N_IDX_HEADS, N_IDX_HEADS ** -0.5, 0.0).astype(F32)
    a = jnp.concatenate([a64[:, :IDX_DIM], jnp.broadcast_to(wi_scale, (p, IDX_DIM))], axis=1)
    b = jnp.concatenate([b64[:, :IDX_DIM], jnp.zeros((p, IDX_DIM), F32)], axis=1)
    return a, b


def _attn_projections(h, w_in, w_kiwi, q_gain, k_gain, pos, tm, stack_qi):
    dh = LANES
    qw = N_HEADS * dh
    kvw = N_KV_HEADS * dh
    qiw = N_IDX_HEADS * IDX_DIM
    a128, b128 = _rope_tables(pos, dh)
    a64, b64 = _rope_tables(pos, IDX_DIM)
    akw, bkw = _kiwi_tables(pos)
    tn = 512
    (q,) = _proj_rope(h, w_in, 0, qw, q_gain, a128, b128, [BF16], norm=True, half=dh // 2, tm=tm, tn=tn)
    k32, k16 = _proj_rope(h, w_in, qw, kvw, k_gain, a128, b128, [F32, BF16], norm=True, half=dh // 2, tm=tm, tn=tn)
    tm_plain = 2 * tm if h.shape[0] % (2 * tm) == 0 else tm
    v32, v16 = _proj_plain(h, w_in, qw + kvw, kvw, [F32, BF16], act=None, tm=tm_plain, tn=tn)
    if stack_qi:
        (qi,) = _proj_rope(h, w_in, qw + 2 * kvw, qiw, q_gain, a64, b64, [BF16], norm=False, half=IDX_DIM // 2,
                           tm=tm, tn=qiw, layout="stack")
    else:
        (qi,) = _proj_rope(h, w_in, qw + 2 * kvw, qiw, q_gain, a64, b64, [BF16], norm=False, half=IDX_DIM // 2,
                           tm=tm, tn=tn)
    if stack_qi:
        kw32, kiab = _proj_rope(h, w_kiwi, 0, LANES, q_gain, akw, bkw, [F32, BF16], norm=False,
                                half=IDX_DIM // 2, tm=tm, tn=LANES, layout="ki_split")
    else:
        (kw32,), kiab = _proj_rope(h, w_kiwi, 0, LANES, q_gain, akw, bkw, [F32], norm=False,
                                   half=IDX_DIM // 2, tm=tm, tn=LANES), None
    return q, qi, k32, k16, v32, v16, kw32, kiab


def _hgrn_projections(h, w_in, log_lb, log_1mlb, tm):
    d = h.shape[1]
    tn = 512
    (qs,) = _proj_plain(h, w_in, 0, d, [F32], act="silu", tm=tm, tn=tn)
    lf, kk = _proj_forget(h, w_in, d, d, log_lb, log_1mlb, tm=tm, tn=tn)
    (vv,) = _proj_plain(h, w_in, 2 * d, d, [F32], act=None, tm=tm, tn=tn)
    (sg,) = _proj_plain(h, w_in, 3 * d, d, [F32], act="silu", tm=tm, tn=tn)
    return qs, kk, vv, lf, sg


def _stack(arrs):
    return arrs[0][None] if len(arrs) == 1 else jnp.stack(arrs)


def kernel(x_prompt, x_sample, c_prompt, c_sample, cache_k, cache_v, cache_kidx, state_hgrn, page_table, norm_gain, w_ada, b_ada, w_ff_up, w_ff_down, w_attn_in, w_attn_out, q_norm, k_norm, w_rec_in, w_rec_out, rec_out_norm, lb_logits):
    batch, seq, d = x_prompt.shape
    bs, n_tok, _ = x_sample.shape
    depth = w_ada.shape[0]
    n_pages = page_table.shape[1]
    past = n_pages * PAGE
    dh = LANES
    kvw = N_KV_HEADS * dh

    c_all = jnp.concatenate([c_prompt, jnp.zeros((SUBLANES - batch, d), F32), c_sample], axis=0)
    mod = _ada_mod(c_all, w_ada, b_ada)
    tm_p = min(1024, seq)
    mod_p = _Mod(mod[:, :batch].reshape(depth, batch, 9, 1, d), False, seq // tm_p)
    mod_s = _Mod(mod[:, SUBLANES:].reshape(depth, bs, 9, d).transpose(0, 2, 1, 3), True, None)

    gains = norm_gain.reshape(depth, 3, 1, d)
    p = jnp.exp(lb_logits - jnp.max(lb_logits, axis=0, keepdims=True))
    p = p / jnp.sum(p, axis=0, keepdims=True)
    lb_all = jnp.cumsum(p, axis=0) - p[0]

    xp = x_prompt.reshape(batch * seq, d)
    xs = x_sample.transpose(1, 0, 2).reshape(n_tok * bs, d)
    pos_p = jnp.arange(seq)
    pos_s = jnp.repeat(past + jnp.arange(n_tok), bs)
    ms = n_tok * bs

    outs_p, outs_s = {}, {}
    for i in range(depth):
        j = i // 2
        xp = _ffn(xp, mod_p, i, 0, gains, w_ff_up, w_ff_down, 0, tm_p, 512)
        xs = _ffn(xs, mod_s, i, 0, gains, w_ff_up, w_ff_down, 0, ms, 512)
        hp = _prenorm(xp, mod_p, i, 1, gains, tm_p)
        hs = _prenorm(xs, mod_s, i, 1, gains, bs)
        if i % 2 == 0:
            w_in = w_attn_in[j]
            col = N_HEADS * dh + 2 * kvw + N_IDX_HEADS * IDX_DIM
            w_kiwi = jnp.pad(w_in[:, col:], ((0, 0), (0, LANES - (w_in.shape[1] - col))))
            qg = q_norm[j].reshape(1, dh)
            kg = k_norm[j].reshape(1, dh)
            q, qi, k32, k16, v32, v16, kw32, kiab = _attn_projections(hp, w_in, w_kiwi, qg, kg, pos_p, tm_p // 2, True)
            o = _attn_prompt(q, qi, kw32, k16, v16, kiab, batch, seq)
            xp = _mm_res(o, w_attn_out[j], xp, mod_p, i, 5, tm=tm_p, tn=512)
            outs_p[i] = (k32, v32, kw32[:, :IDX_DIM])
            q, qi, k32, k16, v32, v16, kw32, _ = _attn_projections(hs, w_in, w_kiwi, qg, kg, pos_s, bs, False)

            def bmajor(a):
                return a.reshape(n_tok, bs, a.shape[1]).transpose(1, 0, 2)

            qi_b = bmajor(qi).reshape(bs, n_tok, N_IDX_HEADS, IDX_DIM)
            qi_ht = qi_b.transpose(0, 2, 1, 3).reshape(bs, N_IDX_HEADS * n_tok, IDX_DIM)
            wi_b = bmajor(kw32)[:, :, IDX_DIM:IDX_DIM + N_IDX_HEADS] * (IDX_DIM ** -0.5)
            amat = (wi_b[:, :, :, None] * jnp.eye(n_tok, dtype=F32)[None, :, None, :]).reshape(bs, n_tok, -1)
            amat = jnp.concatenate([amat, amat], axis=1)
            rows_pad = 2 * n_tok

            def pad_rows(a):
                return jnp.pad(a, ((0, 0), (0, rows_pad - a.shape[1]), (0, 0)))

            ki_new_t = jnp.pad(bmajor(kw32)[:, :, :IDX_DIM].astype(BF16).transpose(0, 2, 1),
                               ((0, 0), (0, 0), (0, PAGE - n_tok)))
            n_pool = cache_k.shape[1]
            pool_kidx_t = cache_kidx.reshape((-1,) + cache_kidx.shape[2:]).transpose(0, 2, 1)
            keys = _idx_sample(page_table, qi_ht, amat, ki_new_t, pool_kidx_t, j * n_pool)
            width = keys.shape[2]
            topk = min(INDEX_TOPK, (past + n_tok) // 4)
            bias = _topk_bias(keys.reshape(bs * rows_pad, width), topk, min(256, bs * rows_pad)).reshape(bs, rows_pad, width)
            q64 = bmajor(q).reshape(bs, n_tok, N_HEADS, dh).transpose(0, 2, 1, 3).reshape(bs, N_HEADS * n_tok, dh)
            o64 = _attn_sample(page_table, q64, bias, pad_rows(bmajor(k16)), pad_rows(bmajor(v16)),
                               cache_k.reshape(-1, PAGE * N_KV_HEADS, dh), cache_v.reshape(-1, PAGE * N_KV_HEADS, dh),
                               j * n_pool)
            o = o64.reshape(bs, N_HEADS, n_tok, dh).transpose(2, 0, 1, 3).reshape(ms, N_HEADS * dh)
            xs = _mm_res(o, w_attn_out[j], xs, mod_s, i, 5, tm=ms, tn=512)
            outs_s[i] = (bmajor(k32), bmajor(v32), bmajor(kw32)[:, :, :IDX_DIM])
        else:
            lb = lb_all[i].reshape(1, d)
            log_lb, log_1mlb = jnp.log(lb), jnp.log1p(-lb)
            gain = rec_out_norm[j].reshape(1, d)
            qs, kk, vv, lf, sg = _hgrn_projections(hp, w_rec_in[j], log_lb, log_1mlb, tm_p)
            y, st = _gla_prompt(qs, kk, vv, lf, sg, gain, batch, seq, min(512, seq))
            xp = _mm_res(y, w_rec_out[j], xp, mod_p, i, 5, tm=tm_p, tn=512)
            outs_p[i] = (st,)
            qs, kk, vv, lf, sg = _hgrn_projections(hs, w_rec_in[j], log_lb, log_1mlb, bs)

            def bpad(a):
                a = a.reshape(n_tok, bs, d).transpose(1, 0, 2)
                return jnp.pad(a, ((0, 0), (0, SUBLANES - n_tok), (0, 0)))

            y, st = _gla_sample(bpad(qs), bpad(kk), bpad(vv), bpad(lf), bpad(sg), gain,
                                state_hgrn.reshape((-1,) + state_hgrn.shape[2:]), j * bs, n_tok)
            y = y[:, :n_tok].transpose(1, 0, 2).reshape(ms, d)
            xs = _mm_res(y, w_rec_out[j], xs, mod_s, i, 5, tm=ms, tn=512)
            outs_s[i] = (st,)
        xp = _ffn(xp, mod_p, i, 2, gains, w_ff_up, w_ff_down, 1, tm_p, 512)
        xs = _ffn(xs, mod_s, i, 2, gains, w_ff_up, w_ff_down, 1, ms, 512)

    attn_layers = [i for i in range(depth) if i % 2 == 0]
    rec_layers = [i for i in range(depth) if i % 2 == 1]
    na, pages_p = len(attn_layers), seq // PAGE
    y_prompt = xp.reshape(batch, seq, d)
    y_sample = xs.reshape(n_tok, bs, d).transpose(1, 0, 2)
    k_prompt = _stack([outs_p[i][0] for i in attn_layers]).reshape(na, batch, pages_p, PAGE, N_KV_HEADS, dh)
    v_prompt = _stack([outs_p[i][1] for i in attn_layers]).reshape(na, batch, pages_p, PAGE, N_KV_HEADS, dh)
    kidx_prompt = _stack([outs_p[i][2] for i in attn_layers]).reshape(na, batch, pages_p, PAGE, IDX_DIM)
    state_prompt = _stack([outs_p[i][0] for i in rec_layers]).astype(state_hgrn.dtype)
    k_sample = _stack([outs_s[i][0] for i in attn_layers]).reshape(na, bs, n_tok, N_KV_HEADS, dh)
    v_sample = _stack([outs_s[i][1] for i in attn_layers]).reshape(na, bs, n_tok, N_KV_HEADS, dh)
    kidx_sample = _stack([outs_s[i][2] for i in attn_layers])
    state_sample = _stack([outs_s[i][0] for i in rec_layers]).astype(state_hgrn.dtype)
    return (y_prompt, y_sample, k_prompt, v_prompt, kidx_prompt, state_prompt, k_sample, v_sample, kidx_sample, state_sample)
```

```python
import functools

import jax
import jax.numpy as jnp
from jax import lax
from jax.experimental import pallas as pl
from jax.experimental.pallas import tpu as pltpu

F32 = jnp.float32
BF16 = jnp.bfloat16
I32 = jnp.int32

LANES = 128
SUBLANES = 8
VMEM_LIMIT = 60 * 1024 * 1024

N_HEADS = 16
N_KV_HEADS = 4
N_IDX_HEADS = 16
IDX_DIM = 64
INDEX_TOPK = 256
ROPE_THETA = 10000.0
REC_CHUNK = 64
EPS = 1e-6
PAGE = 128
NEG = -1e30
IMIN = -2 ** 31


def _cparams(*sem):
    return pltpu.CompilerParams(dimension_semantics=sem, vmem_limit_bytes=VMEM_LIMIT)


def _rows(mod, tm):
    rm = mod.shape[0]
    if rm == 1 or rm == tm:
        return mod
    return jnp.concatenate([mod] * (tm // rm), axis=0)


def _ada_norm(x, gain, shift, scale):
    tm = x.shape[0]
    y = x * lax.rsqrt(jnp.mean(x * x, axis=-1, keepdims=True) + EPS) * gain
    return y * (1.0 + _rows(scale, tm)) + _rows(shift, tm)


def _silu(x):
    return x * jax.nn.sigmoid(x)


def _ada_mod_kernel(c_ref, w_ref, b_ref, o_ref):
    cs = _silu(c_ref[...]).astype(BF16)
    o_ref[...] = jnp.dot(cs, w_ref[...].astype(BF16), preferred_element_type=F32) + b_ref[...]


def _ada_mod(c_all, w_ada, b_ada):
    depth, d, n = w_ada.shape
    m = c_all.shape[0]
    tn = 2048
    return pl.pallas_call(
        _ada_mod_kernel,
        grid=(depth, n // tn),
        in_specs=[
            pl.BlockSpec((m, d), lambda i, j: (0, 0)),
            pl.BlockSpec((None, d, tn), lambda i, j: (i, 0, j)),
            pl.BlockSpec((None, 1, tn), lambda i, j: (i, 0, j)),
        ],
        out_specs=pl.BlockSpec((None, m, tn), lambda i, j: (i, 0, j)),
        out_shape=jax.ShapeDtypeStruct((depth, m, n), F32),
        compiler_params=_cparams("parallel", "parallel"),
        name="ada_mod",
    )(c_all, w_ada, b_ada.reshape(depth, 1, n))


class _Mod:
    def __init__(self, arr, per_row, tiles_per_seq):
        self.arr = arr
        self.per_row = per_row
        self.tps = tiles_per_seq

    def spec(self, layer, k, tn=None, n_axis=False):
        d = self.arr.shape[-1]
        tn = d if tn is None else tn
        if self.per_row:
            rows = self.arr.shape[2]
            if n_axis:
                return pl.BlockSpec((None, None, rows, tn), lambda m, n: (layer, k, 0, n))
            return pl.BlockSpec((None, None, rows, tn), lambda m, *_: (layer, k, 0, 0))
        tps = self.tps
        if n_axis:
            return pl.BlockSpec((None, None, None, 1, tn), lambda m, n: (layer, m // tps, k, 0, n))
        return pl.BlockSpec((None, None, None, 1, tn), lambda m, *_: (layer, m // tps, k, 0, 0))


def _ffn_kernel(x_ref, sh_ref, sc_ref, gt_ref, gain_ref, wa_ref, wb_ref, wd_ref, o_ref, h_ref):
    j = pl.program_id(1)

    @pl.when(j == 0)
    def _():
        h_ref[...] = _ada_norm(x_ref[...], gain_ref[...], sh_ref[...], sc_ref[...]).astype(BF16)
        o_ref[...] = jnp.zeros_like(o_ref)

    h = h_ref[...]
    tf = wa_ref.shape[1]
    nsplit = 2 if tf % (2 * LANES) == 0 else 1
    gs = []
    for c in range(nsplit):
        cols = slice(c * tf // nsplit, (c + 1) * tf // nsplit)
        a = jnp.dot(h, wa_ref[:, cols].astype(BF16), preferred_element_type=F32)
        b = jnp.dot(h, wb_ref[:, cols].astype(BF16), preferred_element_type=F32)
        gs.append((_silu(a) * b).astype(BF16))
    g = gs[0] if nsplit == 1 else jnp.concatenate(gs, axis=1)
    o_ref[...] += jnp.dot(g, wd_ref[...].astype(BF16), preferred_element_type=F32)

    @pl.when(j == pl.num_programs(1) - 1)
    def _():
        tm = o_ref.shape[0]
        o_ref[...] = x_ref[...] + 0.5 * (1.0 + _rows(gt_ref[...], tm)) * o_ref[...]


def _ffn(x, mod, layer, sub, gain, w_up, w_down, ffn_idx, tm, tf):
    m, d = x.shape
    f = w_down.shape[2]
    nf = f // tf
    return pl.pallas_call(
        _ffn_kernel,
        grid=(m // tm, nf),
        in_specs=[
            pl.BlockSpec((tm, d), lambda i, j: (i, 0), pipeline_mode=pl.Buffered(1)),
            mod.spec(layer, 3 * sub + 0),
            mod.spec(layer, 3 * sub + 1),
            mod.spec(layer, 3 * sub + 2),
            pl.BlockSpec((None, None, 1, d), lambda i, j: (layer, sub, 0, 0)),
            pl.BlockSpec((None, None, d, tf), lambda i, j: (layer, ffn_idx, 0, j)),
            pl.BlockSpec((None, None, d, tf), lambda i, j: (layer, ffn_idx, 0, j + nf)),
            pl.BlockSpec((None, None, tf, d), lambda i, j: (layer, ffn_idx, j, 0)),
        ],
        out_specs=pl.BlockSpec((tm, d), lambda i, j: (i, 0), pipeline_mode=pl.Buffered(1)),
        out_shape=jax.ShapeDtypeStruct((m, d), F32),
        scratch_shapes=[pltpu.VMEM((tm, d), BF16)],
        compiler_params=_cparams("parallel", "arbitrary"),
        name="ffn",
    )(x, mod.arr, mod.arr, mod.arr, gain, w_up, w_up, w_down)


def _prenorm_kernel(x_ref, sh_ref, sc_ref, gain_ref, h_ref):
    h_ref[...] = _ada_norm(x_ref[...], gain_ref[...], sh_ref[...], sc_ref[...]).astype(BF16)


def _prenorm(x, mod, layer, sub, gain, tm):
    m, d = x.shape
    return pl.pallas_call(
        _prenorm_kernel,
        grid=(m // tm,),
        in_specs=[
            pl.BlockSpec((tm, d), lambda i: (i, 0)),
            mod.spec(layer, 3 * sub + 0),
            mod.spec(layer, 3 * sub + 1),
            pl.BlockSpec((None, None, 1, d), lambda i: (layer, sub, 0, 0)),
        ],
        out_specs=pl.BlockSpec((tm, d), lambda i: (i, 0)),
        out_shape=jax.ShapeDtypeStruct((m, d), BF16),
        compiler_params=_cparams("parallel"),
        name="prenorm",
    )(x, mod.arr, mod.arr, gain)


def _rot_half(y, half):
    if 2 * half == LANES:
        return pltpu.roll(y, half, 1)
    lane = lax.broadcasted_iota(I32, y.shape, 1)
    return jnp.where(lane % (2 * half) < half, pltpu.roll(y, LANES - half, 1), pltpu.roll(y, half, 1))


def _cached_bf16(w_ref, wbf_ref):
    @pl.when(pl.program_id(1) == 0)
    def _():
        wbf_ref[...] = w_ref[...].astype(BF16)
    return wbf_ref[...]


def _proj_rope_kernel(a_ref, w_ref, g_ref, ca_ref, cb_ref, *refs, norm, half, layout):
    o_refs, wbf_ref = refs[:-1], refs[-1]
    y = jnp.dot(a_ref[...], _cached_bf16(w_ref, wbf_ref), preferred_element_type=F32)
    ca = ca_ref[...]
    cb = cb_ref[...]
    tm = y.shape[0]
    ncol = y.shape[1] // LANES
    for c in range(ncol):
        yc = y[:, c * LANES:(c + 1) * LANES]
        if norm:
            yc = yc * lax.rsqrt(jnp.mean(yc * yc, axis=-1, keepdims=True) + EPS) * g_ref[...]
        r = yc * ca + _rot_half(yc, half) * cb
        if layout == "ki_split":
            lane = lax.broadcasted_iota(I32, r.shape, 1)
            lo = jnp.where(lane < IDX_DIM, r, 0.0)
            hi = pltpu.roll(lo, IDX_DIM, 1)
            o_refs[0][...] = r
            for qb in range(tm // PAGE):
                rows = slice(qb * PAGE, (qb + 1) * PAGE)
                o_refs[1][2 * qb * PAGE:(2 * qb + 1) * PAGE, :] = lo[rows].astype(BF16)
                o_refs[1][(2 * qb + 1) * PAGE:(2 * qb + 2) * PAGE, :] = hi[rows].astype(BF16)
        elif layout == "stack":
            for qb in range(tm // PAGE):
                o_refs[0][(qb * ncol + c) * PAGE:(qb * ncol + c + 1) * PAGE, :] = (
                    r[qb * PAGE:(qb + 1) * PAGE].astype(o_refs[0].dtype))
        elif layout == "kv_rows":
            _store_kv_rows(o_refs, r, c, ncol)
        else:
            for o_ref in o_refs:
                o_ref[:, c * LANES:(c + 1) * LANES] = r.astype(o_ref.dtype)


def _store_kv_rows(o_refs, r, c, ncol):
    o_refs[0][pl.ds(c, r.shape[0], stride=ncol), :] = r
    o_refs[1][:, c * LANES:(c + 1) * LANES] = r.astype(BF16)


def _kv_rows_outs(m, n, tm):
    ncol = n // LANES
    specs = [pl.BlockSpec((tm * ncol, LANES), lambda j, i: (i, 0)), pl.BlockSpec((tm, n), lambda j, i: (i, 0))]
    shapes = [jax.ShapeDtypeStruct((m * ncol, LANES), F32), jax.ShapeDtypeStruct((m, n), BF16)]
    return specs, shapes


def _proj_rope(a, w, col0, n, gain, tab_a, tab_b, out_dtypes, *, norm, half, tm, tn, layout="cols"):
    m, k = a.shape
    ntab = tab_a.shape[0] // tm
    cb0 = col0 // tn
    if layout == "stack":
        assert tn == n and tm % PAGE == 0
        out_specs = [pl.BlockSpec((tm * (n // LANES), LANES), lambda j, i: (i, 0))]
        out_shape = [jax.ShapeDtypeStruct((m * (n // LANES), LANES), out_dtypes[0])]
    elif layout == "ki_split":
        assert tn == n == LANES and tm % PAGE == 0
        out_specs = [pl.BlockSpec((tm, LANES), lambda j, i: (i, 0)), pl.BlockSpec((2 * tm, LANES), lambda j, i: (i, 0))]
        out_shape = [jax.ShapeDtypeStruct((m, LANES), F32), jax.ShapeDtypeStruct((2 * m, LANES), BF16)]
    elif layout == "kv_rows":
        assert tn == n
        out_specs, out_shape = _kv_rows_outs(m, n, tm)
    else:
        out_specs = [pl.BlockSpec((tm, tn), lambda j, i: (i, j)) for _ in out_dtypes]
        out_shape = [jax.ShapeDtypeStruct((m, n), dt) for dt in out_dtypes]
    outs = pl.pallas_call(
        functools.partial(_proj_rope_kernel, norm=norm, half=half, layout=layout),
        grid=(n // tn, m // tm),
        in_specs=[
            pl.BlockSpec((tm, k), lambda j, i: (i, 0)),
            pl.BlockSpec((k, tn), lambda j, i: (0, cb0 + j)),
            pl.BlockSpec((1, LANES), lambda j, i: (0, 0)),
            pl.BlockSpec((tm, LANES), lambda j, i: (i % ntab, 0)),
            pl.BlockSpec((tm, LANES), lambda j, i: (i % ntab, 0)),
        ],
        out_specs=out_specs,
        out_shape=out_shape,
        scratch_shapes=[pltpu.VMEM((k, tn), BF16)],
        compiler_params=_cparams("parallel", "arbitrary"),
        name="proj_rope",
    )(a, w, gain, tab_a, tab_b)
    return outs


def _proj_plain_kernel(a_ref, w_ref, *refs, act, kv_rows):
    o_refs, wbf_ref = refs[:-1], refs[-1]
    y = jnp.dot(a_ref[...], _cached_bf16(w_ref, wbf_ref), preferred_element_type=F32)
    if act == "silu":
        y = _silu(y)
    if kv_rows:
        ncol = y.shape[1] // LANES
        for c in range(ncol):
            _store_kv_rows(o_refs, y[:, c * LANES:(c + 1) * LANES], c, ncol)
    else:
        for o_ref in o_refs:
            o_ref[...] = y.astype(o_ref.dtype)


def _proj_plain(a, w, col0, n, out_dtypes, *, act, tm, tn, kv_rows=False):
    m, k = a.shape
    cb0 = col0 // tn
    if kv_rows:
        assert tn == n
        out_specs, out_shape = _kv_rows_outs(m, n, tm)
    else:
        out_specs = [pl.BlockSpec((tm, tn), lambda j, i: (i, j)) for _ in out_dtypes]
        out_shape = [jax.ShapeDtypeStruct((m, n), dt) for dt in out_dtypes]
    return pl.pallas_call(
        functools.partial(_proj_plain_kernel, act=act, kv_rows=kv_rows),
        grid=(n // tn, m // tm),
        in_specs=[
            pl.BlockSpec((tm, k), lambda j, i: (i, 0)),
            pl.BlockSpec((k, tn), lambda j, i: (0, cb0 + j)),
        ],
        out_specs=out_specs,
        out_shape=out_shape,
        scratch_shapes=[pltpu.VMEM((k, tn), BF16)],
        compiler_params=_cparams("parallel", "arbitrary"),
        name="proj_plain",
    )(a, w)


def _proj_forget_kernel(a_ref, w_ref, loglb_ref, log1mlb_ref, lf_ref, k_ref, wbf_ref):
    fr = jnp.dot(a_ref[...], _cached_bf16(w_ref, wbf_ref), preferred_element_type=F32)
    t = jnp.log1p(jnp.exp(-jnp.abs(fr)))
    ls_pos = jnp.minimum(fr, 0.0) - t
    ls_neg = jnp.minimum(-fr, 0.0) - t
    a = loglb_ref[...]
    c = log1mlb_ref[...] + ls_pos
    lf_ref[...] = jnp.maximum(a, c) + jnp.log1p(jnp.exp(-jnp.abs(a - c)))
    k_ref[...] = jnp.exp(log1mlb_ref[...] + ls_neg)


def _proj_forget(a, w, col0, n, log_lb, log_1mlb, *, tm, tn):
    m, k = a.shape
    cb0 = col0 // tn
    return pl.pallas_call(
        _proj_forget_kernel,
        grid=(n // tn, m // tm),
        in_specs=[
            pl.BlockSpec((tm, k), lambda j, i: (i, 0)),
            pl.BlockSpec((k, tn), lambda j, i: (0, cb0 + j)),
            pl.BlockSpec((1, tn), lambda j, i: (0, j)),
            pl.BlockSpec((1, tn), lambda j, i: (0, j)),
        ],
        out_specs=[pl.BlockSpec((tm, tn), lambda j, i: (i, j)) for _ in range(2)],
        out_shape=[jax.ShapeDtypeStruct((m, n), F32) for _ in range(2)],
        scratch_shapes=[pltpu.VMEM((k, tn), BF16)],
        compiler_params=_cparams("parallel", "arbitrary"),
        name="proj_forget",
    )(a, w, log_lb, log_1mlb)


def _mm_res_kernel(a_ref, w_ref, x_ref, gt_ref, o_ref):
    y = jnp.dot(a_ref[...], w_ref[...].astype(BF16), preferred_element_type=F32)
    o_ref[...] = x_ref[...] + (1.0 + _rows(gt_ref[...], y.shape[0])) * y


def _mm_res(a, w, x, mod, layer, k_gate, *, tm, tn):
    m, k = a.shape
    n = w.shape[1]
    return pl.pallas_call(
        _mm_res_kernel,
        grid=(m // tm, n // tn),
        in_specs=[
            pl.BlockSpec((tm, k), lambda i, j: (i, 0)),
            pl.BlockSpec((k, tn), lambda i, j: (0, j)),
            pl.BlockSpec((tm, tn), lambda i, j: (i, j)),
            mod.spec(layer, k_gate, tn=tn, n_axis=True),
        ],
        out_specs=pl.BlockSpec((tm, tn), lambda i, j: (i, j)),
        out_shape=jax.ShapeDtypeStruct((m, n), F32),
        compiler_params=_cparams("parallel", "parallel"),
        name="mm_res",
    )(a, w, x, mod.arr)


def _sort_key(score):
    bits = pltpu.bitcast(score + 0.0, I32)
    return jnp.where(bits < 0, bits ^ jnp.int32(0x7FFFFFFF), bits)


COUNT_UNROLL = 4
ATTN_TQ = 256


def _attn_prompt_kernel(q_ref, qi_ref, wi_ref, k_ref, v_ref, kiab_ref, o_ref,
                        key_ref, bias_ref, wb_ref, j_ref, s_ref, m_ref, lp_ref, acc_ref, *, topk, idx_bits, kv_unroll):
    tq = q_ref.shape[0]
    tk = PAGE
    nsub = tq // tk
    dh = LANES
    rep = N_HEADS // N_KV_HEADS
    npair = N_IDX_HEADS // 2
    qb = pl.program_id(1)
    nkb = (qb + 1) * nsub
    row = lax.broadcasted_iota(I32, (tq, LANES), 0)
    lane = lax.broadcasted_iota(I32, (tq, LANES), 1)
    qpos = qb * tq + row
    imin = jnp.int32(IMIN)
    nt = (((1,), (1,)), ((), ()))

    wi = wi_ref[...]
    for h in range(N_IDX_HEADS):
        wcol = wi[:, IDX_DIM + h:IDX_DIM + h + 1] * (IDX_DIM ** -0.5)
        wb_ref[h] = jnp.broadcast_to(wcol, (tq, LANES))

    qi_all = qi_ref[...]

    def score_body(kb, c):
        lg = lax.dot_general(qi_all, kiab_ref[kb], nt, preferred_element_type=F32)
        accs = []
        for sb in range(nsub):
            qrows = slice(sb * tk, (sb + 1) * tk)
            acc = jnp.zeros((tk, LANES), F32)
            for c2 in range(npair):
                r = slice((sb * npair + c2) * tk, (sb * npair + c2 + 1) * tk)
                acc = acc + (wb_ref[2 * c2, qrows, :] * jnp.maximum(lg[r, 0:tk], 0.0)
                             + wb_ref[2 * c2 + 1, qrows, :] * jnp.maximum(lg[r, tk:2 * tk], 0.0))
            accs.append(acc)
        acc = accs[0] if nsub == 1 else jnp.concatenate(accs, axis=0)
        key_ref[kb] = jnp.where(kb * tk + lane <= qpos, _sort_key(acc), imin)
        return c

    lax.fori_loop(0, nkb, score_body, 0)
    for u in range(COUNT_UNROLL - 1):
        key_ref[nkb + u] = jnp.full((tq, LANES), IMIN, I32)

    sub_rows = [slice(sb * tk, (sb + 1) * tk) for sb in range(nsub)]
    lane_k = lax.broadcasted_iota(I32, (tk, LANES), 1)
    zeros_col = tuple(jnp.zeros((tk, 1), I32) for _ in range(nsub))

    def count(pred):
        n_it = (nkb + COUNT_UNROLL - 1) // COUNT_UNROLL
        parts = []
        for sb in range(nsub):
            def body(it, part, sb=sb):
                w = [jnp.where(pred(it * COUNT_UNROLL + u, key_ref[it * COUNT_UNROLL + u, sub_rows[sb], :], sb), 1.0, 0.0)
                     for u in range(COUNT_UNROLL)]
                while len(w) > 1:
                    w = [w[i] + w[i + 1] for i in range(0, len(w) - 1, 2)] + ([w[-1]] if len(w) % 2 else [])
                return part + w[0]

            parts.append(lax.fori_loop(0, n_it, body, jnp.zeros((tk, LANES), F32)))
        return [jnp.sum(part, axis=1, keepdims=True) for part in parts]

    def bit_body(i, tus):
        bit = jnp.left_shift(jnp.int32(1), 31 - i)
        cus = [tu | bit for tu in tus]
        cands = [cu ^ imin for cu in cus]
        cnts = count(lambda kb, k, sb: k >= cands[sb])
        return tuple(jnp.where(cnt >= topk, cu, tu) for cnt, cu, tu in zip(cnts, cus, tus))

    thrs = [tu ^ imin for tu in lax.fori_loop(0, 32, bit_body, zeros_col)]
    cnt_gt = count(lambda kb, k, sb: k > thrs[sb])
    cnt_ge = count(lambda kb, k, sb: k >= thrs[sb])
    needs = [topk - c for c in cnt_gt]
    ties = [jnp.logical_and(c > topk, thr > imin) for c, thr in zip(cnt_ge, thrs)]
    any_tie = functools.reduce(jnp.maximum, [jnp.max(jnp.where(t, 1.0, 0.0)) for t in ties])

    j_ref[...] = jnp.full(j_ref.shape, 2 ** idx_bits, I32)

    @pl.when(any_tie > 0.0)
    def _():
        def jbit_body(i, jvs):
            bit = jnp.left_shift(jnp.int32(1), idx_bits - 1 - i)
            cands = [jv | bit for jv in jvs]
            cs = count(lambda kb, k, sb: jnp.logical_and(k == thrs[sb], kb * tk + lane_k < cands[sb]))
            return tuple(jnp.where(c < need, cand, jv) for c, need, cand, jv in zip(cs, needs, cands, jvs))
        jvs = lax.fori_loop(0, idx_bits, jbit_body, zeros_col)
        for sb in range(nsub):
            j_ref[sub_rows[sb], :] = jnp.broadcast_to(jvs[sb], (tk, LANES))

    def bias_body(kb, c):
        for sb in range(nsub):
            k = key_ref[kb, sub_rows[sb], :]
            thr = thrs[sb]
            sel = jnp.logical_or(k > thr, jnp.logical_and(k == thr, kb * tk + lane_k <= j_ref[sub_rows[sb], :]))
            sel = jnp.logical_and(sel, k > imin)
            bias_ref[kb, sub_rows[sb], :] = jnp.where(sel, 0.0, NEG)
        return c

    lax.fori_loop(0, nkb, bias_body, 0)

    for u in range(kv_unroll - 1):
        bias_ref[nkb + u] = jnp.full((tq, LANES), NEG, F32)
    n_it = (nkb + kv_unroll - 1) // kv_unroll
    span = kv_unroll * tk
    c2 = (dh ** -0.5) * 1.4426950408889634
    for g in range(N_KV_HEADS):
        cols = slice(g * dh, (g + 1) * dh)
        qg = jnp.concatenate([q_ref[:, (g * rep + j) * dh:(g * rep + j + 1) * dh] for j in range(rep)], axis=0)
        m_ref[...] = jnp.full(m_ref.shape, NEG, F32)

        def pass1(it, c, qg=qg, cols=cols):
            keys = pl.ds(pl.multiple_of(it * span, span), span)
            s = lax.dot_general(qg, k_ref[keys, cols], nt, preferred_element_type=F32)
            b = jnp.concatenate([bias_ref[it * kv_unroll + u] for u in range(kv_unroll)], axis=1)
            for j in range(rep):
                rows = slice(j * tq, (j + 1) * tq)
                sj = s[rows] + b
                s_ref[it, rows, :] = sj
                t = sj[:, 0:LANES]
                for u in range(1, kv_unroll):
                    t = jnp.maximum(t, sj[:, u * LANES:(u + 1) * LANES])
                m_ref[rows, :] = jnp.maximum(m_ref[rows, :], t)
            return c

        lax.fori_loop(0, n_it, pass1, 0)
        mc = jnp.max(m_ref[...], axis=1, keepdims=True) * c2
        lp_ref[...] = jnp.zeros_like(lp_ref)
        acc_ref[...] = jnp.zeros_like(acc_ref)

        def pass2(it, c, cols=cols, mc=mc):
            keys = pl.ds(pl.multiple_of(it * span, span), span)
            ps = []
            for j in range(rep):
                rows = slice(j * tq, (j + 1) * tq)
                p = jnp.exp2(s_ref[it, rows, :] * c2 - mc[rows])
                t = p[:, 0:LANES]
                for u in range(1, kv_unroll):
                    t = t + p[:, u * LANES:(u + 1) * LANES]
                lp_ref[rows, :] += t
                ps.append(p.astype(BF16))
            acc_ref[...] += jnp.dot(jnp.concatenate(ps, axis=0), v_ref[keys, cols], preferred_element_type=F32)
            return c

        lax.fori_loop(0, n_it, pass2, 0)
        o = acc_ref[...] / jnp.sum(lp_ref[...], axis=1, keepdims=True)
        for j in range(rep):
            h = g * rep + j
            o_ref[:, h * dh:(h + 1) * dh] = o[j * tq:(j + 1) * tq].astype(o_ref.dtype)


def _attn_prompt(q, qi_stack, kiwi_f32, k_bf, v_bf, kiab, batch, seq):
    m = q.shape[0]
    tk = PAGE
    tq = min(ATTN_TQ, seq)
    nq, nk = seq // tq, seq // tk
    topk = min(INDEX_TOPK, seq // 4)
    kvw = k_bf.shape[1]
    npair = N_IDX_HEADS // 2
    rep = N_HEADS // N_KV_HEADS
    kv_unroll = next(u for u in (4, 2, 1) if nk % u == 0)
    kv_spec = pl.BlockSpec((None, seq, kvw), lambda b, i: (b, 0, 0), pipeline_mode=pl.Buffered(1))
    ki_spec = pl.BlockSpec((None, nk, 2 * tk, LANES), lambda b, i: (b, 0, 0, 0), pipeline_mode=pl.Buffered(1))
    return pl.pallas_call(
        functools.partial(_attn_prompt_kernel, topk=topk, idx_bits=max(1, (seq - 1).bit_length()), kv_unroll=kv_unroll),
        grid=(batch, nq),
        in_specs=[
            pl.BlockSpec((tq, q.shape[1]), lambda b, i: (b * nq + i, 0)),
            pl.BlockSpec((npair * tq, LANES), lambda b, i: (b * nq + i, 0)),
            pl.BlockSpec((tq, LANES), lambda b, i: (b * nq + i, 0)),
            kv_spec, kv_spec, ki_spec,
        ],
        out_specs=pl.BlockSpec((tq, q.shape[1]), lambda b, i: (b * nq + i, 0)),
        out_shape=jax.ShapeDtypeStruct((m, q.shape[1]), BF16),
        scratch_shapes=[
            pltpu.VMEM((nk + COUNT_UNROLL - 1, tq, LANES), I32),
            pltpu.VMEM((nk + kv_unroll - 1, tq, LANES), F32),
            pltpu.VMEM((N_IDX_HEADS, tq, LANES), F32),
            pltpu.VMEM((tq, LANES), I32),
            pltpu.VMEM((nk // kv_unroll, rep * tq, kv_unroll * tk), F32),
            pltpu.VMEM((rep * tq, LANES), F32),
            pltpu.VMEM((rep * tq, LANES), F32),
            pltpu.VMEM((rep * tq, LANES), F32),
        ],
        compiler_params=_cparams("parallel", "arbitrary"),
        name="attn_prompt",
    )(q, qi_stack, kiwi_f32, k_bf.reshape(batch, seq, kvw), v_bf.reshape(batch, seq, kvw),
      kiab.reshape(batch, nk, 2 * tk, LANES))


def _idx_sample_kernel(pt_ref, qi_ref, a_ref, kinew_ref, *refs, n_pages, n_tok):
    page_refs = refs[:n_pages]
    o_ref = refs[n_pages]
    a = a_ref[...]
    rows = a.shape[0]
    width = (n_pages + 1) * PAGE
    kit = jnp.concatenate([page_refs[p][...].astype(BF16) for p in range(n_pages)] + [kinew_ref[...]], axis=1)
    lg = jnp.dot(qi_ref[...], kit, preferred_element_type=F32)
    sc = jnp.dot(a, jnp.maximum(lg, 0.0), precision=lax.Precision.HIGHEST, preferred_element_type=F32)
    idx = lax.broadcasted_iota(I32, (rows, width), 1)
    row = lax.broadcasted_iota(I32, (rows, width), 0)
    adm = idx - n_pages * PAGE <= row % n_tok
    o_ref[...] = jnp.where(adm, _sort_key(sc), jnp.int32(IMIN))


def _idx_sample(page_table, qi_ht, amat, ki_new_t, pool_kidx_t, page0):
    bs, n_pages = page_table.shape
    n_tok = qi_ht.shape[1] // N_IDX_HEADS
    rows = amat.shape[1]
    width = (n_pages + 1) * LANES
    page_specs = [pl.BlockSpec((None, IDX_DIM, PAGE), lambda b, pt, p=p: (page0 + pt[b, p], 0, 0))
                  for p in range(n_pages)]
    grid_spec = pltpu.PrefetchScalarGridSpec(
        num_scalar_prefetch=1,
        grid=(bs,),
        in_specs=[
            pl.BlockSpec((None,) + qi_ht.shape[1:], lambda b, pt: (b, 0, 0)),
            pl.BlockSpec((None,) + amat.shape[1:], lambda b, pt: (b, 0, 0)),
            pl.BlockSpec((None,) + ki_new_t.shape[1:], lambda b, pt: (b, 0, 0)),
        ] + page_specs,
        out_specs=pl.BlockSpec((None, rows, width), lambda b, pt: (b, 0, 0)),
    )
    return pl.pallas_call(
        functools.partial(_idx_sample_kernel, n_pages=n_pages, n_tok=n_tok),
        grid_spec=grid_spec,
        out_shape=jax.ShapeDtypeStruct((bs, rows, width), I32),
        compiler_params=_cparams("arbitrary"),
        name="idx_sample",
    )(page_table, qi_ht, amat, ki_new_t, *([pool_kidx_t] * n_pages))


def _topk_bias_kernel(key_ref, bias_ref, *, topk, idx_bits):
    key = key_ref[...]
    rows, width = key.shape
    imin = jnp.int32(IMIN)
    idx = lax.broadcasted_iota(I32, (rows, width), 1)

    def count(pred):
        return jnp.sum(jnp.where(pred, 1.0, 0.0), axis=1, keepdims=True)

    def bit_body(i, tu):
        cu = tu | jnp.left_shift(jnp.int32(1), 31 - i)
        cnt = count(key >= (cu ^ imin))
        return jnp.where(cnt >= topk, cu, tu)

    thr = lax.fori_loop(0, 32, bit_body, jnp.zeros((rows, 1), I32)) ^ imin
    need = topk - count(key > thr)
    eq = key == thr

    def jbit_body(i, jv):
        cand = jv | jnp.left_shift(jnp.int32(1), idx_bits - 1 - i)
        c = count(jnp.logical_and(eq, idx < cand))
        return jnp.where(c < need, cand, jv)

    jv = lax.fori_loop(0, idx_bits, jbit_body, jnp.zeros((rows, 1), I32))
    sel = jnp.logical_or(key > thr, jnp.logical_and(eq, idx <= jv))
    sel = jnp.logical_and(sel, key > imin)
    bias_ref[...] = jnp.where(sel, 0.0, NEG)


def _topk_bias(keys2d, topk, tr):
    rows, width = keys2d.shape
    return pl.pallas_call(
        functools.partial(_topk_bias_kernel, topk=topk, idx_bits=max(1, (width - 1).bit_length())),
        grid=(rows // tr,),
        in_specs=[pl.BlockSpec((tr, width), lambda i: (i, 0))],
        out_specs=pl.BlockSpec((tr, width), lambda i: (i, 0)),
        out_shape=jax.ShapeDtypeStruct((rows, width), F32),
        compiler_params=_cparams("parallel"),
        name="topk_bias",
    )(keys2d)


def _attn_sample_kernel(pt_ref, q_ref, bias_ref, knew_ref, vnew_ref, *refs, n_pages):
    k_refs = refs[:n_pages]
    v_refs = refs[n_pages:2 * n_pages]
    o_ref = refs[2 * n_pages]
    rows, dh = q_ref.shape
    rpg = rows // N_KV_HEADS
    bias8 = bias_ref[...]
    bias = jnp.concatenate([bias8] * (rpg // bias8.shape[0]), axis=0)
    pad = PAGE - knew_ref.shape[0]
    zpad = jnp.zeros((pad, dh), BF16)

    for g in range(N_KV_HEADS):
        qg = q_ref[g * rpg:(g + 1) * rpg, :]
        cols = slice(g * dh, (g + 1) * dh)
        s_blocks = []
        for p in range(n_pages + 1):
            if p < n_pages:
                kp = k_refs[p][pl.ds(g, PAGE, stride=N_KV_HEADS), :].astype(BF16)
            else:
                kp = jnp.concatenate([knew_ref[:, cols], zpad], axis=0)
            s_blocks.append(lax.dot_general(qg, kp, (((1,), (1,)), ((), ())), preferred_element_type=F32))
        s = jnp.concatenate(s_blocks, axis=1) * (dh ** -0.5) + bias
        m = jnp.max(s, axis=1, keepdims=True)
        pexp = jnp.exp(s - m)
        l = jnp.sum(pexp, axis=1, keepdims=True)
        pb = pexp.astype(BF16)
        o = jnp.zeros((rpg, dh), F32)
        for p in range(n_pages + 1):
            if p < n_pages:
                vp = v_refs[p][pl.ds(g, PAGE, stride=N_KV_HEADS), :].astype(BF16)
            else:
                vp = jnp.concatenate([vnew_ref[:, cols], zpad], axis=0)
            o = o + jnp.dot(pb[:, p * LANES:(p + 1) * LANES], vp, preferred_element_type=F32)
        o_ref[g * rpg:(g + 1) * rpg, :] = (o / l).astype(o_ref.dtype)


def _attn_sample(page_table, q64, bias, k_new, v_new, pool_k, pool_v, page0):
    bs, n_pages = page_table.shape
    pspecs = [pl.BlockSpec((None,) + pool_k.shape[1:], lambda b, pt, p=p: (page0 + pt[b, p], 0, 0))
              for p in range(n_pages)]
    npg = len(pspecs)
    grid_spec = pltpu.PrefetchScalarGridSpec(
        num_scalar_prefetch=1,
        grid=(bs,),
        in_specs=[
            pl.BlockSpec((None,) + q64.shape[1:], lambda b, pt: (b, 0, 0)),
            pl.BlockSpec((None,) + bias.shape[1:], lambda b, pt: (b, 0, 0)),
            pl.BlockSpec((None,) + k_new.shape[1:], lambda b, pt: (b, 0, 0)),
            pl.BlockSpec((None,) + v_new.shape[1:], lambda b, pt: (b, 0, 0)),
        ] + pspecs + pspecs,
        out_specs=pl.BlockSpec((None,) + q64.shape[1:], lambda b, pt: (b, 0, 0)),
    )
    return pl.pallas_call(
        functools.partial(_attn_sample_kernel, n_pages=n_pages),
        grid_spec=grid_spec,
        out_shape=jax.ShapeDtypeStruct(q64.shape, BF16),
        compiler_params=_cparams("arbitrary"),
        name="attn_sample",
    )(page_table, q64, bias, k_new, v_new, *([pool_k] * npg), *([pool_v] * npg))


def _gla_out(o, sg, gain):
    return (o * lax.rsqrt(jnp.mean(o * o, axis=-1, keepdims=True) + EPS) * gain * sg)


GLA_CHUNK = 16


def _gla_prompt_kernel(q_ref, k_ref, v_ref, g_ref, sg_ref, gain_ref, y_ref, s_ref, st_ref, *, blk):
    ci = pl.program_id(2)
    c = GLA_CHUNK

    @pl.when(ci == 0)
    def _():
        st_ref[...] = jnp.zeros_like(st_ref)

    tri = (lax.broadcasted_iota(I32, (blk, blk), 0) >= lax.broadcasted_iota(I32, (blk, blk), 1)).astype(F32)
    ones = jnp.ones((LANES, LANES), BF16)
    nsub = c // SUBLANES
    nch = blk // c
    gain = gain_ref[...]
    r8 = lax.broadcasted_iota(I32, (SUBLANES, LANES), 0)
    mask_bias = [jnp.where(r8 >= j, 0.0, NEG) for j in range(SUBLANES)]
    log2e = 1.4426950408889634

    def body(bi, carry):
        r0 = pl.multiple_of(bi * blk, blk)
        bb = jnp.dot(tri, g_ref[pl.ds(r0, blk), :] * log2e, precision=lax.Precision.HIGHEST,
                     preferred_element_type=F32)

        qs, vs, bs, xs = [], [], [], []
        for ch in range(nch):
            lo = ch * c
            rows = pl.ds(r0 + lo, c)
            q, k, v = q_ref[rows, :], k_ref[rows, :], v_ref[rows, :]
            b = bb[lo:lo + c] - bb[lo - 1:lo] if ch > 0 else bb[lo:lo + c]
            qs.append(q)
            vs.append(v)
            bs.append((b, k))
            for s in range(c):
                i0 = s // SUBLANES
                qk = q[i0 * SUBLANES:] * k[s:s + 1]
                diff = b[i0 * SUBLANES:] - b[s:s + 1]
                for i in range(i0, nsub):
                    d = diff[(i - i0) * SUBLANES:(i - i0 + 1) * SUBLANES]
                    if i == i0:
                        d = d + mask_bias[s % SUBLANES]
                    xs.append(qk[(i - i0) * SUBLANES:(i - i0 + 1) * SUBLANES] * jnp.exp2(d))
        a_all = jnp.dot(jnp.concatenate(xs, axis=0).astype(BF16), ones, preferred_element_type=F32)
        upds, decs, qes = [], [], []
        for ch in range(nch):
            b, k = bs[ch]
            b_end = b[c - 1:c]
            kd = (k * jnp.exp2(b_end - b)).astype(BF16)
            upds.append(lax.dot_general(vs[ch].astype(BF16), kd, (((0,), (0,)), ((), ())), preferred_element_type=F32))
            decs.append(jnp.exp2(b_end))
            qes.append((qs[ch] * jnp.exp2(b)).astype(BF16))

        st = st_ref[...]
        off = 0
        for ch in range(nch):
            o = lax.dot_general(qes[ch], st.astype(BF16), (((1,), (1,)), ((), ())), preferred_element_type=F32)
            st = st * decs[ch] + upds[ch]
            o_parts = [o[i * SUBLANES:(i + 1) * SUBLANES] for i in range(nsub)]
            v = vs[ch]
            for s in range(c):
                for i in range(s // SUBLANES, nsub):
                    o_parts[i] = o_parts[i] + a_all[off:off + SUBLANES] * v[s:s + 1]
                    off += SUBLANES
            rows = pl.ds(r0 + ch * c, c)
            y_ref[rows, :] = _gla_out(jnp.concatenate(o_parts, axis=0), sg_ref[rows, :], gain).astype(y_ref.dtype)
        st_ref[...] = st
        return carry

    lax.fori_loop(0, q_ref.shape[0] // blk, body, 0)

    @pl.when(ci == pl.num_programs(2) - 1)
    def _():
        s_ref[...] = st_ref[...].T


def _gla_prompt(qs, kk, vv, lf, sg, gain, batch, seq, tb):
    m, d = qs.shape
    nh = d // LANES
    nc = seq // tb
    tok = pl.BlockSpec((tb, LANES), lambda b, h, c: (b * nc + c, h))
    return pl.pallas_call(
        functools.partial(_gla_prompt_kernel, blk=min(LANES, tb)),
        grid=(batch, nh, nc),
        in_specs=[tok, tok, tok, tok, tok, pl.BlockSpec((1, LANES), lambda b, h, c: (0, h))],
        out_specs=[tok, pl.BlockSpec((None, None, LANES, LANES), lambda b, h, c: (b, h, 0, 0))],
        out_shape=[jax.ShapeDtypeStruct((m, d), BF16), jax.ShapeDtypeStruct((batch, nh, LANES, LANES), F32)],
        scratch_shapes=[pltpu.VMEM((LANES, LANES), F32)],
        compiler_params=_cparams("parallel", "parallel", "arbitrary"),
        name="gla_prompt",
    )(qs, kk, vv, lf, sg, gain)


def _gla_sample_kernel(q_ref, k_ref, v_ref, g_ref, sg_ref, gain_ref, s0_ref, y_ref, s_ref, *, n_tok):
    nh = s0_ref.shape[0]
    tp = q_ref.shape[0]
    log2e = 1.4426950408889634
    tri = (lax.broadcasted_iota(I32, (tp, tp), 0) >= lax.broadcasted_iota(I32, (tp, tp), 1)).astype(F32)
    b_all = jnp.dot(tri, g_ref[...] * log2e, precision=lax.Precision.HIGHEST, preferred_element_type=F32)
    b_end_all = b_all[tp - 1:tp]
    r8 = lax.broadcasted_iota(I32, (tp, LANES), 0)
    mask_bias = [jnp.where(r8 >= j, 0.0, NEG) for j in range(n_tok)]
    row0 = lax.broadcasted_iota(I32, b_all.shape, 0) == 0
    dec_cols = lax.dot_general(jnp.where(row0, jnp.exp2(b_end_all), 0.0), jnp.ones((tp, LANES), F32),
                               (((0,), (0,)), ((), ())), precision=lax.Precision.HIGHEST,
                               preferred_element_type=F32)
    xs = []
    for h in range(nh):
        c = slice(h * LANES, (h + 1) * LANES)
        q, k, b = q_ref[:, c], k_ref[:, c], b_all[:, c]
        for s in range(n_tok):
            xs.append(q * k[s:s + 1] * jnp.exp2(b - b[s:s + 1] + mask_bias[s]))
    a_all = jnp.dot(jnp.concatenate(xs, axis=0).astype(BF16), jnp.ones((LANES, LANES), BF16),
                    preferred_element_type=F32)
    for h in range(nh):
        c = slice(h * LANES, (h + 1) * LANES)
        q, k, v, b = q_ref[:, c], k_ref[:, c], v_ref[:, c], b_all[:, c]
        st0 = s0_ref[h]
        o = jnp.dot((q * jnp.exp2(b)).astype(BF16), st0.astype(BF16), preferred_element_type=F32)
        for s in range(n_tok):
            o = o + a_all[(h * n_tok + s) * tp:(h * n_tok + s + 1) * tp] * v[s:s + 1]
        kd = (k * jnp.exp2(b_end_all[:, c] - b)).astype(BF16)
        upd = lax.dot_general(kd, v.astype(BF16), (((0,), (0,)), ((), ())), preferred_element_type=F32)
        s_ref[h] = dec_cols[h * LANES:(h + 1) * LANES] * st0 + upd
        y_ref[:, c] = _gla_out(o, sg_ref[:, c], gain_ref[:, c]).astype(y_ref.dtype)


def _gla_sample(qs, kk, vv, lf, sg, gain, state_pool, seq0, n_tok):
    bs, tp, d = qs.shape
    nh = state_pool.shape[1]
    tok = pl.BlockSpec((None, tp, d), lambda b: (b, 0, 0))
    st_in = pl.BlockSpec((None, nh, LANES, LANES), lambda b: (seq0 + b, 0, 0, 0))
    st = pl.BlockSpec((None, nh, LANES, LANES), lambda b: (b, 0, 0, 0))
    return pl.pallas_call(
        functools.partial(_gla_sample_kernel, n_tok=n_tok),
        grid=(bs,),
        in_specs=[tok, tok, tok, tok, tok, pl.BlockSpec((1, d), lambda b: (0, 0)), st_in],
        out_specs=[tok, st],
        out_shape=[jax.ShapeDtypeStruct((bs, tp, d), BF16), jax.ShapeDtypeStruct((bs, nh, LANES, LANES), F32)],
        compiler_params=_cparams("parallel"),
        name="gla_sample",
    )(qs, kk, vv, lf, sg, gain, state_pool)


def _rope_tables(pos, d):
    inv = ROPE_THETA ** (-jnp.arange(0, d, 2, dtype=F32) / d)
    ang = pos.astype(F32)[:, None] * inv[None, :]
    cos, sin = jnp.cos(ang), jnp.sin(ang)
    reps = LANES // d
    a = jnp.tile(jnp.concatenate([cos, cos], axis=1), (1, reps))
    b = jnp.tile(jnp.concatenate([-sin, sin], axis=1), (1, reps))
    return a, b


def _kiwi_tables(pos):
    a64, b64 = _rope_tables(pos, IDX_DIM)
    p = pos.shape[0]
    lane = jnp.arange(IDX_DIM)
    wi_scale = jnp.where(lane < N_IDX_HEADS, N_IDX_HEADS ** -0.5, 0.0).astype(F32)
    a = jnp.concatenate([a64[:, :IDX_DIM], jnp.broadcast_to(wi_scale, (p, IDX_DIM))], axis=1)
    b = jnp.concatenate([b64[:, :IDX_DIM], jnp.zeros((p, IDX_DIM), F32)], axis=1)
    return a, b


def _attn_projections(h, w_in, w_kiwi, q_gain, k_gain, pos, tm, stack_qi):
    dh = LANES
    qw = N_HEADS * dh
    kvw = N_KV_HEADS * dh
    qiw = N_IDX_HEADS * IDX_DIM
    a128, b128 = _rope_tables(pos, dh)
    a64, b64 = _rope_tables(pos, IDX_DIM)
    akw, bkw = _kiwi_tables(pos)
    tn = 512
    (q,) = _proj_rope(h, w_in, 0, qw, q_gain, a128, b128, [BF16], norm=True, half=dh // 2, tm=tm, tn=tn)
    k32, k16 = _proj_rope(h, w_in, qw, kvw, k_gain, a128, b128, [F32, BF16], norm=True, half=dh // 2, tm=tm, tn=tn,
                          layout="kv_rows" if stack_qi else "cols")
    tm_plain = 2 * tm if h.shape[0] % (2 * tm) == 0 else tm
    v32, v16 = _proj_plain(h, w_in, qw + kvw, kvw, [F32, BF16], act=None, tm=tm_plain, tn=tn, kv_rows=stack_qi)
    if stack_qi:
        (qi,) = _proj_rope(h, w_in, qw + 2 * kvw, qiw, q_gain, a64, b64, [BF16], norm=False, half=IDX_DIM // 2,
                           tm=tm, tn=qiw, layout="stack")
    else:
        (qi,) = _proj_rope(h, w_in, qw + 2 * kvw, qiw, q_gain, a64, b64, [BF16], norm=False, half=IDX_DIM // 2,
                           tm=tm, tn=tn)
    if stack_qi:
        kw32, kiab = _proj_rope(h, w_kiwi, 0, LANES, q_gain, akw, bkw, [F32, BF16], norm=False,
                                half=IDX_DIM // 2, tm=tm, tn=LANES, layout="ki_split")
    else:
        (kw32,), kiab = _proj_rope(h, w_kiwi, 0, LANES, q_gain, akw, bkw, [F32], norm=False,
                                   half=IDX_DIM // 2, tm=tm, tn=LANES), None
    return q, qi, k32, k16, v32, v16, kw32, kiab


def _hgrn_projections(h, w_in, log_lb, log_1mlb, tm):
    d = h.shape[1]
    tn = 512
    (qs,) = _proj_plain(h, w_in, 0, d, [F32], act="silu", tm=tm, tn=tn)
    lf, kk = _proj_forget(h, w_in, d, d, log_lb, log_1mlb, tm=tm, tn=tn)
    (vv,) = _proj_plain(h, w_in, 2 * d, d, [F32], act=None, tm=tm, tn=tn)
    (sg,) = _proj_plain(h, w_in, 3 * d, d, [F32], act="silu", tm=tm, tn=tn)
    return qs, kk, vv, lf, sg


def _stack(arrs):
    return arrs[0][None] if len(arrs) == 1 else jnp.stack(arrs)


def kernel(x_prompt, x_sample, c_prompt, c_sample, cache_k, cache_v, cache_kidx, state_hgrn, page_table, norm_gain, w_ada, b_ada, w_ff_up, w_ff_down, w_attn_in, w_attn_out, q_norm, k_norm, w_rec_in, w_rec_out, rec_out_norm, lb_logits):
    batch, seq, d = x_prompt.shape
    bs, n_tok, _ = x_sample.shape
    depth = w_ada.shape[0]
    n_pages = page_table.shape[1]
    past = n_pages * PAGE
    dh = LANES
    kvw = N_KV_HEADS * dh

    c_all = jnp.concatenate([c_prompt, jnp.zeros((SUBLANES - batch, d), F32), c_sample], axis=0)
    mod = _ada_mod(c_all, w_ada, b_ada)
    tm_p = min(1024, seq)
    mod_p = _Mod(mod[:, :batch].reshape(depth, batch, 9, 1, d), False, seq // tm_p)
    mod_s = _Mod(mod[:, SUBLANES:].reshape(depth, bs, 9, d).transpose(0, 2, 1, 3), True, None)

    gains = norm_gain.reshape(depth, 3, 1, d)
    p = jnp.exp(lb_logits - jnp.max(lb_logits, axis=0, keepdims=True))
    p = p / jnp.sum(p, axis=0, keepdims=True)
    lb_all = jnp.cumsum(p, axis=0) - p[0]

    xp = x_prompt.reshape(batch * seq, d)
    xs = x_sample.transpose(1, 0, 2).reshape(n_tok * bs, d)
    pos_p = jnp.arange(seq)
    pos_s = jnp.repeat(past + jnp.arange(n_tok), bs)
    ms = n_tok * bs

    outs_p, outs_s = {}, {}
    for i in range(depth):
        j = i // 2
        xp = _ffn(xp, mod_p, i, 0, gains, w_ff_up, w_ff_down, 0, tm_p, 512)
        xs = _ffn(xs, mod_s, i, 0, gains, w_ff_up, w_ff_down, 0, ms, 512)
        hp = _prenorm(xp, mod_p, i, 1, gains, tm_p)
        hs = _prenorm(xs, mod_s, i, 1, gains, bs)
        if i % 2 == 0:
            w_in = w_attn_in[j]
            col = N_HEADS * dh + 2 * kvw + N_IDX_HEADS * IDX_DIM
            w_kiwi = jnp.pad(w_in[:, col:], ((0, 0), (0, LANES - (w_in.shape[1] - col))))
            qg = q_norm[j].reshape(1, dh)
            kg = k_norm[j].reshape(1, dh)
            q, qi, k32, k16, v32, v16, kw32, kiab = _attn_projections(hp, w_in, w_kiwi, qg, kg, pos_p, tm_p // 2, True)
            o = _attn_prompt(q, qi, kw32, k16, v16, kiab, batch, seq)
            xp = _mm_res(o, w_attn_out[j], xp, mod_p, i, 5, tm=tm_p, tn=512)
            outs_p[i] = (k32, v32, kw32[:, :IDX_DIM])
            q, qi, k32, k16, v32, v16, kw32, _ = _attn_projections(hs, w_in, w_kiwi, qg, kg, pos_s, bs, False)

            def bmajor(a):
                return a.reshape(n_tok, bs, a.shape[1]).transpose(1, 0, 2)

            qi_b = bmajor(qi).reshape(bs, n_tok, N_IDX_HEADS, IDX_DIM)
            qi_ht = qi_b.transpose(0, 2, 1, 3).reshape(bs, N_IDX_HEADS * n_tok, IDX_DIM)
            wi_b = bmajor(kw32)[:, :, IDX_DIM:IDX_DIM + N_IDX_HEADS] * (IDX_DIM ** -0.5)
            amat = (wi_b[:, :, :, None] * jnp.eye(n_tok, dtype=F32)[None, :, None, :]).reshape(bs, n_tok, -1)
            amat = jnp.concatenate([amat, amat], axis=1)
            rows_pad = 2 * n_tok

            def pad_rows(a):
                return jnp.pad(a, ((0, 0), (0, rows_pad - a.shape[1]), (0, 0)))

            ki_new_t = jnp.pad(bmajor(kw32)[:, :, :IDX_DIM].astype(BF16).transpose(0, 2, 1),
                               ((0, 0), (0, 0), (0, PAGE - n_tok)))
            n_pool = cache_k.shape[1]
            pool_kidx_t = cache_kidx.reshape((-1,) + cache_kidx.shape[2:]).transpose(0, 2, 1)
            keys = _idx_sample(page_table, qi_ht, amat, ki_new_t, pool_kidx_t, j * n_pool)
            width = keys.shape[2]
            topk = min(INDEX_TOPK, (past + n_tok) // 4)
            bias = _topk_bias(keys.reshape(bs * rows_pad, width), topk, min(256, bs * rows_pad)).reshape(bs, rows_pad, width)
            q64 = bmajor(q).reshape(bs, n_tok, N_HEADS, dh).transpose(0, 2, 1, 3).reshape(bs, N_HEADS * n_tok, dh)
            o64 = _attn_sample(page_table, q64, bias, pad_rows(bmajor(k16)), pad_rows(bmajor(v16)),
                               cache_k.reshape(-1, PAGE * N_KV_HEADS, dh), cache_v.reshape(-1, PAGE * N_KV_HEADS, dh),
                               j * n_pool)
            o = o64.reshape(bs, N_HEADS, n_tok, dh).transpose(2, 0, 1, 3).reshape(ms, N_HEADS * dh)
            xs = _mm_res(o, w_attn_out[j], xs, mod_s, i, 5, tm=ms, tn=512)
            outs_s[i] = (bmajor(k32), bmajor(v32), bmajor(kw32)[:, :, :IDX_DIM])
        else:
            lb = lb_all[i].reshape(1, d)
            log_lb, log_1mlb = jnp.log(lb), jnp.log1p(-lb)
            gain = rec_out_norm[j].reshape(1, d)
            qs, kk, vv, lf, sg = _hgrn_projections(hp, w_rec_in[j], log_lb, log_1mlb, tm_p)
            y, st = _gla_prompt(qs, kk, vv, lf, sg, gain, batch, seq, min(512, seq))
            xp = _mm_res(y, w_rec_out[j], xp, mod_p, i, 5, tm=tm_p, tn=512)
            outs_p[i] = (st,)
            qs, kk, vv, lf, sg = _hgrn_projections(hs, w_rec_in[j], log_lb, log_1mlb, bs)

            def bpad(a):
                a = a.reshape(n_tok, bs, d).transpose(1, 0, 2)
                return jnp.pad(a, ((0, 0), (0, SUBLANES - n_tok), (0, 0)))

            y, st = _gla_sample(bpad(qs), bpad(kk), bpad(vv), bpad(lf), bpad(sg), gain,
                                state_hgrn.reshape((-1,) + state_hgrn.shape[2:]), j * bs, n_tok)
            y = y[:, :n_tok].transpose(1, 0, 2).reshape(ms, d)
            xs = _mm_res(y, w_rec_out[j], xs, mod_s, i, 5, tm=ms, tn=512)
            outs_s[i] = (st,)
        xp = _ffn(xp, mod_p, i, 2, gains, w_ff_up, w_ff_down, 1, tm_p, 512)
        xs = _ffn(xs, mod_s, i, 2, gains, w_ff_up, w_ff_down, 1, ms, 512)

    attn_layers = [i for i in range(depth) if i % 2 == 0]
    rec_layers = [i for i in range(depth) if i % 2 == 1]
    na, pages_p = len(attn_layers), seq // PAGE
    y_prompt = xp.reshape(batch, seq, d)
    y_sample = xs.reshape(n_tok, bs, d).transpose(1, 0, 2)
    k_prompt = _stack([outs_p[i][0] for i in attn_layers]).reshape(na, batch, pages_p, PAGE, N_KV_HEADS, dh)
    v_prompt = _stack([outs_p[i][1] for i in attn_layers]).reshape(na, batch, pages_p, PAGE, N_KV_HEADS, dh)
    kidx_prompt = _stack([outs_p[i][2] for i in attn_layers]).reshape(na, batch, pages_p, PAGE, IDX_DIM)
    state_prompt = _stack([outs_p[i][0] for i in rec_layers]).astype(state_hgrn.dtype)
    k_sample = _stack([outs_s[i][0] for i in attn_layers]).reshape(na, bs, n_tok, N_KV_HEADS, dh)
    v_sample = _stack([outs_s[i][1] for i in attn_layers]).reshape(na, bs, n_tok, N_KV_HEADS, dh)
    kidx_sample = _stack([outs_s[i][2] for i in attn_layers])
    state_sample = _stack([outs_s[i][0] for i in rec_layers]).astype(state_hgrn.dtype)
    return (y_prompt, y_sample, k_prompt, v_prompt, kidx_prompt, state_prompt, k_sample, v_sample, kidx_sample, state_sample)
```

```python
import functools

import jax
import jax.numpy as jnp
from jax import lax
from jax.experimental import pallas as pl
from jax.experimental.pallas import tpu as pltpu

F32 = jnp.float32
BF16 = jnp.bfloat16
I32 = jnp.int32

LANES = 128
SUBLANES = 8
VMEM_LIMIT = 60 * 1024 * 1024

N_HEADS = 16
N_KV_HEADS = 4
N_IDX_HEADS = 16
IDX_DIM = 64
INDEX_TOPK = 256
ROPE_THETA = 10000.0
REC_CHUNK = 64
EPS = 1e-6
PAGE = 128
NEG = -1e30
IMIN = -2 ** 31


def _cparams(*sem):
    return pltpu.CompilerParams(dimension_semantics=sem, vmem_limit_bytes=VMEM_LIMIT)


def _rows(mod, tm):
    rm = mod.shape[0]
    if rm == 1 or rm == tm:
        return mod
    return jnp.concatenate([mod] * (tm // rm), axis=0)


def _ada_norm(x, gain, shift, scale):
    tm = x.shape[0]
    y = x * lax.rsqrt(jnp.mean(x * x, axis=-1, keepdims=True) + EPS) * gain
    return y * (1.0 + _rows(scale, tm)) + _rows(shift, tm)


def _silu(x):
    return x * jax.nn.sigmoid(x)


def _ada_mod_kernel(c_ref, w_ref, b_ref, o_ref):
    cs = _silu(c_ref[...]).astype(BF16)
    o_ref[...] = jnp.dot(cs, w_ref[...].astype(BF16), preferred_element_type=F32) + b_ref[...]


def _ada_mod(c_all, w_ada, b_ada):
    depth, d, n = w_ada.shape
    m = c_all.shape[0]
    tn = 2048
    return pl.pallas_call(
        _ada_mod_kernel,
        grid=(depth, n // tn),
        in_specs=[
            pl.BlockSpec((m, d), lambda i, j: (0, 0)),
            pl.BlockSpec((None, d, tn), lambda i, j: (i, 0, j)),
            pl.BlockSpec((None, 1, tn), lambda i, j: (i, 0, j)),
        ],
        out_specs=pl.BlockSpec((None, m, tn), lambda i, j: (i, 0, j)),
        out_shape=jax.ShapeDtypeStruct((depth, m, n), F32),
        compiler_params=_cparams("parallel", "parallel"),
        name="ada_mod",
    )(c_all, w_ada, b_ada.reshape(depth, 1, n))


class _Mod:
    def __init__(self, arr, per_row, tiles_per_seq):
        self.arr = arr
        self.per_row = per_row
        self.tps = tiles_per_seq

    def spec(self, layer, k, tn=None, n_axis=False):
        d = self.arr.shape[-1]
        tn = d if tn is None else tn
        if self.per_row:
            rows = self.arr.shape[2]
            if n_axis:
                return pl.BlockSpec((None, None, rows, tn), lambda n, m: (layer, k, 0, n))
            return pl.BlockSpec((None, None, rows, tn), lambda m, *_: (layer, k, 0, 0))
        tps = self.tps
        if n_axis:
            return pl.BlockSpec((None, None, None, 1, tn), lambda n, m: (layer, m // tps, k, 0, n))
        return pl.BlockSpec((None, None, None, 1, tn), lambda m, *_: (layer, m // tps, k, 0, 0))


def _ffn_kernel(x_ref, sh_ref, sc_ref, gt_ref, gain_ref, wa_ref, wb_ref, wd_ref, o_ref, h_ref):
    j = pl.program_id(1)

    @pl.when(j == 0)
    def _():
        h_ref[...] = _ada_norm(x_ref[...], gain_ref[...], sh_ref[...], sc_ref[...]).astype(BF16)
        o_ref[...] = jnp.zeros_like(o_ref)

    h = h_ref[...]
    tf = wa_ref.shape[1]
    nsplit = 2 if tf % (2 * LANES) == 0 else 1
    gs = []
    for c in range(nsplit):
        cols = slice(c * tf // nsplit, (c + 1) * tf // nsplit)
        a = jnp.dot(h, wa_ref[:, cols].astype(BF16), preferred_element_type=F32)
        b = jnp.dot(h, wb_ref[:, cols].astype(BF16), preferred_element_type=F32)
        gs.append((_silu(a) * b).astype(BF16))
    g = gs[0] if nsplit == 1 else jnp.concatenate(gs, axis=1)
    o_ref[...] += jnp.dot(g, wd_ref[...].astype(BF16), preferred_element_type=F32)

    @pl.when(j == pl.num_programs(1) - 1)
    def _():
        tm = o_ref.shape[0]
        o_ref[...] = x_ref[...] + 0.5 * (1.0 + _rows(gt_ref[...], tm)) * o_ref[...]


def _ffn(x, mod, layer, sub, gain, w_up, w_down, ffn_idx, tm, tf):
    m, d = x.shape
    f = w_down.shape[2]
    nf = f // tf
    return pl.pallas_call(
        _ffn_kernel,
        grid=(m // tm, nf),
        in_specs=[
            pl.BlockSpec((tm, d), lambda i, j: (i, 0), pipeline_mode=pl.Buffered(1)),
            mod.spec(layer, 3 * sub + 0),
            mod.spec(layer, 3 * sub + 1),
            mod.spec(layer, 3 * sub + 2),
            pl.BlockSpec((None, None, 1, d), lambda i, j: (layer, sub, 0, 0)),
            pl.BlockSpec((None, None, d, tf), lambda i, j: (layer, ffn_idx, 0, j)),
            pl.BlockSpec((None, None, d, tf), lambda i, j: (layer, ffn_idx, 0, j + nf)),
            pl.BlockSpec((None, None, tf, d), lambda i, j: (layer, ffn_idx, j, 0)),
        ],
        out_specs=pl.BlockSpec((tm, d), lambda i, j: (i, 0), pipeline_mode=pl.Buffered(1)),
        out_shape=jax.ShapeDtypeStruct((m, d), F32),
        scratch_shapes=[pltpu.VMEM((tm, d), BF16)],
        compiler_params=_cparams("parallel", "arbitrary"),
        name="ffn",
    )(x, mod.arr, mod.arr, mod.arr, gain, w_up, w_up, w_down)


def _prenorm_kernel(x_ref, sh_ref, sc_ref, gain_ref, h_ref):
    h_ref[...] = _ada_norm(x_ref[...], gain_ref[...], sh_ref[...], sc_ref[...]).astype(BF16)


def _prenorm(x, mod, layer, sub, gain, tm):
    m, d = x.shape
    return pl.pallas_call(
        _prenorm_kernel,
        grid=(m // tm,),
        in_specs=[
            pl.BlockSpec((tm, d), lambda i: (i, 0)),
            mod.spec(layer, 3 * sub + 0),
            mod.spec(layer, 3 * sub + 1),
            pl.BlockSpec((None, None, 1, d), lambda i: (layer, sub, 0, 0)),
        ],
        out_specs=pl.BlockSpec((tm, d), lambda i: (i, 0)),
        out_shape=jax.ShapeDtypeStruct((m, d), BF16),
        compiler_params=_cparams("parallel"),
        name="prenorm",
    )(x, mod.arr, mod.arr, gain)


def _rot_half(y, half):
    if 2 * half == LANES:
        return pltpu.roll(y, half, 1)
    lane = lax.broadcasted_iota(I32, y.shape, 1)
    return jnp.where(lane % (2 * half) < half, pltpu.roll(y, LANES - half, 1), pltpu.roll(y, half, 1))


def _cached_bf16(w_ref, wbf_ref):
    @pl.when(pl.program_id(1) == 0)
    def _():
        wbf_ref[...] = w_ref[...].astype(BF16)
    return wbf_ref[...]


def _proj_rope_kernel(a_ref, w_ref, g_ref, ca_ref, cb_ref, *refs, norm, half, layout):
    o_refs, wbf_ref = refs[:-1], refs[-1]
    wbf = _cached_bf16(w_ref, wbf_ref)
    a = a_ref[...]
    ca = ca_ref[...]
    cb = cb_ref[...]
    tm = a.shape[0]
    ncol = wbf.shape[1] // LANES
    pair = 2 if ncol % 2 == 0 else 1
    ys = [jnp.dot(a, wbf[:, p * pair * LANES:(p + 1) * pair * LANES], preferred_element_type=F32)
          for p in range(ncol // pair)]
    for c in range(ncol):
        yc = ys[c // pair][:, (c % pair) * LANES:(c % pair + 1) * LANES]
        if norm:
            yc = yc * lax.rsqrt(jnp.mean(yc * yc, axis=-1, keepdims=True) + EPS) * g_ref[...]
        r = yc * ca + _rot_half(yc, half) * cb
        if layout == "ki_split":
            lane = lax.broadcasted_iota(I32, r.shape, 1)
            lo = jnp.where(lane < IDX_DIM, r, 0.0)
            hi = pltpu.roll(lo, IDX_DIM, 1)
            o_refs[0][...] = r
            for qb in range(tm // PAGE):
                rows = slice(qb * PAGE, (qb + 1) * PAGE)
                o_refs[1][2 * qb * PAGE:(2 * qb + 1) * PAGE, :] = lo[rows].astype(BF16)
                o_refs[1][(2 * qb + 1) * PAGE:(2 * qb + 2) * PAGE, :] = hi[rows].astype(BF16)
        elif layout == "stack":
            for qb in range(tm // PAGE):
                o_refs[0][(qb * ncol + c) * PAGE:(qb * ncol + c + 1) * PAGE, :] = (
                    r[qb * PAGE:(qb + 1) * PAGE].astype(o_refs[0].dtype))
        elif layout == "kv_rows":
            _store_kv_rows(o_refs, r, c, ncol)
        else:
            for o_ref in o_refs:
                o_ref[:, c * LANES:(c + 1) * LANES] = r.astype(o_ref.dtype)


def _store_kv_rows(o_refs, r, c, ncol):
    o_refs[0][pl.ds(c, r.shape[0], stride=ncol), :] = r
    o_refs[1][:, c * LANES:(c + 1) * LANES] = r.astype(BF16)


def _kv_rows_outs(m, n, tm):
    ncol = n // LANES
    specs = [pl.BlockSpec((tm * ncol, LANES), lambda j, i: (i, 0)), pl.BlockSpec((tm, n), lambda j, i: (i, 0))]
    shapes = [jax.ShapeDtypeStruct((m * ncol, LANES), F32), jax.ShapeDtypeStruct((m, n), BF16)]
    return specs, shapes


def _proj_rope(a, w, col0, n, gain, tab_a, tab_b, out_dtypes, *, norm, half, tm, tn, layout="cols"):
    m, k = a.shape
    ntab = tab_a.shape[0] // tm
    cb0 = col0 // tn
    if layout == "stack":
        assert tn == n and tm % PAGE == 0
        out_specs = [pl.BlockSpec((tm * (n // LANES), LANES), lambda j, i: (i, 0))]
        out_shape = [jax.ShapeDtypeStruct((m * (n // LANES), LANES), out_dtypes[0])]
    elif layout == "ki_split":
        assert tn == n == LANES and tm % PAGE == 0
        out_specs = [pl.BlockSpec((tm, LANES), lambda j, i: (i, 0)), pl.BlockSpec((2 * tm, LANES), lambda j, i: (i, 0))]
        out_shape = [jax.ShapeDtypeStruct((m, LANES), F32), jax.ShapeDtypeStruct((2 * m, LANES), BF16)]
    elif layout == "kv_rows":
        assert tn == n
        out_specs, out_shape = _kv_rows_outs(m, n, tm)
    else:
        out_specs = [pl.BlockSpec((tm, tn), lambda j, i: (i, j)) for _ in out_dtypes]
        out_shape = [jax.ShapeDtypeStruct((m, n), dt) for dt in out_dtypes]
    outs = pl.pallas_call(
        functools.partial(_proj_rope_kernel, norm=norm, half=half, layout=layout),
        grid=(n // tn, m // tm),
        in_specs=[
            pl.BlockSpec((tm, k), lambda j, i: (i, 0)),
            pl.BlockSpec((k, tn), lambda j, i: (0, cb0 + j)),
            pl.BlockSpec((1, LANES), lambda j, i: (0, 0)),
            pl.BlockSpec((tm, LANES), lambda j, i: (i % ntab, 0)),
            pl.BlockSpec((tm, LANES), lambda j, i: (i % ntab, 0)),
        ],
        out_specs=out_specs,
        out_shape=out_shape,
        scratch_shapes=[pltpu.VMEM((k, tn), BF16)],
        compiler_params=_cparams("parallel", "arbitrary"),
        name="proj_rope",
    )(a, w, gain, tab_a, tab_b)
    return outs


def _proj_plain_kernel(a_ref, w_ref, *refs, act, kv_rows):
    o_refs, wbf_ref = refs[:-1], refs[-1]
    y = jnp.dot(a_ref[...], _cached_bf16(w_ref, wbf_ref), preferred_element_type=F32)
    if act == "silu":
        y = _silu(y)
    if kv_rows:
        ncol = y.shape[1] // LANES
        for c in range(ncol):
            _store_kv_rows(o_refs, y[:, c * LANES:(c + 1) * LANES], c, ncol)
    else:
        for o_ref in o_refs:
            o_ref[...] = y.astype(o_ref.dtype)


def _proj_plain(a, w, col0, n, out_dtypes, *, act, tm, tn, kv_rows=False):
    m, k = a.shape
    cb0 = col0 // tn
    if kv_rows:
        assert tn == n
        out_specs, out_shape = _kv_rows_outs(m, n, tm)
    else:
        out_specs = [pl.BlockSpec((tm, tn), lambda j, i: (i, j)) for _ in out_dtypes]
        out_shape = [jax.ShapeDtypeStruct((m, n), dt) for dt in out_dtypes]
    return pl.pallas_call(
        functools.partial(_proj_plain_kernel, act=act, kv_rows=kv_rows),
        grid=(n // tn, m // tm),
        in_specs=[
            pl.BlockSpec((tm, k), lambda j, i: (i, 0)),
            pl.BlockSpec((k, tn), lambda j, i: (0, cb0 + j)),
        ],
        out_specs=out_specs,
        out_shape=out_shape,
        scratch_shapes=[pltpu.VMEM((k, tn), BF16)],
        compiler_params=_cparams("parallel", "arbitrary"),
        name="proj_plain",
    )(a, w)


def _proj_forget_kernel(a_ref, w_ref, loglb_ref, log1mlb_ref, lf_ref, k_ref, wbf_ref):
    fr = jnp.dot(a_ref[...], _cached_bf16(w_ref, wbf_ref), preferred_element_type=F32)
    t = jnp.log1p(jnp.exp(-jnp.abs(fr)))
    ls_pos = jnp.minimum(fr, 0.0) - t
    ls_neg = jnp.minimum(-fr, 0.0) - t
    a = loglb_ref[...]
    c = log1mlb_ref[...] + ls_pos
    lf_ref[...] = jnp.maximum(a, c) + jnp.log1p(jnp.exp(-jnp.abs(a - c)))
    k_ref[...] = jnp.exp(log1mlb_ref[...] + ls_neg)


def _proj_forget(a, w, col0, n, log_lb, log_1mlb, *, tm, tn):
    m, k = a.shape
    cb0 = col0 // tn
    return pl.pallas_call(
        _proj_forget_kernel,
        grid=(n // tn, m // tm),
        in_specs=[
            pl.BlockSpec((tm, k), lambda j, i: (i, 0)),
            pl.BlockSpec((k, tn), lambda j, i: (0, cb0 + j)),
            pl.BlockSpec((1, tn), lambda j, i: (0, j)),
            pl.BlockSpec((1, tn), lambda j, i: (0, j)),
        ],
        out_specs=[pl.BlockSpec((tm, tn), lambda j, i: (i, j)) for _ in range(2)],
        out_shape=[jax.ShapeDtypeStruct((m, n), F32) for _ in range(2)],
        scratch_shapes=[pltpu.VMEM((k, tn), BF16)],
        compiler_params=_cparams("parallel", "arbitrary"),
        name="proj_forget",
    )(a, w, log_lb, log_1mlb)


def _mm_res_kernel(a_ref, w_ref, x_ref, gt_ref, o_ref, wbf_ref):
    y = jnp.dot(a_ref[...], _cached_bf16(w_ref, wbf_ref), preferred_element_type=F32)
    o_ref[...] = x_ref[...] + (1.0 + _rows(gt_ref[...], y.shape[0])) * y


def _mm_res(a, w, x, mod, layer, k_gate, *, tm, tn):
    m, k = a.shape
    n = w.shape[1]
    return pl.pallas_call(
        _mm_res_kernel,
        grid=(n // tn, m // tm),
        in_specs=[
            pl.BlockSpec((tm, k), lambda j, i: (i, 0)),
            pl.BlockSpec((k, tn), lambda j, i: (0, j)),
            pl.BlockSpec((tm, tn), lambda j, i: (i, j)),
            mod.spec(layer, k_gate, tn=tn, n_axis=True),
        ],
        out_specs=pl.BlockSpec((tm, tn), lambda j, i: (i, j)),
        out_shape=jax.ShapeDtypeStruct((m, n), F32),
        scratch_shapes=[pltpu.VMEM((k, tn), BF16)],
        compiler_params=_cparams("parallel", "arbitrary"),
        name="mm_res",
    )(a, w, x, mod.arr)


def _sort_key(score):
    bits = pltpu.bitcast(score + 0.0, I32)
    return jnp.where(bits < 0, bits ^ jnp.int32(0x7FFFFFFF), bits)


COUNT_UNROLL = 4
ATTN_TQ = 256


def _attn_prompt_kernel(q_ref, qi_ref, wi_ref, k_ref, v_ref, kiab_ref, o_ref,
                        key_ref, bias_ref, wb_ref, j_ref, s_ref, m_ref, lp_ref, acc_ref, *, topk, idx_bits, kv_unroll):
    tq = q_ref.shape[0]
    tk = PAGE
    nsub = tq // tk
    dh = LANES
    rep = N_HEADS // N_KV_HEADS
    npair = N_IDX_HEADS // 2
    qb = pl.program_id(1)
    nkb = (qb + 1) * nsub
    row = lax.broadcasted_iota(I32, (tq, LANES), 0)
    lane = lax.broadcasted_iota(I32, (tq, LANES), 1)
    qpos = qb * tq + row
    imin = jnp.int32(IMIN)
    nt = (((1,), (1,)), ((), ()))

    wi = wi_ref[...]
    for h in range(N_IDX_HEADS):
        wcol = wi[:, IDX_DIM + h:IDX_DIM + h + 1] * (IDX_DIM ** -0.5)
        wb_ref[h] = jnp.broadcast_to(wcol, (tq, LANES))

    qi_all = qi_ref[...]

    def score_body(kb, c):
        lg = lax.dot_general(qi_all, kiab_ref[kb], nt, preferred_element_type=F32)
        accs = []
        for sb in range(nsub):
            qrows = slice(sb * tk, (sb + 1) * tk)
            acc = jnp.zeros((tk, LANES), F32)
            for c2 in range(npair):
                r = slice((sb * npair + c2) * tk, (sb * npair + c2 + 1) * tk)
                acc = acc + (wb_ref[2 * c2, qrows, :] * jnp.maximum(lg[r, 0:tk], 0.0)
                             + wb_ref[2 * c2 + 1, qrows, :] * jnp.maximum(lg[r, tk:2 * tk], 0.0))
            accs.append(acc)
        acc = accs[0] if nsub == 1 else jnp.concatenate(accs, axis=0)
        key_ref[kb] = jnp.where(kb * tk + lane <= qpos, _sort_key(acc), imin)
        return c

    lax.fori_loop(0, nkb, score_body, 0)
    for u in range(COUNT_UNROLL - 1):
        key_ref[nkb + u] = jnp.full((tq, LANES), IMIN, I32)

    sub_rows = [slice(sb * tk, (sb + 1) * tk) for sb in range(nsub)]
    lane_k = lax.broadcasted_iota(I32, (tk, LANES), 1)
    zeros_col = tuple(jnp.zeros((tk, 1), I32) for _ in range(nsub))

    def count(pred):
        n_it = (nkb + COUNT_UNROLL - 1) // COUNT_UNROLL
        parts = []
        for sb in range(nsub):
            def body(it, part, sb=sb):
                w = [jnp.where(pred(it * COUNT_UNROLL + u, key_ref[it * COUNT_UNROLL + u, sub_rows[sb], :], sb), 1.0, 0.0)
                     for u in range(COUNT_UNROLL)]
                while len(w) > 1:
                    w = [w[i] + w[i + 1] for i in range(0, len(w) - 1, 2)] + ([w[-1]] if len(w) % 2 else [])
                return part + w[0]

            parts.append(lax.fori_loop(0, n_it, body, jnp.zeros((tk, LANES), F32)))
        return [jnp.sum(part, axis=1, keepdims=True) for part in parts]

    def bit_body(i, tus):
        bit = jnp.left_shift(jnp.int32(1), 31 - i)
        cus = [tu | bit for tu in tus]
        cands = [cu ^ imin for cu in cus]
        cnts = count(lambda kb, k, sb: k >= cands[sb])
        return tuple(jnp.where(cnt >= topk, cu, tu) for cnt, cu, tu in zip(cnts, cus, tus))

    thrs = [tu ^ imin for tu in lax.fori_loop(0, 32, bit_body, zeros_col)]
    cnt_gt = count(lambda kb, k, sb: k > thrs[sb])
    cnt_ge = count(lambda kb, k, sb: k >= thrs[sb])
    needs = [topk - c for c in cnt_gt]
    ties = [jnp.logical_and(c > topk, thr > imin) for c, thr in zip(cnt_ge, thrs)]
    any_tie = functools.reduce(jnp.maximum, [jnp.max(jnp.where(t, 1.0, 0.0)) for t in ties])

    j_ref[...] = jnp.full(j_ref.shape, 2 ** idx_bits, I32)

    @pl.when(any_tie > 0.0)
    def _():
        def jbit_body(i, jvs):
            bit = jnp.left_shift(jnp.int32(1), idx_bits - 1 - i)
            cands = [jv | bit for jv in jvs]
            cs = count(lambda kb, k, sb: jnp.logical_and(k == thrs[sb], kb * tk + lane_k < cands[sb]))
            return tuple(jnp.where(c < need, cand, jv) for c, need, cand, jv in zip(cs, needs, cands, jvs))
        jvs = lax.fori_loop(0, idx_bits, jbit_body, zeros_col)
        for sb in range(nsub):
            j_ref[sub_rows[sb], :] = jnp.broadcast_to(jvs[sb], (tk, LANES))

    def bias_body(kb, c):
        for sb in range(nsub):
            k = key_ref[kb, sub_rows[sb], :]
            thr = thrs[sb]
            sel = jnp.logical_or(k > thr, jnp.logical_and(k == thr, kb * tk + lane_k <= j_ref[sub_rows[sb], :]))
            sel = jnp.logical_and(sel, k > imin)
            bias_ref[kb, sub_rows[sb], :] = jnp.where(sel, 0.0, NEG)
        return c

    lax.fori_loop(0, nkb, bias_body, 0)

    for u in range(kv_unroll - 1):
        bias_ref[nkb + u] = jnp.full((tq, LANES), NEG, F32)
    n_it = (nkb + kv_unroll - 1) // kv_unroll
    span = kv_unroll * tk
    c2 = (dh ** -0.5) * 1.4426950408889634
    for g in range(N_KV_HEADS):
        cols = slice(g * dh, (g + 1) * dh)
        qg = jnp.concatenate([q_ref[:, (g * rep + j) * dh:(g * rep + j + 1) * dh] for j in range(rep)], axis=0)
        m_ref[...] = jnp.full(m_ref.shape, NEG, F32)

        def pass1(it, c, qg=qg, cols=cols):
            keys = pl.ds(pl.multiple_of(it * span, span), span)
            s = lax.dot_general(qg, k_ref[keys, cols], nt, preferred_element_type=F32)
            b = jnp.concatenate([bias_ref[it * kv_unroll + u] for u in range(kv_unroll)], axis=1)
            for j in range(rep):
                rows = slice(j * tq, (j + 1) * tq)
                sj = s[rows] + b
                s_ref[it, rows, :] = sj
                t = sj[:, 0:LANES]
                for u in range(1, kv_unroll):
                    t = jnp.maximum(t, sj[:, u * LANES:(u + 1) * LANES])
                m_ref[rows, :] = jnp.maximum(m_ref[rows, :], t)
            return c

        lax.fori_loop(0, n_it, pass1, 0)
        mc = jnp.max(m_ref[...], axis=1, keepdims=True) * c2
        lp_ref[...] = jnp.zeros_like(lp_ref)
        acc_ref[...] = jnp.zeros_like(acc_ref)

        def pass2(it, c, cols=cols, mc=mc):
            keys = pl.ds(pl.multiple_of(it * span, span), span)
            ps = []
            for j in range(rep):
                rows = slice(j * tq, (j + 1) * tq)
                p = jnp.exp2(s_ref[it, rows, :] * c2 - mc[rows])
                t = p[:, 0:LANES]
                for u in range(1, kv_unroll):
                    t = t + p[:, u * LANES:(u + 1) * LANES]
                lp_ref[rows, :] += t
                ps.append(p.astype(BF16))
            acc_ref[...] += jnp.dot(jnp.concatenate(ps, axis=0), v_ref[keys, cols], preferred_element_type=F32)
            return c

        lax.fori_loop(0, n_it, pass2, 0)
        o = acc_ref[...] / jnp.sum(lp_ref[...], axis=1, keepdims=True)
        for j in range(rep):
            h = g * rep + j
            o_ref[:, h * dh:(h + 1) * dh] = o[j * tq:(j + 1) * tq].astype(o_ref.dtype)


def _attn_prompt(q, qi_stack, kiwi_f32, k_bf, v_bf, kiab, batch, seq):
    m = q.shape[0]
    tk = PAGE
    tq = min(ATTN_TQ, seq)
    nq, nk = seq // tq, seq // tk
    topk = min(INDEX_TOPK, seq // 4)
    kvw = k_bf.shape[1]
    npair = N_IDX_HEADS // 2
    rep = N_HEADS // N_KV_HEADS
    kv_unroll = next(u for u in (4, 2, 1) if nk % u == 0)
    kv_spec = pl.BlockSpec((None, seq, kvw), lambda b, i: (b, 0, 0), pipeline_mode=pl.Buffered(1))
    ki_spec = pl.BlockSpec((None, nk, 2 * tk, LANES), lambda b, i: (b, 0, 0, 0), pipeline_mode=pl.Buffered(1))
    return pl.pallas_call(
        functools.partial(_attn_prompt_kernel, topk=topk, idx_bits=max(1, (seq - 1).bit_length()), kv_unroll=kv_unroll),
        grid=(batch, nq),
        in_specs=[
            pl.BlockSpec((tq, q.shape[1]), lambda b, i: (b * nq + i, 0)),
            pl.BlockSpec((npair * tq, LANES), lambda b, i: (b * nq + i, 0)),
            pl.BlockSpec((tq, LANES), lambda b, i: (b * nq + i, 0)),
            kv_spec, kv_spec, ki_spec,
        ],
        out_specs=pl.BlockSpec((tq, q.shape[1]), lambda b, i: (b * nq + i, 0)),
        out_shape=jax.ShapeDtypeStruct((m, q.shape[1]), BF16),
        scratch_shapes=[
            pltpu.VMEM((nk + COUNT_UNROLL - 1, tq, LANES), I32),
            pltpu.VMEM((nk + kv_unroll - 1, tq, LANES), F32),
            pltpu.VMEM((N_IDX_HEADS, tq, LANES), F32),
            pltpu.VMEM((tq, LANES), I32),
            pltpu.VMEM((nk // kv_unroll, rep * tq, kv_unroll * tk), F32),
            pltpu.VMEM((rep * tq, LANES), F32),
            pltpu.VMEM((rep * tq, LANES), F32),
            pltpu.VMEM((rep * tq, LANES), F32),
        ],
        compiler_params=_cparams("parallel", "arbitrary"),
        name="attn_prompt",
    )(q, qi_stack, kiwi_f32, k_bf.reshape(batch, seq, kvw), v_bf.reshape(batch, seq, kvw),
      kiab.reshape(batch, nk, 2 * tk, LANES))


def _idx_sample_kernel(pt_ref, qi_ref, a_ref, kinew_ref, *refs, n_pages, n_tok):
    page_refs = refs[:n_pages]
    o_ref = refs[n_pages]
    a = a_ref[...]
    rows = a.shape[0]
    width = (n_pages + 1) * PAGE
    kit = jnp.concatenate([page_refs[p][...].astype(BF16) for p in range(n_pages)] + [kinew_ref[...]], axis=1)
    lg = jnp.dot(qi_ref[...], kit, preferred_element_type=F32)
    sc = jnp.dot(a, jnp.maximum(lg, 0.0), precision=lax.Precision.HIGHEST, preferred_element_type=F32)
    idx = lax.broadcasted_iota(I32, (rows, width), 1)
    row = lax.broadcasted_iota(I32, (rows, width), 0)
    adm = idx - n_pages * PAGE <= row % n_tok
    o_ref[...] = jnp.where(adm, _sort_key(sc), jnp.int32(IMIN))


def _idx_sample(page_table, qi_ht, amat, ki_new_t, pool_kidx_t, page0):
    bs, n_pages = page_table.shape
    n_tok = qi_ht.shape[1] // N_IDX_HEADS
    rows = amat.shape[1]
    width = (n_pages + 1) * LANES
    page_specs = [pl.BlockSpec((None, IDX_DIM, PAGE), lambda b, pt, p=p: (page0 + pt[b, p], 0, 0))
                  for p in range(n_pages)]
    grid_spec = pltpu.PrefetchScalarGridSpec(
        num_scalar_prefetch=1,
        grid=(bs,),
        in_specs=[
            pl.BlockSpec((None,) + qi_ht.shape[1:], lambda b, pt: (b, 0, 0)),
            pl.BlockSpec((None,) + amat.shape[1:], lambda b, pt: (b, 0, 0)),
            pl.BlockSpec((None,) + ki_new_t.shape[1:], lambda b, pt: (b, 0, 0)),
        ] + page_specs,
        out_specs=pl.BlockSpec((None, rows, width), lambda b, pt: (b, 0, 0)),
    )
    return pl.pallas_call(
        functools.partial(_idx_sample_kernel, n_pages=n_pages, n_tok=n_tok),
        grid_spec=grid_spec,
        out_shape=jax.ShapeDtypeStruct((bs, rows, width), I32),
        compiler_params=_cparams("arbitrary"),
        name="idx_sample",
    )(page_table, qi_ht, amat, ki_new_t, *([pool_kidx_t] * n_pages))


def _topk_bias_kernel(key_ref, bias_ref, *, topk, idx_bits):
    key = key_ref[...]
    rows, width = key.shape
    imin = jnp.int32(IMIN)
    idx = lax.broadcasted_iota(I32, (rows, width), 1)

    def count(pred):
        return jnp.sum(jnp.where(pred, 1.0, 0.0), axis=1, keepdims=True)

    def bit_body(i, tu):
        cu = tu | jnp.left_shift(jnp.int32(1), 31 - i)
        cnt = count(key >= (cu ^ imin))
        return jnp.where(cnt >= topk, cu, tu)

    thr = lax.fori_loop(0, 32, bit_body, jnp.zeros((rows, 1), I32)) ^ imin
    need = topk - count(key > thr)
    eq = key == thr

    def jbit_body(i, jv):
        cand = jv | jnp.left_shift(jnp.int32(1), idx_bits - 1 - i)
        c = count(jnp.logical_and(eq, idx < cand))
        return jnp.where(c < need, cand, jv)

    jv = lax.fori_loop(0, idx_bits, jbit_body, jnp.zeros((rows, 1), I32))
    sel = jnp.logical_or(key > thr, jnp.logical_and(eq, idx <= jv))
    sel = jnp.logical_and(sel, key > imin)
    bias_ref[...] = jnp.where(sel, 0.0, NEG)


def _topk_bias(keys2d, topk, tr):
    rows, width = keys2d.shape
    return pl.pallas_call(
        functools.partial(_topk_bias_kernel, topk=topk, idx_bits=max(1, (width - 1).bit_length())),
        grid=(rows // tr,),
        in_specs=[pl.BlockSpec((tr, width), lambda i: (i, 0))],
        out_specs=pl.BlockSpec((tr, width), lambda i: (i, 0)),
        out_shape=jax.ShapeDtypeStruct((rows, width), F32),
        compiler_params=_cparams("parallel"),
        name="topk_bias",
    )(keys2d)


def _attn_sample_kernel(pt_ref, q_ref, bias_ref, knew_ref, vnew_ref, *refs, n_pages):
    k_refs = refs[:n_pages]
    v_refs = refs[n_pages:2 * n_pages]
    o_ref = refs[2 * n_pages]
    rows, dh = q_ref.shape
    rpg = rows // N_KV_HEADS
    bias8 = bias_ref[...]
    bias = jnp.concatenate([bias8] * (rpg // bias8.shape[0]), axis=0)
    pad = PAGE - knew_ref.shape[0]
    zpad = jnp.zeros((pad, dh), BF16)

    for g in range(N_KV_HEADS):
        qg = q_ref[g * rpg:(g + 1) * rpg, :]
        cols = slice(g * dh, (g + 1) * dh)
        kcat = jnp.concatenate([k_refs[p][pl.ds(g, PAGE, stride=N_KV_HEADS), :].astype(BF16) for p in range(n_pages)]
                               + [knew_ref[:, cols], zpad], axis=0)
        vcat = jnp.concatenate([v_refs[p][pl.ds(g, PAGE, stride=N_KV_HEADS), :].astype(BF16) for p in range(n_pages)]
                               + [vnew_ref[:, cols], zpad], axis=0)
        s = lax.dot_general(qg, kcat, (((1,), (1,)), ((), ())), preferred_element_type=F32) * (dh ** -0.5) + bias
        m = jnp.max(s, axis=1, keepdims=True)
        pexp = jnp.exp(s - m)
        l = jnp.sum(pexp, axis=1, keepdims=True)
        o = jnp.dot(pexp.astype(BF16), vcat, preferred_element_type=F32)
        o_ref[g * rpg:(g + 1) * rpg, :] = (o / l).astype(o_ref.dtype)


def _attn_sample(page_table, q64, bias, k_new, v_new, pool_k, pool_v, page0):
    bs, n_pages = page_table.shape
    pspecs = [pl.BlockSpec((None,) + pool_k.shape[1:], lambda b, pt, p=p: (page0 + pt[b, p], 0, 0))
              for p in range(n_pages)]
    npg = len(pspecs)
    grid_spec = pltpu.PrefetchScalarGridSpec(
        num_scalar_prefetch=1,
        grid=(bs,),
        in_specs=[
            pl.BlockSpec((None,) + q64.shape[1:], lambda b, pt: (b, 0, 0)),
            pl.BlockSpec((None,) + bias.shape[1:], lambda b, pt: (b, 0, 0)),
            pl.BlockSpec((None,) + k_new.shape[1:], lambda b, pt: (b, 0, 0)),
            pl.BlockSpec((None,) + v_new.shape[1:], lambda b, pt: (b, 0, 0)),
        ] + pspecs + pspecs,
        out_specs=pl.BlockSpec((None,) + q64.shape[1:], lambda b, pt: (b, 0, 0)),
    )
    return pl.pallas_call(
        functools.partial(_attn_sample_kernel, n_pages=n_pages),
        grid_spec=grid_spec,
        out_shape=jax.ShapeDtypeStruct(q64.shape, BF16),
        compiler_params=_cparams("arbitrary"),
        name="attn_sample",
    )(page_table, q64, bias, k_new, v_new, *([pool_k] * npg), *([pool_v] * npg))


def _gla_out(o, sg, gain):
    return (o * lax.rsqrt(jnp.mean(o * o, axis=-1, keepdims=True) + EPS) * gain * sg)


GLA_CHUNK = 16


GLA_HEADS = 2


def _gla_prompt_kernel(q_ref, k_ref, v_ref, g_ref, sg_ref, gain_ref, y_ref, s_ref, st_ref, bb_ref, *, blk):
    ci = pl.program_id(2)
    c = GLA_CHUNK

    @pl.when(ci == 0)
    def _():
        st_ref[...] = jnp.zeros_like(st_ref)

    tri = (lax.broadcasted_iota(I32, (blk, blk), 0) >= lax.broadcasted_iota(I32, (blk, blk), 1)).astype(F32)
    ones = jnp.ones((LANES, LANES), BF16)
    nsub = c // SUBLANES
    nch = blk // c
    gain = gain_ref[...]
    r8 = lax.broadcasted_iota(I32, (SUBLANES, LANES), 0)
    mask_bias = [jnp.where(r8 >= j, 0.0, NEG) for j in range(SUBLANES)]
    log2e = 1.4426950408889634

    nhb = st_ref.shape[0]
    heads = [slice(hh * LANES, (hh + 1) * LANES) for hh in range(nhb)]

    for bi in range(q_ref.shape[0] // blk):
        rows = slice(bi * blk, (bi + 1) * blk)
        bb_ref[rows, :] = jnp.dot(tri, g_ref[rows, :] * log2e, precision=lax.Precision.HIGHEST,
                                  preferred_element_type=F32)

    def body(bi, carry):
        r0 = pl.multiple_of(bi * blk, blk)
        bb = bb_ref[pl.ds(r0, blk), :]

        qs, vs, bs, xs = {}, {}, {}, []
        for hh in range(nhb):
            for ch in range(nch):
                lo = ch * c
                rows = pl.ds(r0 + lo, c)
                q, k, v = q_ref[rows, heads[hh]], k_ref[rows, heads[hh]], v_ref[rows, heads[hh]]
                bh = bb[:, heads[hh]]
                b = bh[lo:lo + c] - bh[lo - 1:lo] if ch > 0 else bh[lo:lo + c]
                qs[hh, ch], vs[hh, ch], bs[hh, ch] = q, v, (b, k)
                for s in range(c):
                    i0 = s // SUBLANES
                    qk = q[i0 * SUBLANES:] * k[s:s + 1]
                    diff = b[i0 * SUBLANES:] - b[s:s + 1]
                    for i in range(i0, nsub):
                        d = diff[(i - i0) * SUBLANES:(i - i0 + 1) * SUBLANES]
                        if i == i0:
                            d = d + mask_bias[s % SUBLANES]
                        xs.append(qk[(i - i0) * SUBLANES:(i - i0 + 1) * SUBLANES] * jnp.exp2(d))
        a_all = jnp.dot(jnp.concatenate(xs, axis=0).astype(BF16), ones, preferred_element_type=F32)
        upds, decs, qes = {}, {}, {}
        for hh in range(nhb):
            for ch in range(nch):
                b, k = bs[hh, ch]
                b_end = b[c - 1:c]
                kd = (k * jnp.exp2(b_end - b)).astype(BF16)
                upds[hh, ch] = lax.dot_general(vs[hh, ch].astype(BF16), kd, (((0,), (0,)), ((), ())),
                                               preferred_element_type=F32)
                decs[hh, ch] = jnp.exp2(b_end)
                qes[hh, ch] = (qs[hh, ch] * jnp.exp2(b)).astype(BF16)

        off = 0
        for hh in range(nhb):
            st = st_ref[hh]
            for ch in range(nch):
                o = lax.dot_general(qes[hh, ch], st.astype(BF16), (((1,), (1,)), ((), ())), preferred_element_type=F32)
                st = st * decs[hh, ch] + upds[hh, ch]
                o_parts = [o[i * SUBLANES:(i + 1) * SUBLANES] for i in range(nsub)]
                v = vs[hh, ch]
                for s in range(c):
                    for i in range(s // SUBLANES, nsub):
                        o_parts[i] = o_parts[i] + a_all[off:off + SUBLANES] * v[s:s + 1]
                        off += SUBLANES
                rows = pl.ds(r0 + ch * c, c)
                y_ref[rows, heads[hh]] = _gla_out(jnp.concatenate(o_parts, axis=0), sg_ref[rows, heads[hh]],
                                                  gain[:, heads[hh]]).astype(y_ref.dtype)
            st_ref[hh] = st
        return carry

    lax.fori_loop(0, q_ref.shape[0] // blk, body, 0)

    @pl.when(ci == pl.num_programs(2) - 1)
    def _():
        for hh in range(nhb):
            s_ref[hh] = st_ref[hh].T


def _gla_prompt(qs, kk, vv, lf, sg, gain, batch, seq, tb):
    m, d = qs.shape
    nh = d // LANES
    nc = seq // tb
    nhb = GLA_HEADS
    tok = pl.BlockSpec((tb, nhb * LANES), lambda b, h, c: (b * nc + c, h))
    return pl.pallas_call(
        functools.partial(_gla_prompt_kernel, blk=min(LANES, tb)),
        grid=(batch, nh // nhb, nc),
        in_specs=[tok, tok, tok, tok, tok, pl.BlockSpec((1, nhb * LANES), lambda b, h, c: (0, h))],
        out_specs=[tok, pl.BlockSpec((None, nhb, LANES, LANES), lambda b, h, c: (b, h, 0, 0))],
        out_shape=[jax.ShapeDtypeStruct((m, d), BF16), jax.ShapeDtypeStruct((batch, nh, LANES, LANES), F32)],
        scratch_shapes=[pltpu.VMEM((nhb, LANES, LANES), F32), pltpu.VMEM((tb, nhb * LANES), F32)],
        compiler_params=_cparams("parallel", "parallel", "arbitrary"),
        name="gla_prompt",
    )(qs, kk, vv, lf, sg, gain)


def _gla_sample_kernel(q_ref, k_ref, v_ref, g_ref, sg_ref, gain_ref, s0_ref, y_ref, s_ref, *, n_tok):
    nh = s0_ref.shape[0]
    tp = q_ref.shape[0]
    log2e = 1.4426950408889634
    tri = (lax.broadcasted_iota(I32, (tp, tp), 0) >= lax.broadcasted_iota(I32, (tp, tp), 1)).astype(F32)
    b_all = jnp.dot(tri, g_ref[...] * log2e, precision=lax.Precision.HIGHEST, preferred_element_type=F32)
    b_end_all = b_all[tp - 1:tp]
    r8 = lax.broadcasted_iota(I32, (tp, LANES), 0)
    mask_bias = [jnp.where(r8 >= j, 0.0, NEG) for j in range(n_tok)]
    row0 = lax.broadcasted_iota(I32, b_all.shape, 0) == 0
    dec_cols = lax.dot_general(jnp.where(row0, jnp.exp2(b_end_all), 0.0), jnp.ones((tp, LANES), F32),
                               (((0,), (0,)), ((), ())), precision=lax.Precision.HIGHEST,
                               preferred_element_type=F32)
    xs = []
    for h in range(nh):
        c = slice(h * LANES, (h + 1) * LANES)
        q, k, b = q_ref[:, c], k_ref[:, c], b_all[:, c]
        for s in range(n_tok):
            xs.append(q * k[s:s + 1] * jnp.exp2(b - b[s:s + 1] + mask_bias[s]))
    a_all = jnp.dot(jnp.concatenate(xs, axis=0).astype(BF16), jnp.ones((LANES, LANES), BF16),
                    preferred_element_type=F32)
    for h in range(nh):
        c = slice(h * LANES, (h + 1) * LANES)
        q, k, v, b = q_ref[:, c], k_ref[:, c], v_ref[:, c], b_all[:, c]
        st0 = s0_ref[h]
        o = jnp.dot((q * jnp.exp2(b)).astype(BF16), st0.astype(BF16), preferred_element_type=F32)
        for s in range(n_tok):
            o = o + a_all[(h * n_tok + s) * tp:(h * n_tok + s + 1) * tp] * v[s:s + 1]
        kd = (k * jnp.exp2(b_end_all[:, c] - b)).astype(BF16)
        upd = lax.dot_general(kd, v.astype(BF16), (((0,), (0,)), ((), ())), preferred_element_type=F32)
        s_ref[h] = dec_cols[h * LANES:(h + 1) * LANES] * st0 + upd
        y_ref[:, c] = _gla_out(o, sg_ref[:, c], gain_ref[:, c]).astype(y_ref.dtype)


def _gla_sample(qs, kk, vv, lf, sg, gain, state_pool, seq0, n_tok):
    bs, tp, d = qs.shape
    nh = state_pool.shape[1]
    tok = pl.BlockSpec((None, tp, d), lambda b: (b, 0, 0))
    st_in = pl.BlockSpec((None, nh, LANES, LANES), lambda b: (seq0 + b, 0, 0, 0))
    st = pl.BlockSpec((None, nh, LANES, LANES), lambda b: (b, 0, 0, 0))
    return pl.pallas_call(
        functools.partial(_gla_sample_kernel, n_tok=n_tok),
        grid=(bs,),
        in_specs=[tok, tok, tok, tok, tok, pl.BlockSpec((1, d), lambda b: (0, 0)), st_in],
        out_specs=[tok, st],
        out_shape=[jax.ShapeDtypeStruct((bs, tp, d), BF16), jax.ShapeDtypeStruct((bs, nh, LANES, LANES), F32)],
        compiler_params=_cparams("parallel"),
        name="gla_sample",
    )(qs, kk, vv, lf, sg, gain, state_pool)


def _rope_tables(pos, d):
    inv = ROPE_THETA ** (-jnp.arange(0, d, 2, dtype=F32) / d)
    ang = pos.astype(F32)[:, None] * inv[None, :]
    cos, sin = jnp.cos(ang), jnp.sin(ang)
    reps = LANES // d
    a = jnp.tile(jnp.concatenate([cos, cos], axis=1), (1, reps))
    b = jnp.tile(jnp.concatenate([-sin, sin], axis=1), (1, reps))
    return a, b


def _kiwi_tables(pos):
    a64, b64 = _rope_tables(pos, IDX_DIM)
    p = pos.shape[0]
    lane = jnp.arange(IDX_DIM)
    wi_scale = jnp.where(lane < N_IDX_HEADS, N_IDX_HEADS ** -0.5, 0.0).astype(F32)
    a = jnp.concatenate([a64[:, :IDX_DIM], jnp.broadcast_to(wi_scale, (p, IDX_DIM))], axis=1)
    b = jnp.concatenate([b64[:, :IDX_DIM], jnp.zeros((p, IDX_DIM), F32)], axis=1)
    return a, b


def _attn_projections(h, w_in, w_kiwi, q_gain, k_gain, pos, tm, stack_qi):
    dh = LANES
    qw = N_HEADS * dh
    kvw = N_KV_HEADS * dh
    qiw = N_IDX_HEADS * IDX_DIM
    a128, b128 = _rope_tables(pos, dh)
    a64, b64 = _rope_tables(pos, IDX_DIM)
    akw, bkw = _kiwi_tables(pos)
    tn = 512
    (q,) = _proj_rope(h, w_in, 0, qw, q_gain, a128, b128, [BF16], norm=True, half=dh // 2, tm=tm, tn=tn)
    k32, k16 = _proj_rope(h, w_in, qw, kvw, k_gain, a128, b128, [F32, BF16], norm=True, half=dh // 2, tm=tm, tn=tn,
                          layout="kv_rows" if stack_qi else "cols")
    tm_plain = 2 * tm if h.shape[0] % (2 * tm) == 0 else tm
    v32, v16 = _proj_plain(h, w_in, qw + kvw, kvw, [F32, BF16], act=None, tm=tm_plain, tn=tn, kv_rows=stack_qi)
    if stack_qi:
        (qi,) = _proj_rope(h, w_in, qw + 2 * kvw, qiw, q_gain, a64, b64, [BF16], norm=False, half=IDX_DIM // 2,
                           tm=tm, tn=qiw, layout="stack")
    else:
        (qi,) = _proj_rope(h, w_in, qw + 2 * kvw, qiw, q_gain, a64, b64, [BF16], norm=False, half=IDX_DIM // 2,
                           tm=tm, tn=tn)
    if stack_qi:
        kw32, kiab = _proj_rope(h, w_kiwi, 0, LANES, q_gain, akw, bkw, [F32, BF16], norm=False,
                                half=IDX_DIM // 2, tm=tm, tn=LANES, layout="ki_split")
    else:
        (kw32,), kiab = _proj_rope(h, w_kiwi, 0, LANES, q_gain, akw, bkw, [F32], norm=False,
                                   half=IDX_DIM // 2, tm=tm, tn=LANES), None
    return q, qi, k32, k16, v32, v16, kw32, kiab


def _hgrn_projections(h, w_in, log_lb, log_1mlb, tm):
    d = h.shape[1]
    tn = 512
    (qs,) = _proj_plain(h, w_in, 0, d, [F32], act="silu", tm=tm, tn=tn)
    lf, kk = _proj_forget(h, w_in, d, d, log_lb, log_1mlb, tm=tm, tn=tn)
    (vv,) = _proj_plain(h, w_in, 2 * d, d, [F32], act=None, tm=tm, tn=tn)
    (sg,) = _proj_plain(h, w_in, 3 * d, d, [F32], act="silu", tm=tm, tn=tn)
    return qs, kk, vv, lf, sg


def _stack(arrs):
    return arrs[0][None] if len(arrs) == 1 else jnp.stack(arrs)


def kernel(x_prompt, x_sample, c_prompt, c_sample, cache_k, cache_v, cache_kidx, state_hgrn, page_table, norm_gain, w_ada, b_ada, w_ff_up, w_ff_down, w_attn_in, w_attn_out, q_norm, k_norm, w_rec_in, w_rec_out, rec_out_norm, lb_logits):
    batch, seq, d = x_prompt.shape
    bs, n_tok, _ = x_sample.shape
    depth = w_ada.shape[0]
    n_pages = page_table.shape[1]
    past = n_pages * PAGE
    dh = LANES
    kvw = N_KV_HEADS * dh

    c_all = jnp.concatenate([c_prompt, jnp.zeros((SUBLANES - batch, d), F32), c_sample], axis=0)
    mod = _ada_mod(c_all, w_ada, b_ada)
    tm_p = min(1024, seq)
    mod_p = _Mod(mod[:, :batch].reshape(depth, batch, 9, 1, d), False, seq // tm_p)
    mod_s = _Mod(mod[:, SUBLANES:].reshape(depth, bs, 9, d).transpose(0, 2, 1, 3), True, None)

    gains = norm_gain.reshape(depth, 3, 1, d)
    p = jnp.exp(lb_logits - jnp.max(lb_logits, axis=0, keepdims=True))
    p = p / jnp.sum(p, axis=0, keepdims=True)
    lb_all = jnp.cumsum(p, axis=0) - p[0]

    xp = x_prompt.reshape(batch * seq, d)
    xs = x_sample.transpose(1, 0, 2).reshape(n_tok * bs, d)
    pos_p = jnp.arange(seq)
    pos_s = jnp.repeat(past + jnp.arange(n_tok), bs)
    ms = n_tok * bs

    outs_p, outs_s = {}, {}
    for i in range(depth):
        j = i // 2
        xp = _ffn(xp, mod_p, i, 0, gains, w_ff_up, w_ff_down, 0, tm_p, 512)
        xs = _ffn(xs, mod_s, i, 0, gains, w_ff_up, w_ff_down, 0, ms, 512)
        hp = _prenorm(xp, mod_p, i, 1, gains, tm_p)
        hs = _prenorm(xs, mod_s, i, 1, gains, bs)
        if i % 2 == 0:
            w_in = w_attn_in[j]
            col = N_HEADS * dh + 2 * kvw + N_IDX_HEADS * IDX_DIM
            w_kiwi = jnp.pad(w_in[:, col:], ((0, 0), (0, LANES - (w_in.shape[1] - col))))
            qg = q_norm[j].reshape(1, dh)
            kg = k_norm[j].reshape(1, dh)
            q, qi, k32, k16, v32, v16, kw32, kiab = _attn_projections(hp, w_in, w_kiwi, qg, kg, pos_p, tm_p // 2, True)
            o = _attn_prompt(q, qi, kw32, k16, v16, kiab, batch, seq)
            xp = _mm_res(o, w_attn_out[j], xp, mod_p, i, 5, tm=tm_p, tn=512)
            outs_p[i] = (k32, v32, kw32[:, :IDX_DIM])
            q, qi, k32, k16, v32, v16, kw32, _ = _attn_projections(hs, w_in, w_kiwi, qg, kg, pos_s, bs, False)

            def bmajor(a):
                return a.reshape(n_tok, bs, a.shape[1]).transpose(1, 0, 2)

            qi_b = bmajor(qi).reshape(bs, n_tok, N_IDX_HEADS, IDX_DIM)
            qi_ht = qi_b.transpose(0, 2, 1, 3).reshape(bs, N_IDX_HEADS * n_tok, IDX_DIM)
            wi_b = bmajor(kw32)[:, :, IDX_DIM:IDX_DIM + N_IDX_HEADS] * (IDX_DIM ** -0.5)
            amat = (wi_b[:, :, :, None] * jnp.eye(n_tok, dtype=F32)[None, :, None, :]).reshape(bs, n_tok, -1)
            amat = jnp.concatenate([amat, amat], axis=1)
            rows_pad = 2 * n_tok

            def pad_rows(a):
                return jnp.pad(a, ((0, 0), (0, rows_pad - a.shape[1]), (0, 0)))

            ki_new_t = jnp.pad(bmajor(kw32)[:, :, :IDX_DIM].astype(BF16).transpose(0, 2, 1),
                               ((0, 0), (0, 0), (0, PAGE - n_tok)))
            n_pool = cache_k.shape[1]
            pool_kidx_t = cache_kidx.reshape((-1,) + cache_kidx.shape[2:]).transpose(0, 2, 1)
            keys = _idx_sample(page_table, qi_ht, amat, ki_new_t, pool_kidx_t, j * n_pool)
            width = keys.shape[2]
            topk = min(INDEX_TOPK, (past + n_tok) // 4)
            bias = _topk_bias(keys.reshape(bs * rows_pad, width), topk, min(256, bs * rows_pad)).reshape(bs, rows_pad, width)
            q64 = bmajor(q).reshape(bs, n_tok, N_HEADS, dh).transpose(0, 2, 1, 3).reshape(bs, N_HEADS * n_tok, dh)
            o64 = _attn_sample(page_table, q64, bias, pad_rows(bmajor(k16)), pad_rows(bmajor(v16)),
                               cache_k.reshape(-1, PAGE * N_KV_HEADS, dh), cache_v.reshape(-1, PAGE * N_KV_HEADS, dh),
                               j * n_pool)
            o = o64.reshape(bs, N_HEADS, n_tok, dh).transpose(2, 0, 1, 3).reshape(ms, N_HEADS * dh)
            xs = _mm_res(o, w_attn_out[j], xs, mod_s, i, 5, tm=ms, tn=512)
            outs_s[i] = (bmajor(k32), bmajor(v32), bmajor(kw32)[:, :, :IDX_DIM])
        else:
            lb = lb_all[i].reshape(1, d)
            log_lb, log_1mlb = jnp.log(lb), jnp.log1p(-lb)
            gain = rec_out_norm[j].reshape(1, d)
            qs, kk, vv, lf, sg = _hgrn_projections(hp, w_rec_in[j], log_lb, log_1mlb, tm_p)
            y, st = _gla_prompt(qs, kk, vv, lf, sg, gain, batch, seq, min(512, seq))
            xp = _mm_res(y, w_rec_out[j], xp, mod_p, i, 5, tm=tm_p, tn=512)
            outs_p[i] = (st,)
            qs, kk, vv, lf, sg = _hgrn_projections(hs, w_rec_in[j], log_lb, log_1mlb, bs)

            def bpad(a):
                a = a.reshape(n_tok, bs, d).transpose(1, 0, 2)
                return jnp.pad(a, ((0, 0), (0, SUBLANES - n_tok), (0, 0)))

            y, st = _gla_sample(bpad(qs), bpad(kk), bpad(vv), bpad(lf), bpad(sg), gain,
                                state_hgrn.reshape((-1,) + state_hgrn.shape[2:]), j * bs, n_tok)
            y = y[:, :n_tok].transpose(1, 0, 2).reshape(ms, d)
            xs = _mm_res(y, w_rec_out[j], xs, mod_s, i, 5, tm=ms, tn=512)
            outs_s[i] = (st,)
        xp = _ffn(xp, mod_p, i, 2, gains, w_ff_up, w_ff_down, 1, tm_p, 512)
        xs = _ffn(xs, mod_s, i, 2, gains, w_ff_up, w_ff_down, 1, ms, 512)

    attn_layers = [i for i in range(depth) if i % 2 == 0]
    rec_layers = [i for i in range(depth) if i % 2 == 1]
    na, pages_p = len(attn_layers), seq // PAGE
    y_prompt = xp.reshape(batch, seq, d)
    y_sample = xs.reshape(n_tok, bs, d).transpose(1, 0, 2)
    k_prompt = _stack([outs_p[i][0] for i in attn_layers]).reshape(na, batch, pages_p, PAGE, N_KV_HEADS, dh)
    v_prompt = _stack([outs_p[i][1] for i in attn_layers]).reshape(na, batch, pages_p, PAGE, N_KV_HEADS, dh)
    kidx_prompt = _stack([outs_p[i][2] for i in attn_layers]).reshape(na, batch, pages_p, PAGE, IDX_DIM)
    state_prompt = _stack([outs_p[i][0] for i in rec_layers]).astype(state_hgrn.dtype)
    k_sample = _stack([outs_s[i][0] for i in attn_layers]).reshape(na, bs, n_tok, N_KV_HEADS, dh)
    v_sample = _stack([outs_s[i][1] for i in attn_layers]).reshape(na, bs, n_tok, N_KV_HEADS, dh)
    kidx_sample = _stack([outs_s[i][2] for i in attn_layers])
    state_sample = _stack([outs_s[i][0] for i in rec_layers]).astype(state_hgrn.dtype)
    return (y_prompt, y_sample, k_prompt, v_prompt, kidx_prompt, state_prompt, k_sample, v_sample, kidx_sample, state_sample)
```

```python
import functools

import jax
import jax.numpy as jnp
from jax import lax
from jax.experimental import pallas as pl
from jax.experimental.pallas import tpu as pltpu

F32 = jnp.float32
BF16 = jnp.bfloat16
I32 = jnp.int32

LANES = 128
SUBLANES = 8
VMEM_LIMIT = 60 * 1024 * 1024

N_HEADS = 16
N_KV_HEADS = 4
N_IDX_HEADS = 16
IDX_DIM = 64
INDEX_TOPK = 256
ROPE_THETA = 10000.0
REC_CHUNK = 64
EPS = 1e-6
PAGE = 128
NEG = -1e30
IMIN = -2 ** 31


def _cparams(*sem):
    return pltpu.CompilerParams(dimension_semantics=sem, vmem_limit_bytes=VMEM_LIMIT)


def _rows(mod, tm):
    rm = mod.shape[0]
    if rm == 1 or rm == tm:
        return mod
    return jnp.concatenate([mod] * (tm // rm), axis=0)


def _ada_norm(x, gain, shift, scale):
    tm = x.shape[0]
    y = x * lax.rsqrt(jnp.mean(x * x, axis=-1, keepdims=True) + EPS) * gain
    return y * (1.0 + _rows(scale, tm)) + _rows(shift, tm)


def _silu(x):
    return x * jax.nn.sigmoid(x)


def _ada_mod_kernel(c_ref, w_ref, b_ref, o_ref):
    cs = _silu(c_ref[...]).astype(BF16)
    o_ref[...] = jnp.dot(cs, w_ref[...].astype(BF16), preferred_element_type=F32) + b_ref[...]


def _ada_mod(c_all, w_ada, b_ada):
    depth, d, n = w_ada.shape
    m = c_all.shape[0]
    tn = 2048
    return pl.pallas_call(
        _ada_mod_kernel,
        grid=(depth, n // tn),
        in_specs=[
            pl.BlockSpec((m, d), lambda i, j: (0, 0)),
            pl.BlockSpec((None, d, tn), lambda i, j: (i, 0, j)),
            pl.BlockSpec((None, 1, tn), lambda i, j: (i, 0, j)),
        ],
        out_specs=pl.BlockSpec((None, m, tn), lambda i, j: (i, 0, j)),
        out_shape=jax.ShapeDtypeStruct((depth, m, n), F32),
        compiler_params=_cparams("parallel", "parallel"),
        name="ada_mod",
    )(c_all, w_ada, b_ada.reshape(depth, 1, n))


class _Mod:
    def __init__(self, arr, per_row, tiles_per_seq):
        self.arr = arr
        self.per_row = per_row
        self.tps = tiles_per_seq

    def spec(self, layer, k, tn=None, n_axis=False):
        d = self.arr.shape[-1]
        tn = d if tn is None else tn
        if self.per_row:
            rows = self.arr.shape[2]
            if n_axis:
                return pl.BlockSpec((None, None, rows, tn), lambda n, m: (layer, k, 0, n))
            return pl.BlockSpec((None, None, rows, tn), lambda m, *_: (layer, k, 0, 0))
        tps = self.tps
        if n_axis:
            return pl.BlockSpec((None, None, None, 1, tn), lambda n, m: (layer, m // tps, k, 0, n))
        return pl.BlockSpec((None, None, None, 1, tn), lambda m, *_: (layer, m // tps, k, 0, 0))


def _ffn_kernel(x_ref, sh_ref, sc_ref, gt_ref, gain_ref, wa_ref, wb_ref, wd_ref, o_ref, h_ref):
    j = pl.program_id(1)

    @pl.when(j == 0)
    def _():
        h_ref[...] = _ada_norm(x_ref[...], gain_ref[...], sh_ref[...], sc_ref[...]).astype(BF16)
        o_ref[...] = jnp.zeros_like(o_ref)

    h = h_ref[...]
    tf = wa_ref.shape[1]
    nsplit = 2 if tf % (2 * LANES) == 0 else 1
    gs = []
    for c in range(nsplit):
        cols = slice(c * tf // nsplit, (c + 1) * tf // nsplit)
        a = jnp.dot(h, wa_ref[:, cols].astype(BF16), preferred_element_type=F32)
        b = jnp.dot(h, wb_ref[:, cols].astype(BF16), preferred_element_type=F32)
        gs.append((_silu(a) * b).astype(BF16))
    g = gs[0] if nsplit == 1 else jnp.concatenate(gs, axis=1)
    o_ref[...] += jnp.dot(g, wd_ref[...].astype(BF16), preferred_element_type=F32)

    @pl.when(j == pl.num_programs(1) - 1)
    def _():
        tm = o_ref.shape[0]
        o_ref[...] = x_ref[...] + 0.5 * (1.0 + _rows(gt_ref[...], tm)) * o_ref[...]


def _ffn(x, mod, layer, sub, gain, w_up, w_down, ffn_idx, tm, tf):
    m, d = x.shape
    f = w_down.shape[2]
    nf = f // tf
    return pl.pallas_call(
        _ffn_kernel,
        grid=(m // tm, nf),
        in_specs=[
            pl.BlockSpec((tm, d), lambda i, j: (i, 0), pipeline_mode=pl.Buffered(1)),
            mod.spec(layer, 3 * sub + 0),
            mod.spec(layer, 3 * sub + 1),
            mod.spec(layer, 3 * sub + 2),
            pl.BlockSpec((None, None, 1, d), lambda i, j: (layer, sub, 0, 0)),
            pl.BlockSpec((None, None, d, tf), lambda i, j: (layer, ffn_idx, 0, j)),
            pl.BlockSpec((None, None, d, tf), lambda i, j: (layer, ffn_idx, 0, j + nf)),
            pl.BlockSpec((None, None, tf, d), lambda i, j: (layer, ffn_idx, j, 0)),
        ],
        out_specs=pl.BlockSpec((tm, d), lambda i, j: (i, 0), pipeline_mode=pl.Buffered(1)),
        out_shape=jax.ShapeDtypeStruct((m, d), F32),
        scratch_shapes=[pltpu.VMEM((tm, d), BF16)],
        compiler_params=_cparams("parallel", "arbitrary"),
        name="ffn",
    )(x, mod.arr, mod.arr, mod.arr, gain, w_up, w_up, w_down)


def _prenorm_kernel(x_ref, sh_ref, sc_ref, gain_ref, h_ref):
    h_ref[...] = _ada_norm(x_ref[...], gain_ref[...], sh_ref[...], sc_ref[...]).astype(BF16)


def _prenorm(x, mod, layer, sub, gain, tm):
    m, d = x.shape
    return pl.pallas_call(
        _prenorm_kernel,
        grid=(m // tm,),
        in_specs=[
            pl.BlockSpec((tm, d), lambda i: (i, 0)),
            mod.spec(layer, 3 * sub + 0),
            mod.spec(layer, 3 * sub + 1),
            pl.BlockSpec((None, None, 1, d), lambda i: (layer, sub, 0, 0)),
        ],
        out_specs=pl.BlockSpec((tm, d), lambda i: (i, 0)),
        out_shape=jax.ShapeDtypeStruct((m, d), BF16),
        compiler_params=_cparams("parallel"),
        name="prenorm",
    )(x, mod.arr, mod.arr, gain)


def _rot_half(y, half):
    if 2 * half == LANES:
        return pltpu.roll(y, half, 1)
    lane = lax.broadcasted_iota(I32, y.shape, 1)
    return jnp.where(lane % (2 * half) < half, pltpu.roll(y, LANES - half, 1), pltpu.roll(y, half, 1))


def _cached_bf16(w_ref, wbf_ref):
    @pl.when(pl.program_id(1) == 0)
    def _():
        wbf_ref[...] = w_ref[...].astype(BF16)
    return wbf_ref[...]


def _proj_rope_kernel(a_ref, w_ref, g_ref, ca_ref, cb_ref, *refs, norm, half, layout):
    o_refs, wbf_ref = refs[:-1], refs[-1]
    wbf = _cached_bf16(w_ref, wbf_ref)
    a = a_ref[...]
    ca = ca_ref[...]
    cb = cb_ref[...]
    tm = a.shape[0]
    ncol = wbf.shape[1] // LANES
    pair = 2 if ncol % 2 == 0 else 1
    ys = [jnp.dot(a, wbf[:, p * pair * LANES:(p + 1) * pair * LANES], preferred_element_type=F32)
          for p in range(ncol // pair)]
    for c in range(ncol):
        yc = ys[c // pair][:, (c % pair) * LANES:(c % pair + 1) * LANES]
        if norm:
            yc = yc * lax.rsqrt(jnp.mean(yc * yc, axis=-1, keepdims=True) + EPS) * g_ref[...]
        r = yc * ca + _rot_half(yc, half) * cb
        if layout == "ki_split":
            lane = lax.broadcasted_iota(I32, r.shape, 1)
            lo = jnp.where(lane < IDX_DIM, r, 0.0)
            hi = pltpu.roll(lo, IDX_DIM, 1)
            o_refs[0][...] = r
            for qb in range(tm // PAGE):
                rows = slice(qb * PAGE, (qb + 1) * PAGE)
                o_refs[1][2 * qb * PAGE:(2 * qb + 1) * PAGE, :] = lo[rows].astype(BF16)
                o_refs[1][(2 * qb + 1) * PAGE:(2 * qb + 2) * PAGE, :] = hi[rows].astype(BF16)
        elif layout == "stack":
            for qb in range(tm // PAGE):
                o_refs[0][(qb * ncol + c) * PAGE:(qb * ncol + c + 1) * PAGE, :] = (
                    r[qb * PAGE:(qb + 1) * PAGE].astype(o_refs[0].dtype))
        elif layout == "kv_rows":
            _store_kv_rows(o_refs, r, c, ncol)
        else:
            for o_ref in o_refs:
                o_ref[:, c * LANES:(c + 1) * LANES] = r.astype(o_ref.dtype)


def _store_kv_rows(o_refs, r, c, ncol):
    o_refs[0][pl.ds(c, r.shape[0], stride=ncol), :] = r
    o_refs[1][:, c * LANES:(c + 1) * LANES] = r.astype(BF16)


def _kv_rows_outs(m, n, tm):
    ncol = n // LANES
    specs = [pl.BlockSpec((tm * ncol, LANES), lambda j, i: (i, 0)), pl.BlockSpec((tm, n), lambda j, i: (i, 0))]
    shapes = [jax.ShapeDtypeStruct((m * ncol, LANES), F32), jax.ShapeDtypeStruct((m, n), BF16)]
    return specs, shapes


def _proj_rope(a, w, col0, n, gain, tab_a, tab_b, out_dtypes, *, norm, half, tm, tn, layout="cols"):
    m, k = a.shape
    ntab = tab_a.shape[0] // tm
    cb0 = col0 // tn
    if layout == "stack":
        assert tn == n and tm % PAGE == 0
        out_specs = [pl.BlockSpec((tm * (n // LANES), LANES), lambda j, i: (i, 0))]
        out_shape = [jax.ShapeDtypeStruct((m * (n // LANES), LANES), out_dtypes[0])]
    elif layout == "ki_split":
        assert tn == n == LANES and tm % PAGE == 0
        out_specs = [pl.BlockSpec((tm, LANES), lambda j, i: (i, 0)), pl.BlockSpec((2 * tm, LANES), lambda j, i: (i, 0))]
        out_shape = [jax.ShapeDtypeStruct((m, LANES), F32), jax.ShapeDtypeStruct((2 * m, LANES), BF16)]
    elif layout == "kv_rows":
        assert tn == n
        out_specs, out_shape = _kv_rows_outs(m, n, tm)
    else:
        out_specs = [pl.BlockSpec((tm, tn), lambda j, i: (i, j)) for _ in out_dtypes]
        out_shape = [jax.ShapeDtypeStruct((m, n), dt) for dt in out_dtypes]
    outs = pl.pallas_call(
        functools.partial(_proj_rope_kernel, norm=norm, half=half, layout=layout),
        grid=(n // tn, m // tm),
        in_specs=[
            pl.BlockSpec((tm, k), lambda j, i: (i, 0)),
            pl.BlockSpec((k, tn), lambda j, i: (0, cb0 + j)),
            pl.BlockSpec((1, LANES), lambda j, i: (0, 0)),
            pl.BlockSpec((tm, LANES), lambda j, i: (i % ntab, 0)),
            pl.BlockSpec((tm, LANES), lambda j, i: (i % ntab, 0)),
        ],
        out_specs=out_specs,
        out_shape=out_shape,
        scratch_shapes=[pltpu.VMEM((k, tn), BF16)],
        compiler_params=_cparams("parallel", "arbitrary"),
        name="proj_rope",
    )(a, w, gain, tab_a, tab_b)
    return outs


def _proj_plain_kernel(a_ref, w_ref, *refs, act, kv_rows):
    o_refs, wbf_ref = refs[:-1], refs[-1]
    y = jnp.dot(a_ref[...], _cached_bf16(w_ref, wbf_ref), preferred_element_type=F32)
    if act == "silu":
        y = _silu(y)
    if kv_rows:
        ncol = y.shape[1] // LANES
        for c in range(ncol):
            _store_kv_rows(o_refs, y[:, c * LANES:(c + 1) * LANES], c, ncol)
    else:
        for o_ref in o_refs:
            o_ref[...] = y.astype(o_ref.dtype)


def _proj_plain(a, w, col0, n, out_dtypes, *, act, tm, tn, kv_rows=False):
    m, k = a.shape
    cb0 = col0 // tn
    if kv_rows:
        assert tn == n
        out_specs, out_shape = _kv_rows_outs(m, n, tm)
    else:
        out_specs = [pl.BlockSpec((tm, tn), lambda j, i: (i, j)) for _ in out_dtypes]
        out_shape = [jax.ShapeDtypeStruct((m, n), dt) for dt in out_dtypes]
    return pl.pallas_call(
        functools.partial(_proj_plain_kernel, act=act, kv_rows=kv_rows),
        grid=(n // tn, m // tm),
        in_specs=[
            pl.BlockSpec((tm, k), lambda j, i: (i, 0)),
            pl.BlockSpec((k, tn), lambda j, i: (0, cb0 + j)),
        ],
        out_specs=out_specs,
        out_shape=out_shape,
        scratch_shapes=[pltpu.VMEM((k, tn), BF16)],
        compiler_params=_cparams("parallel", "arbitrary"),
        name="proj_plain",
    )(a, w)


def _proj_forget_kernel(a_ref, w_ref, loglb_ref, log1mlb_ref, lf_ref, k_ref, wbf_ref):
    fr = jnp.dot(a_ref[...], _cached_bf16(w_ref, wbf_ref), preferred_element_type=F32)
    t = jnp.log1p(jnp.exp(-jnp.abs(fr)))
    ls_pos = jnp.minimum(fr, 0.0) - t
    ls_neg = jnp.minimum(-fr, 0.0) - t
    a = loglb_ref[...]
    c = log1mlb_ref[...] + ls_pos
    lf_ref[...] = jnp.maximum(a, c) + jnp.log1p(jnp.exp(-jnp.abs(a - c)))
    k_ref[...] = jnp.exp(log1mlb_ref[...] + ls_neg)


def _proj_forget(a, w, col0, n, log_lb, log_1mlb, *, tm, tn):
    m, k = a.shape
    cb0 = col0 // tn
    return pl.pallas_call(
        _proj_forget_kernel,
        grid=(n // tn, m // tm),
        in_specs=[
            pl.BlockSpec((tm, k), lambda j, i: (i, 0)),
            pl.BlockSpec((k, tn), lambda j, i: (0, cb0 + j)),
            pl.BlockSpec((1, tn), lambda j, i: (0, j)),
            pl.BlockSpec((1, tn), lambda j, i: (0, j)),
        ],
        out_specs=[pl.BlockSpec((tm, tn), lambda j, i: (i, j)) for _ in range(2)],
        out_shape=[jax.ShapeDtypeStruct((m, n), F32) for _ in range(2)],
        scratch_shapes=[pltpu.VMEM((k, tn), BF16)],
        compiler_params=_cparams("parallel", "arbitrary"),
        name="proj_forget",
    )(a, w, log_lb, log_1mlb)


def _mm_res_kernel(a_ref, w_ref, x_ref, gt_ref, o_ref, wbf_ref):
    y = jnp.dot(a_ref[...], _cached_bf16(w_ref, wbf_ref), preferred_element_type=F32)
    o_ref[...] = x_ref[...] + (1.0 + _rows(gt_ref[...], y.shape[0])) * y


def _mm_res(a, w, x, mod, layer, k_gate, *, tm, tn):
    m, k = a.shape
    n = w.shape[1]
    return pl.pallas_call(
        _mm_res_kernel,
        grid=(n // tn, m // tm),
        in_specs=[
            pl.BlockSpec((tm, k), lambda j, i: (i, 0)),
            pl.BlockSpec((k, tn), lambda j, i: (0, j)),
            pl.BlockSpec((tm, tn), lambda j, i: (i, j)),
            mod.spec(layer, k_gate, tn=tn, n_axis=True),
        ],
        out_specs=pl.BlockSpec((tm, tn), lambda j, i: (i, j)),
        out_shape=jax.ShapeDtypeStruct((m, n), F32),
        scratch_shapes=[pltpu.VMEM((k, tn), BF16)],
        compiler_params=_cparams("parallel", "arbitrary"),
        name="mm_res",
    )(a, w, x, mod.arr)


def _sort_key(score):
    bits = pltpu.bitcast(score + 0.0, I32)
    return jnp.where(bits < 0, bits ^ jnp.int32(0x7FFFFFFF), bits)


COUNT_UNROLL = 4
ATTN_TQ = 256


def _attn_prompt_kernel(q_ref, qi_ref, wi_ref, k_ref, v_ref, kiab_ref, o_ref,
                        key_ref, bias_ref, wb_ref, j_ref, s_ref, m_ref, lp_ref, acc_ref, *, topk, idx_bits, kv_unroll):
    tq = q_ref.shape[0]
    tk = PAGE
    nsub = tq // tk
    dh = LANES
    rep = N_HEADS // N_KV_HEADS
    npair = N_IDX_HEADS // 2
    qb = pl.program_id(1)
    nkb = (qb + 1) * nsub
    row = lax.broadcasted_iota(I32, (tq, LANES), 0)
    lane = lax.broadcasted_iota(I32, (tq, LANES), 1)
    qpos = qb * tq + row
    imin = jnp.int32(IMIN)
    nt = (((1,), (1,)), ((), ()))

    wi = wi_ref[...]
    for h in range(N_IDX_HEADS):
        wcol = wi[:, IDX_DIM + h:IDX_DIM + h + 1] * (IDX_DIM ** -0.5)
        wb_ref[h] = jnp.broadcast_to(wcol, (tq, LANES))

    qi_all = qi_ref[...]

    def score_body(kb, c):
        lg = lax.dot_general(qi_all, kiab_ref[kb], nt, preferred_element_type=F32)
        accs = []
        for sb in range(nsub):
            qrows = slice(sb * tk, (sb + 1) * tk)
            acc = jnp.zeros((tk, LANES), F32)
            for c2 in range(npair):
                r = slice((sb * npair + c2) * tk, (sb * npair + c2 + 1) * tk)
                acc = acc + (wb_ref[2 * c2, qrows, :] * jnp.maximum(lg[r, 0:tk], 0.0)
                             + wb_ref[2 * c2 + 1, qrows, :] * jnp.maximum(lg[r, tk:2 * tk], 0.0))
            accs.append(acc)
        acc = accs[0] if nsub == 1 else jnp.concatenate(accs, axis=0)
        key_ref[kb] = jnp.where(kb * tk + lane <= qpos, _sort_key(acc), imin)
        return c

    lax.fori_loop(0, nkb, score_body, 0)
    for u in range(COUNT_UNROLL - 1):
        key_ref[nkb + u] = jnp.full((tq, LANES), IMIN, I32)

    sub_rows = [slice(sb * tk, (sb + 1) * tk) for sb in range(nsub)]
    lane_k = lax.broadcasted_iota(I32, (tk, LANES), 1)
    zeros_col = tuple(jnp.zeros((tk, 1), I32) for _ in range(nsub))

    def count(pred):
        n_it = (nkb + COUNT_UNROLL - 1) // COUNT_UNROLL
        parts = []
        for sb in range(nsub):
            def body(it, part, sb=sb):
                w = [jnp.where(pred(it * COUNT_UNROLL + u, key_ref[it * COUNT_UNROLL + u, sub_rows[sb], :], sb), 1.0, 0.0)
                     for u in range(COUNT_UNROLL)]
                while len(w) > 1:
                    w = [w[i] + w[i + 1] for i in range(0, len(w) - 1, 2)] + ([w[-1]] if len(w) % 2 else [])
                return part + w[0]

            parts.append(lax.fori_loop(0, n_it, body, jnp.zeros((tk, LANES), F32)))
        return [jnp.sum(part, axis=1, keepdims=True) for part in parts]

    def bit_body(i, tus):
        bit = jnp.left_shift(jnp.int32(1), 31 - i)
        cus = [tu | bit for tu in tus]
        cands = [cu ^ imin for cu in cus]
        cnts = count(lambda kb, k, sb: k >= cands[sb])
        return tuple(jnp.where(cnt >= topk, cu, tu) for cnt, cu, tu in zip(cnts, cus, tus))

    thrs = [tu ^ imin for tu in lax.fori_loop(0, 32, bit_body, zeros_col)]
    cnt_gt = count(lambda kb, k, sb: k > thrs[sb])
    cnt_ge = count(lambda kb, k, sb: k >= thrs[sb])
    needs = [topk - c for c in cnt_gt]
    ties = [jnp.logical_and(c > topk, thr > imin) for c, thr in zip(cnt_ge, thrs)]
    any_tie = functools.reduce(jnp.maximum, [jnp.max(jnp.where(t, 1.0, 0.0)) for t in ties])

    j_ref[...] = jnp.full(j_ref.shape, 2 ** idx_bits, I32)

    @pl.when(any_tie > 0.0)
    def _():
        def jbit_body(i, jvs):
            bit = jnp.left_shift(jnp.int32(1), idx_bits - 1 - i)
            cands = [jv | bit for jv in jvs]
            cs = count(lambda kb, k, sb: jnp.logical_and(k == thrs[sb], kb * tk + lane_k < cands[sb]))
            return tuple(jnp.where(c < need, cand, jv) for c, need, cand, jv in zip(cs, needs, cands, jvs))
        jvs = lax.fori_loop(0, idx_bits, jbit_body, zeros_col)
        for sb in range(nsub):
            j_ref[sub_rows[sb], :] = jnp.broadcast_to(jvs[sb], (tk, LANES))

    def bias_body(kb, c):
        for sb in range(nsub):
            k = key_ref[kb, sub_rows[sb], :]
            thr = thrs[sb]
            sel = jnp.logical_or(k > thr, jnp.logical_and(k == thr, kb * tk + lane_k <= j_ref[sub_rows[sb], :]))
            sel = jnp.logical_and(sel, k > imin)
            bias_ref[kb, sub_rows[sb], :] = jnp.where(sel, 0.0, NEG)
        return c

    lax.fori_loop(0, nkb, bias_body, 0)

    for u in range(kv_unroll - 1):
        bias_ref[nkb + u] = jnp.full((tq, LANES), NEG, F32)
    n_it = (nkb + kv_unroll - 1) // kv_unroll
    span = kv_unroll * tk
    c2 = (dh ** -0.5) * 1.4426950408889634
    for g in range(N_KV_HEADS):
        cols = slice(g * dh, (g + 1) * dh)
        qg = jnp.concatenate([q_ref[:, (g * rep + j) * dh:(g * rep + j + 1) * dh] for j in range(rep)], axis=0)
        m_ref[...] = jnp.full(m_ref.shape, NEG, F32)

        def pass1(it, c, qg=qg, cols=cols):
            keys = pl.ds(pl.multiple_of(it * span, span), span)
            s = lax.dot_general(qg, k_ref[keys, cols], nt, preferred_element_type=F32)
            b = jnp.concatenate([bias_ref[it * kv_unroll + u] for u in range(kv_unroll)], axis=1)
            for j in range(rep):
                rows = slice(j * tq, (j + 1) * tq)
                sj = s[rows] + b
                s_ref[it, rows, :] = sj
                t = sj[:, 0:LANES]
                for u in range(1, kv_unroll):
                    t = jnp.maximum(t, sj[:, u * LANES:(u + 1) * LANES])
                m_ref[rows, :] = jnp.maximum(m_ref[rows, :], t)
            return c

        lax.fori_loop(0, n_it, pass1, 0)
        mc = jnp.max(m_ref[...], axis=1, keepdims=True) * c2
        lp_ref[...] = jnp.zeros_like(lp_ref)
        acc_ref[...] = jnp.zeros_like(acc_ref)

        def pass2(it, c, cols=cols, mc=mc):
            keys = pl.ds(pl.multiple_of(it * span, span), span)
            ps = []
            for j in range(rep):
                rows = slice(j * tq, (j + 1) * tq)
                p = jnp.exp2(s_ref[it, rows, :] * c2 - mc[rows])
                t = p[:, 0:LANES]
                for u in range(1, kv_unroll):
                    t = t + p[:, u * LANES:(u + 1) * LANES]
                lp_ref[rows, :] += t
                ps.append(p.astype(BF16))
            acc_ref[...] += jnp.dot(jnp.concatenate(ps, axis=0), v_ref[keys, cols], preferred_element_type=F32)
            return c

        lax.fori_loop(0, n_it, pass2, 0)
        o = acc_ref[...] / jnp.sum(lp_ref[...], axis=1, keepdims=True)
        for j in range(rep):
            h = g * rep + j
            o_ref[:, h * dh:(h + 1) * dh] = o[j * tq:(j + 1) * tq].astype(o_ref.dtype)


def _attn_prompt(q, qi_stack, kiwi_f32, k_bf, v_bf, kiab, batch, seq):
    m = q.shape[0]
    tk = PAGE
    tq = min(ATTN_TQ, seq)
    nq, nk = seq // tq, seq // tk
    topk = min(INDEX_TOPK, seq // 4)
    kvw = k_bf.shape[1]
    npair = N_IDX_HEADS // 2
    rep = N_HEADS // N_KV_HEADS
    kv_unroll = next(u for u in (8, 4, 2, 1) if nk % u == 0)
    kv_spec = pl.BlockSpec((None, seq, kvw), lambda b, i: (b, 0, 0), pipeline_mode=pl.Buffered(1))
    ki_spec = pl.BlockSpec((None, nk, 2 * tk, LANES), lambda b, i: (b, 0, 0, 0), pipeline_mode=pl.Buffered(1))
    return pl.pallas_call(
        functools.partial(_attn_prompt_kernel, topk=topk, idx_bits=max(1, (seq - 1).bit_length()), kv_unroll=kv_unroll),
        grid=(batch, nq),
        in_specs=[
            pl.BlockSpec((tq, q.shape[1]), lambda b, i: (b * nq + i, 0)),
            pl.BlockSpec((npair * tq, LANES), lambda b, i: (b * nq + i, 0)),
            pl.BlockSpec((tq, LANES), lambda b, i: (b * nq + i, 0)),
            kv_spec, kv_spec, ki_spec,
        ],
        out_specs=pl.BlockSpec((tq, q.shape[1]), lambda b, i: (b * nq + i, 0)),
        out_shape=jax.ShapeDtypeStruct((m, q.shape[1]), BF16),
        scratch_shapes=[
            pltpu.VMEM((nk + COUNT_UNROLL - 1, tq, LANES), I32),
            pltpu.VMEM((nk + kv_unroll - 1, tq, LANES), F32),
            pltpu.VMEM((N_IDX_HEADS, tq, LANES), F32),
            pltpu.VMEM((tq, LANES), I32),
            pltpu.VMEM((nk // kv_unroll, rep * tq, kv_unroll * tk), F32),
            pltpu.VMEM((rep * tq, LANES), F32),
            pltpu.VMEM((rep * tq, LANES), F32),
            pltpu.VMEM((rep * tq, LANES), F32),
        ],
        compiler_params=_cparams("parallel", "arbitrary"),
        name="attn_prompt",
    )(q, qi_stack, kiwi_f32, k_bf.reshape(batch, seq, kvw), v_bf.reshape(batch, seq, kvw),
      kiab.reshape(batch, nk, 2 * tk, LANES))


def _idx_sample_kernel(pt_ref, qi_ref, a_ref, kinew_ref, *refs, n_pages, n_tok):
    page_refs = refs[:n_pages]
    o_ref = refs[n_pages]
    a = a_ref[...]
    rows = a.shape[0]
    width = (n_pages + 1) * PAGE
    kit = jnp.concatenate([page_refs[p][...].astype(BF16) for p in range(n_pages)] + [kinew_ref[...]], axis=1)
    lg = jnp.dot(qi_ref[...], kit, preferred_element_type=F32)
    sc = jnp.dot(a, jnp.maximum(lg, 0.0), precision=lax.Precision.HIGHEST, preferred_element_type=F32)
    idx = lax.broadcasted_iota(I32, (rows, width), 1)
    row = lax.broadcasted_iota(I32, (rows, width), 0)
    adm = idx - n_pages * PAGE <= row % n_tok
    o_ref[...] = jnp.where(adm, _sort_key(sc), jnp.int32(IMIN))


def _idx_sample(page_table, qi_ht, amat, ki_new_t, pool_kidx_t, page0):
    bs, n_pages = page_table.shape
    n_tok = qi_ht.shape[1] // N_IDX_HEADS
    rows = amat.shape[1]
    width = (n_pages + 1) * LANES
    page_specs = [pl.BlockSpec((None, IDX_DIM, PAGE), lambda b, pt, p=p: (page0 + pt[b, p], 0, 0))
                  for p in range(n_pages)]
    grid_spec = pltpu.PrefetchScalarGridSpec(
        num_scalar_prefetch=1,
        grid=(bs,),
        in_specs=[
            pl.BlockSpec((None,) + qi_ht.shape[1:], lambda b, pt: (b, 0, 0)),
            pl.BlockSpec((None,) + amat.shape[1:], lambda b, pt: (b, 0, 0)),
            pl.BlockSpec((None,) + ki_new_t.shape[1:], lambda b, pt: (b, 0, 0)),
        ] + page_specs,
        out_specs=pl.BlockSpec((None, rows, width), lambda b, pt: (b, 0, 0)),
    )
    return pl.pallas_call(
        functools.partial(_idx_sample_kernel, n_pages=n_pages, n_tok=n_tok),
        grid_spec=grid_spec,
        out_shape=jax.ShapeDtypeStruct((bs, rows, width), I32),
        compiler_params=_cparams("arbitrary"),
        name="idx_sample",
    )(page_table, qi_ht, amat, ki_new_t, *([pool_kidx_t] * n_pages))


def _topk_bias_kernel(key_ref, bias_ref, *, topk, idx_bits):
    key = key_ref[...]
    rows, width = key.shape
    imin = jnp.int32(IMIN)
    idx = lax.broadcasted_iota(I32, (rows, width), 1)

    def count(pred):
        return jnp.sum(jnp.where(pred, 1.0, 0.0), axis=1, keepdims=True)

    def bit_body(i, tu):
        cu = tu | jnp.left_shift(jnp.int32(1), 31 - i)
        cnt = count(key >= (cu ^ imin))
        return jnp.where(cnt >= topk, cu, tu)

    thr = lax.fori_loop(0, 32, bit_body, jnp.zeros((rows, 1), I32)) ^ imin
    need = topk - count(key > thr)
    eq = key == thr

    def jbit_body(i, jv):
        cand = jv | jnp.left_shift(jnp.int32(1), idx_bits - 1 - i)
        c = count(jnp.logical_and(eq, idx < cand))
        return jnp.where(c < need, cand, jv)

    jv = lax.fori_loop(0, idx_bits, jbit_body, jnp.zeros((rows, 1), I32))
    sel = jnp.logical_or(key > thr, jnp.logical_and(eq, idx <= jv))
    sel = jnp.logical_and(sel, key > imin)
    bias_ref[...] = jnp.where(sel, 0.0, NEG)


def _topk_bias(keys2d, topk, tr):
    rows, width = keys2d.shape
    return pl.pallas_call(
        functools.partial(_topk_bias_kernel, topk=topk, idx_bits=max(1, (width - 1).bit_length())),
        grid=(rows // tr,),
        in_specs=[pl.BlockSpec((tr, width), lambda i: (i, 0))],
        out_specs=pl.BlockSpec((tr, width), lambda i: (i, 0)),
        out_shape=jax.ShapeDtypeStruct((rows, width), F32),
        compiler_params=_cparams("parallel"),
        name="topk_bias",
    )(keys2d)


def _attn_sample_kernel(pt_ref, q_ref, bias_ref, knew_ref, vnew_ref, *refs, n_pages):
    k_refs = refs[:n_pages]
    v_refs = refs[n_pages:2 * n_pages]
    o_ref = refs[2 * n_pages]
    rows, dh = q_ref.shape
    rpg = rows // N_KV_HEADS
    bias8 = bias_ref[...]
    bias = jnp.concatenate([bias8] * (rpg // bias8.shape[0]), axis=0)
    pad = PAGE - knew_ref.shape[0]
    zpad = jnp.zeros((pad, dh), BF16)

    for g in range(N_KV_HEADS):
        qg = q_ref[g * rpg:(g + 1) * rpg, :]
        cols = slice(g * dh, (g + 1) * dh)
        kcat = jnp.concatenate([k_refs[p][pl.ds(g, PAGE, stride=N_KV_HEADS), :].astype(BF16) for p in range(n_pages)]
                               + [knew_ref[:, cols], zpad], axis=0)
        vcat = jnp.concatenate([v_refs[p][pl.ds(g, PAGE, stride=N_KV_HEADS), :].astype(BF16) for p in range(n_pages)]
                               + [vnew_ref[:, cols], zpad], axis=0)
        s = lax.dot_general(qg, kcat, (((1,), (1,)), ((), ())), preferred_element_type=F32) * (dh ** -0.5) + bias
        m = jnp.max(s, axis=1, keepdims=True)
        pexp = jnp.exp(s - m)
        l = jnp.sum(pexp, axis=1, keepdims=True)
        o = jnp.dot(pexp.astype(BF16), vcat, preferred_element_type=F32)
        o_ref[g * rpg:(g + 1) * rpg, :] = (o / l).astype(o_ref.dtype)


def _attn_sample(page_table, q64, bias, k_new, v_new, pool_k, pool_v, page0):
    bs, n_pages = page_table.shape
    pspecs = [pl.BlockSpec((None,) + pool_k.shape[1:], lambda b, pt, p=p: (page0 + pt[b, p], 0, 0))
              for p in range(n_pages)]
    npg = len(pspecs)
    grid_spec = pltpu.PrefetchScalarGridSpec(
        num_scalar_prefetch=1,
        grid=(bs,),
        in_specs=[
            pl.BlockSpec((None,) + q64.shape[1:], lambda b, pt: (b, 0, 0)),
            pl.BlockSpec((None,) + bias.shape[1:], lambda b, pt: (b, 0, 0)),
            pl.BlockSpec((None,) + k_new.shape[1:], lambda b, pt: (b, 0, 0)),
            pl.BlockSpec((None,) + v_new.shape[1:], lambda b, pt: (b, 0, 0)),
        ] + pspecs + pspecs,
        out_specs=pl.BlockSpec((None,) + q64.shape[1:], lambda b, pt: (b, 0, 0)),
    )
    return pl.pallas_call(
        functools.partial(_attn_sample_kernel, n_pages=n_pages),
        grid_spec=grid_spec,
        out_shape=jax.ShapeDtypeStruct(q64.shape, BF16),
        compiler_params=_cparams("arbitrary"),
        name="attn_sample",
    )(page_table, q64, bias, k_new, v_new, *([pool_k] * npg), *([pool_v] * npg))


def _gla_out(o, sg, gain):
    return (o * lax.rsqrt(jnp.mean(o * o, axis=-1, keepdims=True) + EPS) * gain * sg)


GLA_CHUNK = 16


GLA_HEADS = 4


def _gla_prompt_kernel(q_ref, k_ref, v_ref, g_ref, sg_ref, gain_ref, y_ref, s_ref, st_ref, bb_ref, *, blk):
    ci = pl.program_id(2)
    c = GLA_CHUNK

    @pl.when(ci == 0)
    def _():
        st_ref[...] = jnp.zeros_like(st_ref)

    tri = (lax.broadcasted_iota(I32, (blk, blk), 0) >= lax.broadcasted_iota(I32, (blk, blk), 1)).astype(F32)
    ones = jnp.ones((LANES, LANES), BF16)
    nsub = c // SUBLANES
    nch = blk // c
    gain = gain_ref[...]
    r8 = lax.broadcasted_iota(I32, (SUBLANES, LANES), 0)
    mask_bias = [jnp.where(r8 >= j, 0.0, NEG) for j in range(SUBLANES)]
    log2e = 1.4426950408889634

    nhb = st_ref.shape[0]
    heads = [slice(hh * LANES, (hh + 1) * LANES) for hh in range(nhb)]

    for bi in range(q_ref.shape[0] // blk):
        rows = slice(bi * blk, (bi + 1) * blk)
        bb_ref[rows, :] = jnp.dot(tri, g_ref[rows, :] * log2e, precision=lax.Precision.HIGHEST,
                                  preferred_element_type=F32)

    def body(bi, carry):
        r0 = pl.multiple_of(bi * blk, blk)
        bb = bb_ref[pl.ds(r0, blk), :]

        qs, vs, bs, xs = {}, {}, {}, []
        for hh in range(nhb):
            for ch in range(nch):
                lo = ch * c
                rows = pl.ds(r0 + lo, c)
                q, k, v = q_ref[rows, heads[hh]], k_ref[rows, heads[hh]], v_ref[rows, heads[hh]]
                bh = bb[:, heads[hh]]
                b = bh[lo:lo + c] - bh[lo - 1:lo] if ch > 0 else bh[lo:lo + c]
                qs[hh, ch], vs[hh, ch], bs[hh, ch] = q, v, (b, k)
                for s in range(c):
                    i0 = s // SUBLANES
                    qk = q[i0 * SUBLANES:] * k[s:s + 1]
                    diff = b[i0 * SUBLANES:] - b[s:s + 1]
                    for i in range(i0, nsub):
                        d = diff[(i - i0) * SUBLANES:(i - i0 + 1) * SUBLANES]
                        if i == i0:
                            d = d + mask_bias[s % SUBLANES]
                        xs.append(qk[(i - i0) * SUBLANES:(i - i0 + 1) * SUBLANES] * jnp.exp2(d))
        a_all = jnp.dot(jnp.concatenate(xs, axis=0).astype(BF16), ones, preferred_element_type=F32)
        upds, decs, qes = {}, {}, {}
        for hh in range(nhb):
            for ch in range(nch):
                b, k = bs[hh, ch]
                b_end = b[c - 1:c]
                kd = (k * jnp.exp2(b_end - b)).astype(BF16)
                upds[hh, ch] = lax.dot_general(vs[hh, ch].astype(BF16), kd, (((0,), (0,)), ((), ())),
                                               preferred_element_type=F32)
                decs[hh, ch] = jnp.exp2(b_end)
                qes[hh, ch] = (qs[hh, ch] * jnp.exp2(b)).astype(BF16)

        off = 0
        for hh in range(nhb):
            st = st_ref[hh]
            for ch in range(nch):
                o = lax.dot_general(qes[hh, ch], st.astype(BF16), (((1,), (1,)), ((), ())), preferred_element_type=F32)
                st = st * decs[hh, ch] + upds[hh, ch]
                o_parts = [o[i * SUBLANES:(i + 1) * SUBLANES] for i in range(nsub)]
                v = vs[hh, ch]
                for s in range(c):
                    for i in range(s // SUBLANES, nsub):
                        o_parts[i] = o_parts[i] + a_all[off:off + SUBLANES] * v[s:s + 1]
                        off += SUBLANES
                rows = pl.ds(r0 + ch * c, c)
                y_ref[rows, heads[hh]] = _gla_out(jnp.concatenate(o_parts, axis=0), sg_ref[rows, heads[hh]],
                                                  gain[:, heads[hh]]).astype(y_ref.dtype)
            st_ref[hh] = st
        return carry

    lax.fori_loop(0, q_ref.shape[0] // blk, body, 0)

    @pl.when(ci == pl.num_programs(2) - 1)
    def _():
        for hh in range(nhb):
            s_ref[hh] = st_ref[hh].T


def _gla_prompt(qs, kk, vv, lf, sg, gain, batch, seq, tb):
    m, d = qs.shape
    nh = d // LANES
    nc = seq // tb
    nhb = GLA_HEADS
    tok = pl.BlockSpec((tb, nhb * LANES), lambda b, h, c: (b * nc + c, h))
    return pl.pallas_call(
        functools.partial(_gla_prompt_kernel, blk=min(LANES, tb)),
        grid=(batch, nh // nhb, nc),
        in_specs=[tok, tok, tok, tok, tok, pl.BlockSpec((1, nhb * LANES), lambda b, h, c: (0, h))],
        out_specs=[tok, pl.BlockSpec((None, nhb, LANES, LANES), lambda b, h, c: (b, h, 0, 0))],
        out_shape=[jax.ShapeDtypeStruct((m, d), BF16), jax.ShapeDtypeStruct((batch, nh, LANES, LANES), F32)],
        scratch_shapes=[pltpu.VMEM((nhb, LANES, LANES), F32), pltpu.VMEM((tb, nhb * LANES), F32)],
        compiler_params=_cparams("parallel", "parallel", "arbitrary"),
        name="gla_prompt",
    )(qs, kk, vv, lf, sg, gain)


def _gla_sample_kernel(q_ref, k_ref, v_ref, g_ref, sg_ref, gain_ref, s0_ref, y_ref, s_ref, *, n_tok):
    nh = s0_ref.shape[0]
    tp = q_ref.shape[0]
    log2e = 1.4426950408889634
    tri = (lax.broadcasted_iota(I32, (tp, tp), 0) >= lax.broadcasted_iota(I32, (tp, tp), 1)).astype(F32)
    b_all = jnp.dot(tri, g_ref[...] * log2e, precision=lax.Precision.HIGHEST, preferred_element_type=F32)
    b_end_all = b_all[tp - 1:tp]
    r8 = lax.broadcasted_iota(I32, (tp, LANES), 0)
    mask_bias = [jnp.where(r8 >= j, 0.0, NEG) for j in range(n_tok)]
    row0 = lax.broadcasted_iota(I32, b_all.shape, 0) == 0
    dec_cols = lax.dot_general(jnp.where(row0, jnp.exp2(b_end_all), 0.0), jnp.ones((tp, LANES), F32),
                               (((0,), (0,)), ((), ())), precision=lax.Precision.HIGHEST,
                               preferred_element_type=F32)
    xs = []
    for h in range(nh):
        c = slice(h * LANES, (h + 1) * LANES)
        q, k, b = q_ref[:, c], k_ref[:, c], b_all[:, c]
        for s in range(n_tok):
            xs.append(q * k[s:s + 1] * jnp.exp2(b - b[s:s + 1] + mask_bias[s]))
    a_all = jnp.dot(jnp.concatenate(xs, axis=0).astype(BF16), jnp.ones((LANES, LANES), BF16),
                    preferred_element_type=F32)
    for h in range(nh):
        c = slice(h * LANES, (h + 1) * LANES)
        q, k, v, b = q_ref[:, c], k_ref[:, c], v_ref[:, c], b_all[:, c]
        st0 = s0_ref[h]
        o = jnp.dot((q * jnp.exp2(b)).astype(BF16), st0.astype(BF16), preferred_element_type=F32)
        for s in range(n_tok):
            o = o + a_all[(h * n_tok + s) * tp:(h * n_tok + s + 1) * tp] * v[s:s + 1]
        kd = (k * jnp.exp2(b_end_all[:, c] - b)).astype(BF16)
        upd = lax.dot_general(kd, v.astype(BF16), (((0,), (0,)), ((), ())), preferred_element_type=F32)
        s_ref[h] = dec_cols[h * LANES:(h + 1) * LANES] * st0 + upd
        y_ref[:, c] = _gla_out(o, sg_ref[:, c], gain_ref[:, c]).astype(y_ref.dtype)


def _gla_sample(qs, kk, vv, lf, sg, gain, state_pool, seq0, n_tok):
    bs, tp, d = qs.shape
    nh = state_pool.shape[1]
    tok = pl.BlockSpec((None, tp, d), lambda b: (b, 0, 0))
    st_in = pl.BlockSpec((None, nh, LANES, LANES), lambda b: (seq0 + b, 0, 0, 0))
    st = pl.BlockSpec((None, nh, LANES, LANES), lambda b: (b, 0, 0, 0))
    return pl.pallas_call(
        functools.partial(_gla_sample_kernel, n_tok=n_tok),
        grid=(bs,),
        in_specs=[tok, tok, tok, tok, tok, pl.BlockSpec((1, d), lambda b: (0, 0)), st_in],
        out_specs=[tok, st],
        out_shape=[jax.ShapeDtypeStruct((bs, tp, d), BF16), jax.ShapeDtypeStruct((bs, nh, LANES, LANES), F32)],
        compiler_params=_cparams("parallel"),
        name="gla_sample",
    )(qs, kk, vv, lf, sg, gain, state_pool)


def _rope_tables(pos, d):
    inv = ROPE_THETA ** (-jnp.arange(0, d, 2, dtype=F32) / d)
    ang = pos.astype(F32)[:, None] * inv[None, :]
    cos, sin = jnp.cos(ang), jnp.sin(ang)
    reps = LANES // d
    a = jnp.tile(jnp.concatenate([cos, cos], axis=1), (1, reps))
    b = jnp.tile(jnp.concatenate([-sin, sin], axis=1), (1, reps))
    return a, b


def _kiwi_tables(pos):
    a64, b64 = _rope_tables(pos, IDX_DIM)
    p = pos.shape[0]
    lane = jnp.arange(IDX_DIM)
    wi_scale = jnp.where(lane < N_IDX_HEADS, N_IDX_HEADS ** -0.5, 0.0).astype(F32)
    a = jnp.concatenate([a64[:, :IDX_DIM], jnp.broadcast_to(wi_scale, (p, IDX_DIM))], axis=1)
    b = jnp.concatenate([b64[:, :IDX_DIM], jnp.zeros((p, IDX_DIM), F32)], axis=1)
    return a, b


def _attn_projections(h, w_in, w_kiwi, q_gain, k_gain, pos, tm, stack_qi):
    dh = LANES
    qw = N_HEADS * dh
    kvw = N_KV_HEADS * dh
    qiw = N_IDX_HEADS * IDX_DIM
    a128, b128 = _rope_tables(pos, dh)
    a64, b64 = _rope_tables(pos, IDX_DIM)
    akw, bkw = _kiwi_tables(pos)
    tn = 512
    (q,) = _proj_rope(h, w_in, 0, qw, q_gain, a128, b128, [BF16], norm=True, half=dh // 2, tm=tm, tn=tn)
    k32, k16 = _proj_rope(h, w_in, qw, kvw, k_gain, a128, b128, [F32, BF16], norm=True, half=dh // 2, tm=tm, tn=tn,
                          layout="kv_rows" if stack_qi else "cols")
    tm_plain = 2 * tm if h.shape[0] % (2 * tm) == 0 else tm
    v32, v16 = _proj_plain(h, w_in, qw + kvw, kvw, [F32, BF16], act=None, tm=tm_plain, tn=tn, kv_rows=stack_qi)
    if stack_qi:
        (qi,) = _proj_rope(h, w_in, qw + 2 * kvw, qiw, q_gain, a64, b64, [BF16], norm=False, half=IDX_DIM // 2,
                           tm=tm, tn=qiw, layout="stack")
    else:
        (qi,) = _proj_rope(h, w_in, qw + 2 * kvw, qiw, q_gain, a64, b64, [BF16], norm=False, half=IDX_DIM // 2,
                           tm=tm, tn=tn)
    if stack_qi:
        kw32, kiab = _proj_rope(h, w_kiwi, 0, LANES, q_gain, akw, bkw, [F32, BF16], norm=False,
                                half=IDX_DIM // 2, tm=tm, tn=LANES, layout="ki_split")
    else:
        (kw32,), kiab = _proj_rope(h, w_kiwi, 0, LANES, q_gain, akw, bkw, [F32], norm=False,
                                   half=IDX_DIM // 2, tm=tm, tn=LANES), None
    return q, qi, k32, k16, v32, v16, kw32, kiab


def _hgrn_projections(h, w_in, log_lb, log_1mlb, tm):
    d = h.shape[1]
    tn = 512
    (qs,) = _proj_plain(h, w_in, 0, d, [F32], act="silu", tm=tm, tn=tn)
    lf, kk = _proj_forget(h, w_in, d, d, log_lb, log_1mlb, tm=tm, tn=tn)
    (vv,) = _proj_plain(h, w_in, 2 * d, d, [F32], act=None, tm=tm, tn=tn)
    (sg,) = _proj_plain(h, w_in, 3 * d, d, [F32], act="silu", tm=tm, tn=tn)
    return qs, kk, vv, lf, sg


def _stack(arrs):
    return arrs[0][None] if len(arrs) == 1 else jnp.stack(arrs)


def kernel(x_prompt, x_sample, c_prompt, c_sample, cache_k, cache_v, cache_kidx, state_hgrn, page_table, norm_gain, w_ada, b_ada, w_ff_up, w_ff_down, w_attn_in, w_attn_out, q_norm, k_norm, w_rec_in, w_rec_out, rec_out_norm, lb_logits):
    batch, seq, d = x_prompt.shape
    bs, n_tok, _ = x_sample.shape
    depth = w_ada.shape[0]
    n_pages = page_table.shape[1]
    past = n_pages * PAGE
    dh = LANES
    kvw = N_KV_HEADS * dh

    c_all = jnp.concatenate([c_prompt, jnp.zeros((SUBLANES - batch, d), F32), c_sample], axis=0)
    mod = _ada_mod(c_all, w_ada, b_ada)
    tm_p = min(1024, seq)
    mod_p = _Mod(mod[:, :batch].reshape(depth, batch, 9, 1, d), False, seq // tm_p)
    mod_s = _Mod(mod[:, SUBLANES:].reshape(depth, bs, 9, d).transpose(0, 2, 1, 3), True, None)

    gains = norm_gain.reshape(depth, 3, 1, d)
    p = jnp.exp(lb_logits - jnp.max(lb_logits, axis=0, keepdims=True))
    p = p / jnp.sum(p, axis=0, keepdims=True)
    lb_all = jnp.cumsum(p, axis=0) - p[0]

    xp = x_prompt.reshape(batch * seq, d)
    xs = x_sample.transpose(1, 0, 2).reshape(n_tok * bs, d)
    pos_p = jnp.arange(seq)
    pos_s = jnp.repeat(past + jnp.arange(n_tok), bs)
    ms = n_tok * bs

    outs_p, outs_s = {}, {}
    for i in range(depth):
        j = i // 2
        xp = _ffn(xp, mod_p, i, 0, gains, w_ff_up, w_ff_down, 0, tm_p, 512)
        xs = _ffn(xs, mod_s, i, 0, gains, w_ff_up, w_ff_down, 0, ms, 512)
        hp = _prenorm(xp, mod_p, i, 1, gains, tm_p)
        hs = _prenorm(xs, mod_s, i, 1, gains, bs)
        if i % 2 == 0:
            w_in = w_attn_in[j]
            col = N_HEADS * dh + 2 * kvw + N_IDX_HEADS * IDX_DIM
            w_kiwi = jnp.pad(w_in[:, col:], ((0, 0), (0, LANES - (w_in.shape[1] - col))))
            qg = q_norm[j].reshape(1, dh)
            kg = k_norm[j].reshape(1, dh)
            q, qi, k32, k16, v32, v16, kw32, kiab = _attn_projections(hp, w_in, w_kiwi, qg, kg, pos_p, tm_p // 2, True)
            o = _attn_prompt(q, qi, kw32, k16, v16, kiab, batch, seq)
            xp = _mm_res(o, w_attn_out[j], xp, mod_p, i, 5, tm=tm_p, tn=512)
            outs_p[i] = (k32, v32, kw32[:, :IDX_DIM])
            q, qi, k32, k16, v32, v16, kw32, _ = _attn_projections(hs, w_in, w_kiwi, qg, kg, pos_s, bs, False)

            def bmajor(a):
                return a.reshape(n_tok, bs, a.shape[1]).transpose(1, 0, 2)

            qi_b = bmajor(qi).reshape(bs, n_tok, N_IDX_HEADS, IDX_DIM)
            qi_ht = qi_b.transpose(0, 2, 1, 3).reshape(bs, N_IDX_HEADS * n_tok, IDX_DIM)
            wi_b = bmajor(kw32)[:, :, IDX_DIM:IDX_DIM + N_IDX_HEADS] * (IDX_DIM ** -0.5)
            amat = (wi_b[:, :, :, None] * jnp.eye(n_tok, dtype=F32)[None, :, None, :]).reshape(bs, n_tok, -1)
            amat = jnp.concatenate([amat, amat], axis=1)
            rows_pad = 2 * n_tok

            def pad_rows(a):
                return jnp.pad(a, ((0, 0), (0, rows_pad - a.shape[1]), (0, 0)))

            ki_new_t = jnp.pad(bmajor(kw32)[:, :, :IDX_DIM].astype(BF16).transpose(0, 2, 1),
                               ((0, 0), (0, 0), (0, PAGE - n_tok)))
            n_pool = cache_k.shape[1]
            pool_kidx_t = cache_kidx.reshape((-1,) + cache_kidx.shape[2:]).transpose(0, 2, 1)
            keys = _idx_sample(page_table, qi_ht, amat, ki_new_t, pool_kidx_t, j * n_pool)
            width = keys.shape[2]
            topk = min(INDEX_TOPK, (past + n_tok) // 4)
            bias = _topk_bias(keys.reshape(bs * rows_pad, width), topk, min(256, bs * rows_pad)).reshape(bs, rows_pad, width)
            q64 = bmajor(q).reshape(bs, n_tok, N_HEADS, dh).transpose(0, 2, 1, 3).reshape(bs, N_HEADS * n_tok, dh)
            o64 = _attn_sample(page_table, q64, bias, pad_rows(bmajor(k16)), pad_rows(bmajor(v16)),
                               cache_k.reshape(-1, PAGE * N_KV_HEADS, dh), cache_v.reshape(-1, PAGE * N_KV_HEADS, dh),
                               j * n_pool)
            o = o64.reshape(bs, N_HEADS, n_tok, dh).transpose(2, 0, 1, 3).reshape(ms, N_HEADS * dh)
            xs = _mm_res(o, w_attn_out[j], xs, mod_s, i, 5, tm=ms, tn=512)
            outs_s[i] = (bmajor(k32), bmajor(v32), bmajor(kw32)[:, :, :IDX_DIM])
        else:
            lb = lb_all[i].reshape(1, d)
            log_lb, log_1mlb = jnp.log(lb), jnp.log1p(-lb)
            gain = rec_out_norm[j].reshape(1, d)
            qs, kk, vv, lf, sg = _hgrn_projections(hp, w_rec_in[j], log_lb, log_1mlb, tm_p)
            y, st = _gla_prompt(qs, kk, vv, lf, sg, gain, batch, seq, min(512, seq))
            xp = _mm_res(y, w_rec_out[j], xp, mod_p, i, 5, tm=tm_p, tn=512)
            outs_p[i] = (st,)
            qs, kk, vv, lf, sg = _hgrn_projections(hs, w_rec_in[j], log_lb, log_1mlb, bs)

            def bpad(a):
                a = a.reshape(n_tok, bs, d).transpose(1, 0, 2)
                return jnp.pad(a, ((0, 0), (0, SUBLANES - n_tok), (0, 0)))

            y, st = _gla_sample(bpad(qs), bpad(kk), bpad(vv), bpad(lf), bpad(sg), gain,
                                state_hgrn.reshape((-1,) + state_hgrn.shape[2:]), j * bs, n_tok)
            y = y[:, :n_tok].transpose(1, 0, 2).reshape(ms, d)
            xs = _mm_res(y, w_rec_out[j], xs, mod_s, i, 5, tm=ms, tn=512)
            outs_s[i] = (st,)
        xp = _ffn(xp, mod_p, i, 2, gains, w_ff_up, w_ff_down, 1, tm_p, 512)
        xs = _ffn(xs, mod_s, i, 2, gains, w_ff_up, w_ff_down, 1, ms, 512)

    attn_layers = [i for i in range(depth) if i % 2 == 0]
    rec_layers = [i for i in range(depth) if i % 2 == 1]
    na, pages_p = len(attn_layers), seq // PAGE
    y_prompt = xp.reshape(batch, seq, d)
    y_sample = xs.reshape(n_tok, bs, d).transpose(1, 0, 2)
    k_prompt = _stack([outs_p[i][0] for i in attn_layers]).reshape(na, batch, pages_p, PAGE, N_KV_HEADS, dh)
    v_prompt = _stack([outs_p[i][1] for i in attn_layers]).reshape(na, batch, pages_p, PAGE, N_KV_HEADS, dh)
    kidx_prompt = _stack([outs_p[i][2] for i in attn_layers]).reshape(na, batch, pages_p, PAGE, IDX_DIM)
    state_prompt = _stack([outs_p[i][0] for i in rec_layers]).astype(state_hgrn.dtype)
    k_sample = _stack([outs_s[i][0] for i in attn_layers]).reshape(na, bs, n_tok, N_KV_HEADS, dh)
    v_sample = _stack([outs_s[i][1] for i in attn_layers]).reshape(na, bs, n_tok, N_KV_HEADS, dh)
    kidx_sample = _stack([outs_s[i][2] for i in attn_layers])
    state_sample = _stack([outs_s[i][0] for i in rec_layers]).astype(state_hgrn.dtype)
    return (y_prompt, y_sample, k_prompt, v_prompt, kidx_prompt, state_prompt, k_sample, v_sample, kidx_sample, state_sample)
```

```python
import functools

import jax
import jax.numpy as jnp
from jax import lax
from jax.experimental import pallas as pl
from jax.experimental.pallas import tpu as pltpu

F32 = jnp.float32
BF16 = jnp.bfloat16
I32 = jnp.int32

LANES = 128
SUBLANES = 8
VMEM_LIMIT = 60 * 1024 * 1024

N_HEADS = 16
N_KV_HEADS = 4
N_IDX_HEADS = 16
IDX_DIM = 64
INDEX_TOPK = 256
ROPE_THETA = 10000.0
REC_CHUNK = 64
EPS = 1e-6
PAGE = 128
NEG = -1e30
IMIN = -2 ** 31


def _cparams(*sem):
    return pltpu.CompilerParams(dimension_semantics=sem, vmem_limit_bytes=VMEM_LIMIT)


def _rows(mod, tm):
    rm = mod.shape[0]
    if rm == 1 or rm == tm:
        return mod
    return jnp.concatenate([mod] * (tm // rm), axis=0)


def _ada_norm(x, gain, shift, scale):
    tm = x.shape[0]
    y = x * lax.rsqrt(jnp.mean(x * x, axis=-1, keepdims=True) + EPS) * gain
    return y * (1.0 + _rows(scale, tm)) + _rows(shift, tm)


def _silu(x):
    return x * jax.nn.sigmoid(x)


def _ada_mod_kernel(c_ref, w_ref, b_ref, o_ref):
    cs = _silu(c_ref[...]).astype(BF16)
    o_ref[...] = jnp.dot(cs, w_ref[...].astype(BF16), preferred_element_type=F32) + b_ref[...]


def _ada_mod(c_all, w_ada, b_ada):
    depth, d, n = w_ada.shape
    m = c_all.shape[0]
    tn = 2048
    return pl.pallas_call(
        _ada_mod_kernel,
        grid=(depth, n // tn),
        in_specs=[
            pl.BlockSpec((m, d), lambda i, j: (0, 0)),
            pl.BlockSpec((None, d, tn), lambda i, j: (i, 0, j)),
            pl.BlockSpec((None, 1, tn), lambda i, j: (i, 0, j)),
        ],
        out_specs=pl.BlockSpec((None, m, tn), lambda i, j: (i, 0, j)),
        out_shape=jax.ShapeDtypeStruct((depth, m, n), F32),
        compiler_params=_cparams("parallel", "parallel"),
        name="ada_mod",
    )(c_all, w_ada, b_ada.reshape(depth, 1, n))


class _Mod:
    def __init__(self, arr, per_row, tiles_per_seq):
        self.arr = arr
        self.per_row = per_row
        self.tps = tiles_per_seq

    def spec(self, layer, k, tn=None, n_axis=False):
        d = self.arr.shape[-1]
        tn = d if tn is None else tn
        if self.per_row:
            rows = self.arr.shape[2]
            if n_axis:
                return pl.BlockSpec((None, None, rows, tn), lambda n, m: (layer, k, 0, n))
            return pl.BlockSpec((None, None, rows, tn), lambda m, *_: (layer, k, 0, 0))
        tps = self.tps
        if n_axis:
            return pl.BlockSpec((None, None, None, 1, tn), lambda n, m: (layer, m // tps, k, 0, n))
        return pl.BlockSpec((None, None, None, 1, tn), lambda m, *_: (layer, m // tps, k, 0, 0))


ROW_CHUNK = 16


def _ffn_kernel(x_ref, sh_ref, sc_ref, gt_ref, gain_ref, wa_ref, wb_ref, wd_ref, o_ref, h_ref):
    j = pl.program_id(1)
    tm = o_ref.shape[0]

    def mod_rows(ref, r0):
        rm = ref.shape[0]
        if ROW_CHUNK % rm == 0:
            return _rows(ref[...], ROW_CHUNK)
        return ref[pl.ds(pl.multiple_of(lax.rem(r0, rm), ROW_CHUNK), ROW_CHUNK), :]

    @pl.when(j == 0)
    def _():
        def chunk(c, carry):
            r0 = pl.multiple_of(c * ROW_CHUNK, ROW_CHUNK)
            rows = pl.ds(r0, ROW_CHUNK)
            h_ref[rows, :] = _ada_norm(x_ref[rows, :], gain_ref[...], mod_rows(sh_ref, r0),
                                       mod_rows(sc_ref, r0)).astype(BF16)
            o_ref[rows, :] = jnp.zeros((ROW_CHUNK, o_ref.shape[1]), F32)
            return carry
        lax.fori_loop(0, tm // ROW_CHUNK, chunk, 0, unroll=4)

    h = h_ref[...]
    tf = wa_ref.shape[1]
    nsplit = 2 if tf % (2 * LANES) == 0 else 1
    gs = []
    for c in range(nsplit):
        cols = slice(c * tf // nsplit, (c + 1) * tf // nsplit)
        a = jnp.dot(h, wa_ref[:, cols].astype(BF16), preferred_element_type=F32)
        b = jnp.dot(h, wb_ref[:, cols].astype(BF16), preferred_element_type=F32)
        gs.append((_silu(a) * b).astype(BF16))
    g = gs[0] if nsplit == 1 else jnp.concatenate(gs, axis=1)
    o_ref[...] += jnp.dot(g, wd_ref[...].astype(BF16), preferred_element_type=F32)

    @pl.when(j == pl.num_programs(1) - 1)
    def _():
        o_ref[...] = x_ref[...] + 0.5 * (1.0 + _rows(gt_ref[...], tm)) * o_ref[...]


def _ffn(x, mod, layer, sub, gain, w_up, w_down, ffn_idx, tm, tf):
    m, d = x.shape
    f = w_down.shape[2]
    nf = f // tf
    return pl.pallas_call(
        _ffn_kernel,
        grid=(m // tm, nf),
        in_specs=[
            pl.BlockSpec((tm, d), lambda i, j: (i, 0), pipeline_mode=pl.Buffered(1)),
            mod.spec(layer, 3 * sub + 0),
            mod.spec(layer, 3 * sub + 1),
            mod.spec(layer, 3 * sub + 2),
            pl.BlockSpec((None, None, 1, d), lambda i, j: (layer, sub, 0, 0)),
            pl.BlockSpec((None, None, d, tf), lambda i, j: (layer, ffn_idx, 0, j)),
            pl.BlockSpec((None, None, d, tf), lambda i, j: (layer, ffn_idx, 0, j + nf)),
            pl.BlockSpec((None, None, tf, d), lambda i, j: (layer, ffn_idx, j, 0)),
        ],
        out_specs=pl.BlockSpec((tm, d), lambda i, j: (i, 0), pipeline_mode=pl.Buffered(1)),
        out_shape=jax.ShapeDtypeStruct((m, d), F32),
        scratch_shapes=[pltpu.VMEM((tm, d), BF16)],
        compiler_params=_cparams("parallel", "arbitrary"),
        name="ffn",
    )(x, mod.arr, mod.arr, mod.arr, gain, w_up, w_up, w_down)


def _prenorm_kernel(x_ref, sh_ref, sc_ref, gain_ref, h_ref):
    h_ref[...] = _ada_norm(x_ref[...], gain_ref[...], sh_ref[...], sc_ref[...]).astype(BF16)


def _prenorm(x, mod, layer, sub, gain, tm):
    m, d = x.shape
    return pl.pallas_call(
        _prenorm_kernel,
        grid=(m // tm,),
        in_specs=[
            pl.BlockSpec((tm, d), lambda i: (i, 0)),
            mod.spec(layer, 3 * sub + 0),
            mod.spec(layer, 3 * sub + 1),
            pl.BlockSpec((None, None, 1, d), lambda i: (layer, sub, 0, 0)),
        ],
        out_specs=pl.BlockSpec((tm, d), lambda i: (i, 0)),
        out_shape=jax.ShapeDtypeStruct((m, d), BF16),
        compiler_params=_cparams("parallel"),
        name="prenorm",
    )(x, mod.arr, mod.arr, gain)


def _rot_half(y, half):
    if 2 * half == LANES:
        return pltpu.roll(y, half, 1)
    lane = lax.broadcasted_iota(I32, y.shape, 1)
    return jnp.where(lane % (2 * half) < half, pltpu.roll(y, LANES - half, 1), pltpu.roll(y, half, 1))


def _cached_bf16(w_ref, wbf_ref):
    @pl.when(pl.program_id(1) == 0)
    def _():
        wbf_ref[...] = w_ref[...].astype(BF16)
    return wbf_ref[...]


def _proj_rope_kernel(a_ref, w_ref, g_ref, ca_ref, cb_ref, *refs, norm, half, layout):
    o_refs, wbf_ref = refs[:-1], refs[-1]
    wbf = _cached_bf16(w_ref, wbf_ref)
    a = a_ref[...]
    ca = ca_ref[...]
    cb = cb_ref[...]
    tm = a.shape[0]
    ncol = wbf.shape[1] // LANES
    pair = 2 if ncol % 2 == 0 else 1
    ys = [jnp.dot(a, wbf[:, p * pair * LANES:(p + 1) * pair * LANES], preferred_element_type=F32)
          for p in range(ncol // pair)]
    for c in range(ncol):
        yc = ys[c // pair][:, (c % pair) * LANES:(c % pair + 1) * LANES]
        if norm:
            yc = yc * lax.rsqrt(jnp.mean(yc * yc, axis=-1, keepdims=True) + EPS) * g_ref[...]
        r = yc * ca + _rot_half(yc, half) * cb
        if layout == "ki_split":
            lane = lax.broadcasted_iota(I32, r.shape, 1)
            lo = jnp.where(lane < IDX_DIM, r, 0.0)
            hi = pltpu.roll(lo, IDX_DIM, 1)
            o_refs[0][...] = r
            for qb in range(tm // PAGE):
                rows = slice(qb * PAGE, (qb + 1) * PAGE)
                o_refs[1][2 * qb * PAGE:(2 * qb + 1) * PAGE, :] = lo[rows].astype(BF16)
                o_refs[1][(2 * qb + 1) * PAGE:(2 * qb + 2) * PAGE, :] = hi[rows].astype(BF16)
        elif layout == "stack":
            for qb in range(tm // PAGE):
                o_refs[0][(qb * ncol + c) * PAGE:(qb * ncol + c + 1) * PAGE, :] = (
                    r[qb * PAGE:(qb + 1) * PAGE].astype(o_refs[0].dtype))
        elif layout == "kv_rows":
            _store_kv_rows(o_refs, r, c, ncol)
        else:
            for o_ref in o_refs:
                o_ref[:, c * LANES:(c + 1) * LANES] = r.astype(o_ref.dtype)


def _store_kv_rows(o_refs, r, c, ncol):
    o_refs[0][pl.ds(c, r.shape[0], stride=ncol), :] = r
    o_refs[1][:, c * LANES:(c + 1) * LANES] = r.astype(BF16)


def _kv_rows_outs(m, n, tm):
    ncol = n // LANES
    specs = [pl.BlockSpec((tm * ncol, LANES), lambda j, i: (i, 0)), pl.BlockSpec((tm, n), lambda j, i: (i, 0))]
    shapes = [jax.ShapeDtypeStruct((m * ncol, LANES), F32), jax.ShapeDtypeStruct((m, n), BF16)]
    return specs, shapes


def _proj_rope(a, w, col0, n, gain, tab_a, tab_b, out_dtypes, *, norm, half, tm, tn, layout="cols"):
    m, k = a.shape
    ntab = tab_a.shape[0] // tm
    cb0 = col0 // tn
    if layout == "stack":
        assert tn == n and tm % PAGE == 0
        out_specs = [pl.BlockSpec((tm * (n // LANES), LANES), lambda j, i: (i, 0))]
        out_shape = [jax.ShapeDtypeStruct((m * (n // LANES), LANES), out_dtypes[0])]
    elif layout == "ki_split":
        assert tn == n == LANES and tm % PAGE == 0
        out_specs = [pl.BlockSpec((tm, LANES), lambda j, i: (i, 0)), pl.BlockSpec((2 * tm, LANES), lambda j, i: (i, 0))]
        out_shape = [jax.ShapeDtypeStruct((m, LANES), F32), jax.ShapeDtypeStruct((2 * m, LANES), BF16)]
    elif layout == "kv_rows":
        assert tn == n
        out_specs, out_shape = _kv_rows_outs(m, n, tm)
    else:
        out_specs = [pl.BlockSpec((tm, tn), lambda j, i: (i, j)) for _ in out_dtypes]
        out_shape = [jax.ShapeDtypeStruct((m, n), dt) for dt in out_dtypes]
    outs = pl.pallas_call(
        functools.partial(_proj_rope_kernel, norm=norm, half=half, layout=layout),
        grid=(n // tn, m // tm),
        in_specs=[
            pl.BlockSpec((tm, k), lambda j, i: (i, 0)),
            pl.BlockSpec((k, tn), lambda j, i: (0, cb0 + j)),
            pl.BlockSpec((1, LANES), lambda j, i: (0, 0)),
            pl.BlockSpec((tm, LANES), lambda j, i: (i % ntab, 0)),
            pl.BlockSpec((tm, LANES), lambda j, i: (i % ntab, 0)),
        ],
        out_specs=out_specs,
        out_shape=out_shape,
        scratch_shapes=[pltpu.VMEM((k, tn), BF16)],
        compiler_params=_cparams("parallel", "arbitrary"),
        name="proj_rope",
    )(a, w, gain, tab_a, tab_b)
    return outs


def _proj_plain_kernel(a_ref, w_ref, *refs, act, kv_rows):
    o_refs, wbf_ref = refs[:-1], refs[-1]
    y = jnp.dot(a_ref[...], _cached_bf16(w_ref, wbf_ref), preferred_element_type=F32)
    if act == "silu":
        y = _silu(y)
    if kv_rows:
        ncol = y.shape[1] // LANES
        for c in range(ncol):
            _store_kv_rows(o_refs, y[:, c * LANES:(c + 1) * LANES], c, ncol)
    else:
        for o_ref in o_refs:
            o_ref[...] = y.astype(o_ref.dtype)


def _proj_plain(a, w, col0, n, out_dtypes, *, act, tm, tn, kv_rows=False):
    m, k = a.shape
    cb0 = col0 // tn
    if kv_rows:
        assert tn == n
        out_specs, out_shape = _kv_rows_outs(m, n, tm)
    else:
        out_specs = [pl.BlockSpec((tm, tn), lambda j, i: (i, j)) for _ in out_dtypes]
        out_shape = [jax.ShapeDtypeStruct((m, n), dt) for dt in out_dtypes]
    return pl.pallas_call(
        functools.partial(_proj_plain_kernel, act=act, kv_rows=kv_rows),
        grid=(n // tn, m // tm),
        in_specs=[
            pl.BlockSpec((tm, k), lambda j, i: (i, 0)),
            pl.BlockSpec((k, tn), lambda j, i: (0, cb0 + j)),
        ],
        out_specs=out_specs,
        out_shape=out_shape,
        scratch_shapes=[pltpu.VMEM((k, tn), BF16)],
        compiler_params=_cparams("parallel", "arbitrary"),
        name="proj_plain",
    )(a, w)


def _proj_forget_kernel(a_ref, w_ref, loglb_ref, log1mlb_ref, lf_ref, k_ref, wbf_ref):
    fr = jnp.dot(a_ref[...], _cached_bf16(w_ref, wbf_ref), preferred_element_type=F32)
    t = jnp.log1p(jnp.exp(-jnp.abs(fr)))
    ls_pos = jnp.minimum(fr, 0.0) - t
    ls_neg = jnp.minimum(-fr, 0.0) - t
    a = loglb_ref[...]
    c = log1mlb_ref[...] + ls_pos
    lf_ref[...] = jnp.maximum(a, c) + jnp.log1p(jnp.exp(-jnp.abs(a - c)))
    k_ref[...] = jnp.exp(log1mlb_ref[...] + ls_neg)


def _proj_forget(a, w, col0, n, log_lb, log_1mlb, *, tm, tn):
    m, k = a.shape
    cb0 = col0 // tn
    return pl.pallas_call(
        _proj_forget_kernel,
        grid=(n // tn, m // tm),
        in_specs=[
            pl.BlockSpec((tm, k), lambda j, i: (i, 0)),
            pl.BlockSpec((k, tn), lambda j, i: (0, cb0 + j)),
            pl.BlockSpec((1, tn), lambda j, i: (0, j)),
            pl.BlockSpec((1, tn), lambda j, i: (0, j)),
        ],
        out_specs=[pl.BlockSpec((tm, tn), lambda j, i: (i, j)) for _ in range(2)],
        out_shape=[jax.ShapeDtypeStruct((m, n), F32) for _ in range(2)],
        scratch_shapes=[pltpu.VMEM((k, tn), BF16)],
        compiler_params=_cparams("parallel", "arbitrary"),
        name="proj_forget",
    )(a, w, log_lb, log_1mlb)


def _mm_res_kernel(a_ref, w_ref, x_ref, gt_ref, o_ref, wbf_ref):
    y = jnp.dot(a_ref[...], _cached_bf16(w_ref, wbf_ref), preferred_element_type=F32)
    o_ref[...] = x_ref[...] + (1.0 + _rows(gt_ref[...], y.shape[0])) * y


def _mm_res(a, w, x, mod, layer, k_gate, *, tm, tn):
    m, k = a.shape
    n = w.shape[1]
    return pl.pallas_call(
        _mm_res_kernel,
        grid=(n // tn, m // tm),
        in_specs=[
            pl.BlockSpec((tm, k), lambda j, i: (i, 0)),
            pl.BlockSpec((k, tn), lambda j, i: (0, j)),
            pl.BlockSpec((tm, tn), lambda j, i: (i, j)),
            mod.spec(layer, k_gate, tn=tn, n_axis=True),
        ],
        out_specs=pl.BlockSpec((tm, tn), lambda j, i: (i, j)),
        out_shape=jax.ShapeDtypeStruct((m, n), F32),
        scratch_shapes=[pltpu.VMEM((k, tn), BF16)],
        compiler_params=_cparams("parallel", "arbitrary"),
        name="mm_res",
    )(a, w, x, mod.arr)


def _sort_key(score):
    bits = pltpu.bitcast(score + 0.0, I32)
    return jnp.where(bits < 0, bits ^ jnp.int32(0x7FFFFFFF), bits)


COUNT_UNROLL = 4
ATTN_TQ = 256


def _attn_prompt_kernel(q_ref, qi_ref, wi_ref, k_ref, v_ref, kiab_ref, o_ref,
                        key_ref, bias_ref, wb_ref, j_ref, s_ref, m_ref, lp_ref, acc_ref, *, topk, idx_bits, kv_unroll):
    tq = q_ref.shape[0]
    tk = PAGE
    nsub = tq // tk
    dh = LANES
    rep = N_HEADS // N_KV_HEADS
    npair = N_IDX_HEADS // 2
    qb = pl.program_id(1)
    nkb = (qb + 1) * nsub
    row = lax.broadcasted_iota(I32, (tq, LANES), 0)
    lane = lax.broadcasted_iota(I32, (tq, LANES), 1)
    qpos = qb * tq + row
    imin = jnp.int32(IMIN)
    nt = (((1,), (1,)), ((), ()))

    wi = wi_ref[...]
    for h in range(N_IDX_HEADS):
        wcol = wi[:, IDX_DIM + h:IDX_DIM + h + 1] * (IDX_DIM ** -0.5)
        wb_ref[h] = jnp.broadcast_to(wcol, (tq, LANES))

    qi_all = qi_ref[...]

    def score_body(kb, c):
        lg = lax.dot_general(qi_all, kiab_ref[kb], nt, preferred_element_type=F32)
        accs = []
        for sb in range(nsub):
            qrows = slice(sb * tk, (sb + 1) * tk)
            acc = jnp.zeros((tk, LANES), F32)
            for c2 in range(npair):
                r = slice((sb * npair + c2) * tk, (sb * npair + c2 + 1) * tk)
                acc = acc + (wb_ref[2 * c2, qrows, :] * jnp.maximum(lg[r, 0:tk], 0.0)
                             + wb_ref[2 * c2 + 1, qrows, :] * jnp.maximum(lg[r, tk:2 * tk], 0.0))
            accs.append(acc)
        acc = accs[0] if nsub == 1 else jnp.concatenate(accs, axis=0)
        key_ref[kb] = jnp.where(kb * tk + lane <= qpos, _sort_key(acc), imin)
        return c

    lax.fori_loop(0, nkb, score_body, 0)
    for u in range(COUNT_UNROLL - 1):
        key_ref[nkb + u] = jnp.full((tq, LANES), IMIN, I32)

    sub_rows = [slice(sb * tk, (sb + 1) * tk) for sb in range(nsub)]
    lane_k = lax.broadcasted_iota(I32, (tk, LANES), 1)
    zeros_col = tuple(jnp.zeros((tk, 1), I32) for _ in range(nsub))

    def count(pred):
        n_it = (nkb + COUNT_UNROLL - 1) // COUNT_UNROLL
        parts = []
        for sb in range(nsub):
            def body(it, part, sb=sb):
                w = [jnp.where(pred(it * COUNT_UNROLL + u, key_ref[it * COUNT_UNROLL + u, sub_rows[sb], :], sb), 1.0, 0.0)
                     for u in range(COUNT_UNROLL)]
                while len(w) > 1:
                    w = [w[i] + w[i + 1] for i in range(0, len(w) - 1, 2)] + ([w[-1]] if len(w) % 2 else [])
                return part + w[0]

            parts.append(lax.fori_loop(0, n_it, body, jnp.zeros((tk, LANES), F32)))
        return [jnp.sum(part, axis=1, keepdims=True) for part in parts]

    def bit_body(i, tus):
        bit = jnp.left_shift(jnp.int32(1), 31 - i)
        cus = [tu | bit for tu in tus]
        cands = [cu ^ imin for cu in cus]
        cnts = count(lambda kb, k, sb: k >= cands[sb])
        return tuple(jnp.where(cnt >= topk, cu, tu) for cnt, cu, tu in zip(cnts, cus, tus))

    thrs = [tu ^ imin for tu in lax.fori_loop(0, 32, bit_body, zeros_col)]
    cnt_gt = count(lambda kb, k, sb: k > thrs[sb])
    cnt_ge = count(lambda kb, k, sb: k >= thrs[sb])
    needs = [topk - c for c in cnt_gt]
    ties = [jnp.logical_and(c > topk, thr > imin) for c, thr in zip(cnt_ge, thrs)]
    any_tie = functools.reduce(jnp.maximum, [jnp.max(jnp.where(t, 1.0, 0.0)) for t in ties])

    j_ref[...] = jnp.full(j_ref.shape, 2 ** idx_bits, I32)

    @pl.when(any_tie > 0.0)
    def _():
        def jbit_body(i, jvs):
            bit = jnp.left_shift(jnp.int32(1), idx_bits - 1 - i)
            cands = [jv | bit for jv in jvs]
            cs = count(lambda kb, k, sb: jnp.logical_and(k == thrs[sb], kb * tk + lane_k < cands[sb]))
            return tuple(jnp.where(c < need, cand, jv) for c, need, cand, jv in zip(cs, needs, cands, jvs))
        jvs = lax.fori_loop(0, idx_bits, jbit_body, zeros_col)
        for sb in range(nsub):
            j_ref[sub_rows[sb], :] = jnp.broadcast_to(jvs[sb], (tk, LANES))

    def bias_body(kb, c):
        for sb in range(nsub):
            k = key_ref[kb, sub_rows[sb], :]
            thr = thrs[sb]
            sel = jnp.logical_or(k > thr, jnp.logical_and(k == thr, kb * tk + lane_k <= j_ref[sub_rows[sb], :]))
            sel = jnp.logical_and(sel, k > imin)
            bias_ref[kb, sub_rows[sb], :] = jnp.where(sel, 0.0, NEG)
        return c

    lax.fori_loop(0, nkb, bias_body, 0)

    for u in range(kv_unroll - 1):
        bias_ref[nkb + u] = jnp.full((tq, LANES), NEG, F32)
    n_it = (nkb + kv_unroll - 1) // kv_unroll
    span = kv_unroll * tk
    c2 = (dh ** -0.5) * 1.4426950408889634
    for g in range(N_KV_HEADS):
        cols = slice(g * dh, (g + 1) * dh)
        qg = jnp.concatenate([q_ref[:, (g * rep + j) * dh:(g * rep + j + 1) * dh] for j in range(rep)], axis=0)
        m_ref[...] = jnp.full(m_ref.shape, NEG, F32)

        def pass1(it, c, qg=qg, cols=cols):
            keys = pl.ds(pl.multiple_of(it * span, span), span)
            s = lax.dot_general(qg, k_ref[keys, cols], nt, preferred_element_type=F32)
            b = jnp.concatenate([bias_ref[it * kv_unroll + u] for u in range(kv_unroll)], axis=1)
            for j in range(rep):
                rows = slice(j * tq, (j + 1) * tq)
                sj = s[rows] + b
                s_ref[it, rows, :] = sj
                t = sj[:, 0:LANES]
                for u in range(1, kv_unroll):
                    t = jnp.maximum(t, sj[:, u * LANES:(u + 1) * LANES])
                m_ref[rows, :] = jnp.maximum(m_ref[rows, :], t)
            return c

        lax.fori_loop(0, n_it, pass1, 0)
        mc = jnp.max(m_ref[...], axis=1, keepdims=True) * c2
        lp_ref[...] = jnp.zeros_like(lp_ref)
        acc_ref[...] = jnp.zeros_like(acc_ref)

        def pass2(it, c, cols=cols, mc=mc):
            keys = pl.ds(pl.multiple_of(it * span, span), span)
            ps = []
            for j in range(rep):
                rows = slice(j * tq, (j + 1) * tq)
                p = jnp.exp2(s_ref[it, rows, :] * c2 - mc[rows])
                t = p[:, 0:LANES]
                for u in range(1, kv_unroll):
                    t = t + p[:, u * LANES:(u + 1) * LANES]
                lp_ref[rows, :] += t
                ps.append(p.astype(BF16))
            acc_ref[...] += jnp.dot(jnp.concatenate(ps, axis=0), v_ref[keys, cols], preferred_element_type=F32)
            return c

        lax.fori_loop(0, n_it, pass2, 0)
        o = acc_ref[...] / jnp.sum(lp_ref[...], axis=1, keepdims=True)
        for j in range(rep):
            h = g * rep + j
            o_ref[:, h * dh:(h + 1) * dh] = o[j * tq:(j + 1) * tq].astype(o_ref.dtype)


def _attn_prompt(q, qi_stack, kiwi_f32, k_bf, v_bf, kiab, batch, seq):
    m = q.shape[0]
    tk = PAGE
    tq = min(ATTN_TQ, seq)
    nq, nk = seq // tq, seq // tk
    topk = min(INDEX_TOPK, seq // 4)
    kvw = k_bf.shape[1]
    npair = N_IDX_HEADS // 2
    rep = N_HEADS // N_KV_HEADS
    kv_unroll = next(u for u in (8, 4, 2, 1) if nk % u == 0)
    kv_spec = pl.BlockSpec((None, seq, kvw), lambda b, i: (b, 0, 0), pipeline_mode=pl.Buffered(1))
    ki_spec = pl.BlockSpec((None, nk, 2 * tk, LANES), lambda b, i: (b, 0, 0, 0), pipeline_mode=pl.Buffered(1))
    return pl.pallas_call(
        functools.partial(_attn_prompt_kernel, topk=topk, idx_bits=max(1, (seq - 1).bit_length()), kv_unroll=kv_unroll),
        grid=(batch, nq),
        in_specs=[
            pl.BlockSpec((tq, q.shape[1]), lambda b, i: (b * nq + i, 0)),
            pl.BlockSpec((npair * tq, LANES), lambda b, i: (b * nq + i, 0)),
            pl.BlockSpec((tq, LANES), lambda b, i: (b * nq + i, 0)),
            kv_spec, kv_spec, ki_spec,
        ],
        out_specs=pl.BlockSpec((tq, q.shape[1]), lambda b, i: (b * nq + i, 0)),
        out_shape=jax.ShapeDtypeStruct((m, q.shape[1]), BF16),
        scratch_shapes=[
            pltpu.VMEM((nk + COUNT_UNROLL - 1, tq, LANES), I32),
            pltpu.VMEM((nk + kv_unroll - 1, tq, LANES), F32),
            pltpu.VMEM((N_IDX_HEADS, tq, LANES), F32),
            pltpu.VMEM((tq, LANES), I32),
            pltpu.VMEM((nk // kv_unroll, rep * tq, kv_unroll * tk), F32),
            pltpu.VMEM((rep * tq, LANES), F32),
            pltpu.VMEM((rep * tq, LANES), F32),
            pltpu.VMEM((rep * tq, LANES), F32),
        ],
        compiler_params=_cparams("parallel", "arbitrary"),
        name="attn_prompt",
    )(q, qi_stack, kiwi_f32, k_bf.reshape(batch, seq, kvw), v_bf.reshape(batch, seq, kvw),
      kiab.reshape(batch, nk, 2 * tk, LANES))


def _idx_sample_kernel(pt_ref, qi_ref, a_ref, kinew_ref, *refs, n_pages, n_tok):
    page_refs = refs[:n_pages]
    o_ref = refs[n_pages]
    a = a_ref[...]
    rows = a.shape[0]
    width = (n_pages + 1) * PAGE
    kit = jnp.concatenate([page_refs[p][...].astype(BF16) for p in range(n_pages)] + [kinew_ref[...]], axis=1)
    lg = jnp.dot(qi_ref[...], kit, preferred_element_type=F32)
    sc = jnp.dot(a, jnp.maximum(lg, 0.0), precision=lax.Precision.HIGHEST, preferred_element_type=F32)
    idx = lax.broadcasted_iota(I32, (rows, width), 1)
    row = lax.broadcasted_iota(I32, (rows, width), 0)
    adm = idx - n_pages * PAGE <= row % n_tok
    o_ref[...] = jnp.where(adm, _sort_key(sc), jnp.int32(IMIN))


def _idx_sample(page_table, qi_ht, amat, ki_new_t, pool_kidx_t, page0):
    bs, n_pages = page_table.shape
    n_tok = qi_ht.shape[1] // N_IDX_HEADS
    rows = amat.shape[1]
    width = (n_pages + 1) * LANES
    page_specs = [pl.BlockSpec((None, IDX_DIM, PAGE), lambda b, pt, p=p: (page0 + pt[b, p], 0, 0))
                  for p in range(n_pages)]
    grid_spec = pltpu.PrefetchScalarGridSpec(
        num_scalar_prefetch=1,
        grid=(bs,),
        in_specs=[
            pl.BlockSpec((None,) + qi_ht.shape[1:], lambda b, pt: (b, 0, 0)),
            pl.BlockSpec((None,) + amat.shape[1:], lambda b, pt: (b, 0, 0)),
            pl.BlockSpec((None,) + ki_new_t.shape[1:], lambda b, pt: (b, 0, 0)),
        ] + page_specs,
        out_specs=pl.BlockSpec((None, rows, width), lambda b, pt: (b, 0, 0)),
    )
    return pl.pallas_call(
        functools.partial(_idx_sample_kernel, n_pages=n_pages, n_tok=n_tok),
        grid_spec=grid_spec,
        out_shape=jax.ShapeDtypeStruct((bs, rows, width), I32),
        compiler_params=_cparams("arbitrary"),
        name="idx_sample",
    )(page_table, qi_ht, amat, ki_new_t, *([pool_kidx_t] * n_pages))


def _topk_bias_kernel(key_ref, bias_ref, *, topk, idx_bits):
    key = key_ref[...]
    rows, width = key.shape
    imin = jnp.int32(IMIN)
    idx = lax.broadcasted_iota(I32, (rows, width), 1)

    def count(pred):
        return jnp.sum(jnp.where(pred, 1.0, 0.0), axis=1, keepdims=True)

    def bit_body(i, tu):
        cu = tu | jnp.left_shift(jnp.int32(1), 31 - i)
        cnt = count(key >= (cu ^ imin))
        return jnp.where(cnt >= topk, cu, tu)

    thr = lax.fori_loop(0, 32, bit_body, jnp.zeros((rows, 1), I32)) ^ imin
    need = topk - count(key > thr)
    eq = key == thr

    def jbit_body(i, jv):
        cand = jv | jnp.left_shift(jnp.int32(1), idx_bits - 1 - i)
        c = count(jnp.logical_and(eq, idx < cand))
        return jnp.where(c < need, cand, jv)

    jv = lax.fori_loop(0, idx_bits, jbit_body, jnp.zeros((rows, 1), I32))
    sel = jnp.logical_or(key > thr, jnp.logical_and(eq, idx <= jv))
    sel = jnp.logical_and(sel, key > imin)
    bias_ref[...] = jnp.where(sel, 0.0, NEG)


def _topk_bias(keys2d, topk, tr):
    rows, width = keys2d.shape
    return pl.pallas_call(
        functools.partial(_topk_bias_kernel, topk=topk, idx_bits=max(1, (width - 1).bit_length())),
        grid=(rows // tr,),
        in_specs=[pl.BlockSpec((tr, width), lambda i: (i, 0))],
        out_specs=pl.BlockSpec((tr, width), lambda i: (i, 0)),
        out_shape=jax.ShapeDtypeStruct((rows, width), F32),
        compiler_params=_cparams("parallel"),
        name="topk_bias",
    )(keys2d)


def _attn_sample_kernel(pt_ref, q_ref, bias_ref, knew_ref, vnew_ref, *refs, n_pages):
    k_refs = refs[:n_pages]
    v_refs = refs[n_pages:2 * n_pages]
    o_ref = refs[2 * n_pages]
    rows, dh = q_ref.shape
    rpg = rows // N_KV_HEADS
    bias8 = bias_ref[...]
    bias = jnp.concatenate([bias8] * (rpg // bias8.shape[0]), axis=0)
    pad = PAGE - knew_ref.shape[0]
    zpad = jnp.zeros((pad, dh), BF16)

    for g in range(N_KV_HEADS):
        qg = q_ref[g * rpg:(g + 1) * rpg, :]
        cols = slice(g * dh, (g + 1) * dh)
        kcat = jnp.concatenate([k_refs[p][pl.ds(g, PAGE, stride=N_KV_HEADS), :].astype(BF16) for p in range(n_pages)]
                               + [knew_ref[:, cols], zpad], axis=0)
        vcat = jnp.concatenate([v_refs[p][pl.ds(g, PAGE, stride=N_KV_HEADS), :].astype(BF16) for p in range(n_pages)]
                               + [vnew_ref[:, cols], zpad], axis=0)
        s = lax.dot_general(qg, kcat, (((1,), (1,)), ((), ())), preferred_element_type=F32) * (dh ** -0.5) + bias
        m = jnp.max(s, axis=1, keepdims=True)
        pexp = jnp.exp(s - m)
        l = jnp.sum(pexp, axis=1, keepdims=True)
        o = jnp.dot(pexp.astype(BF16), vcat, preferred_element_type=F32)
        o_ref[g * rpg:(g + 1) * rpg, :] = (o / l).astype(o_ref.dtype)


def _attn_sample(page_table, q64, bias, k_new, v_new, pool_k, pool_v, page0):
    bs, n_pages = page_table.shape
    pspecs = [pl.BlockSpec((None,) + pool_k.shape[1:], lambda b, pt, p=p: (page0 + pt[b, p], 0, 0))
              for p in range(n_pages)]
    npg = len(pspecs)
    grid_spec = pltpu.PrefetchScalarGridSpec(
        num_scalar_prefetch=1,
        grid=(bs,),
        in_specs=[
            pl.BlockSpec((None,) + q64.shape[1:], lambda b, pt: (b, 0, 0)),
            pl.BlockSpec((None,) + bias.shape[1:], lambda b, pt: (b, 0, 0)),
            pl.BlockSpec((None,) + k_new.shape[1:], lambda b, pt: (b, 0, 0)),
            pl.BlockSpec((None,) + v_new.shape[1:], lambda b, pt: (b, 0, 0)),
        ] + pspecs + pspecs,
        out_specs=pl.BlockSpec((None,) + q64.shape[1:], lambda b, pt: (b, 0, 0)),
    )
    return pl.pallas_call(
        functools.partial(_attn_sample_kernel, n_pages=n_pages),
        grid_spec=grid_spec,
        out_shape=jax.ShapeDtypeStruct(q64.shape, BF16),
        compiler_params=_cparams("arbitrary"),
        name="attn_sample",
    )(page_table, q64, bias, k_new, v_new, *([pool_k] * npg), *([pool_v] * npg))


def _gla_out(o, sg, gain):
    return (o * lax.rsqrt(jnp.mean(o * o, axis=-1, keepdims=True) + EPS) * gain * sg)


GLA_CHUNK = 16


GLA_HEADS = 4


def _gla_prompt_kernel(q_ref, k_ref, v_ref, g_ref, sg_ref, gain_ref, y_ref, s_ref, st_ref, bb_ref, *, blk):
    ci = pl.program_id(2)
    c = GLA_CHUNK

    @pl.when(ci == 0)
    def _():
        st_ref[...] = jnp.zeros_like(st_ref)

    tri = (lax.broadcasted_iota(I32, (blk, blk), 0) >= lax.broadcasted_iota(I32, (blk, blk), 1)).astype(F32)
    ones = jnp.ones((LANES, LANES), BF16)
    nsub = c // SUBLANES
    nch = blk // c
    gain = gain_ref[...]
    r8 = lax.broadcasted_iota(I32, (SUBLANES, LANES), 0)
    mask_bias = [jnp.where(r8 >= j, 0.0, NEG) for j in range(SUBLANES)]
    log2e = 1.4426950408889634

    nhb = st_ref.shape[0]
    heads = [slice(hh * LANES, (hh + 1) * LANES) for hh in range(nhb)]

    for bi in range(q_ref.shape[0] // blk):
        rows = slice(bi * blk, (bi + 1) * blk)
        bb_ref[rows, :] = jnp.dot(tri, g_ref[rows, :] * log2e, precision=lax.Precision.HIGHEST,
                                  preferred_element_type=F32)

    def body(bi, carry):
        r0 = pl.multiple_of(bi * blk, blk)
        bb = bb_ref[pl.ds(r0, blk), :]

        qs, vs, bs, xs = {}, {}, {}, []
        for hh in range(nhb):
            for ch in range(nch):
                lo = ch * c
                rows = pl.ds(r0 + lo, c)
                q, k, v = q_ref[rows, heads[hh]], k_ref[rows, heads[hh]], v_ref[rows, heads[hh]]
                bh = bb[:, heads[hh]]
                b = bh[lo:lo + c] - bh[lo - 1:lo] if ch > 0 else bh[lo:lo + c]
                qs[hh, ch], vs[hh, ch], bs[hh, ch] = q, v, (b, k)
                for s in range(c):
                    i0 = s // SUBLANES
                    qk = q[i0 * SUBLANES:] * k[s:s + 1]
                    diff = b[i0 * SUBLANES:] - b[s:s + 1]
                    for i in range(i0, nsub):
                        d = diff[(i - i0) * SUBLANES:(i - i0 + 1) * SUBLANES]
                        if i == i0:
                            d = d + mask_bias[s % SUBLANES]
                        xs.append(qk[(i - i0) * SUBLANES:(i - i0 + 1) * SUBLANES] * jnp.exp2(d))
        a_all = jnp.dot(jnp.concatenate(xs, axis=0).astype(BF16), ones, preferred_element_type=F32)
        upds, decs, qes = {}, {}, {}
        for hh in range(nhb):
            for ch in range(nch):
                b, k = bs[hh, ch]
                b_end = b[c - 1:c]
                kd = (k * jnp.exp2(b_end - b)).astype(BF16)
                upds[hh, ch] = lax.dot_general(vs[hh, ch].astype(BF16), kd, (((0,), (0,)), ((), ())),
                                               preferred_element_type=F32)
                decs[hh, ch] = jnp.exp2(b_end)
                qes[hh, ch] = (qs[hh, ch] * jnp.exp2(b)).astype(BF16)

        off = 0
        for hh in range(nhb):
            st = st_ref[hh]
            for ch in range(nch):
                o = lax.dot_general(qes[hh, ch], st.astype(BF16), (((1,), (1,)), ((), ())), preferred_element_type=F32)
                st = st * decs[hh, ch] + upds[hh, ch]
                o_parts = [o[i * SUBLANES:(i + 1) * SUBLANES] for i in range(nsub)]
                v = vs[hh, ch]
                for s in range(c):
                    for i in range(s // SUBLANES, nsub):
                        o_parts[i] = o_parts[i] + a_all[off:off + SUBLANES] * v[s:s + 1]
                        off += SUBLANES
                rows = pl.ds(r0 + ch * c, c)
                y_ref[rows, heads[hh]] = _gla_out(jnp.concatenate(o_parts, axis=0), sg_ref[rows, heads[hh]],
                                                  gain[:, heads[hh]]).astype(y_ref.dtype)
            st_ref[hh] = st
        return carry

    lax.fori_loop(0, q_ref.shape[0] // blk, body, 0)

    @pl.when(ci == pl.num_programs(2) - 1)
    def _():
        for hh in range(nhb):
            s_ref[hh] = st_ref[hh].T


def _gla_prompt(qs, kk, vv, lf, sg, gain, batch, seq, tb):
    m, d = qs.shape
    nh = d // LANES
    nc = seq // tb
    nhb = GLA_HEADS
    tok = pl.BlockSpec((tb, nhb * LANES), lambda b, h, c: (b * nc + c, h))
    return pl.pallas_call(
        functools.partial(_gla_prompt_kernel, blk=min(LANES, tb)),
        grid=(batch, nh // nhb, nc),
        in_specs=[tok, tok, tok, tok, tok, pl.BlockSpec((1, nhb * LANES), lambda b, h, c: (0, h))],
        out_specs=[tok, pl.BlockSpec((None, nhb, LANES, LANES), lambda b, h, c: (b, h, 0, 0))],
        out_shape=[jax.ShapeDtypeStruct((m, d), BF16), jax.ShapeDtypeStruct((batch, nh, LANES, LANES), F32)],
        scratch_shapes=[pltpu.VMEM((nhb, LANES, LANES), F32), pltpu.VMEM((tb, nhb * LANES), F32)],
        compiler_params=_cparams("parallel", "parallel", "arbitrary"),
        name="gla_prompt",
    )(qs, kk, vv, lf, sg, gain)


def _gla_sample_kernel(q_ref, k_ref, v_ref, g_ref, sg_ref, gain_ref, s0_ref, y_ref, s_ref, *, n_tok):
    nh = s0_ref.shape[0]
    tp = q_ref.shape[0]
    log2e = 1.4426950408889634
    tri = (lax.broadcasted_iota(I32, (tp, tp), 0) >= lax.broadcasted_iota(I32, (tp, tp), 1)).astype(F32)
    b_all = jnp.dot(tri, g_ref[...] * log2e, precision=lax.Precision.HIGHEST, preferred_element_type=F32)
    b_end_all = b_all[tp - 1:tp]
    r8 = lax.broadcasted_iota(I32, (tp, LANES), 0)
    mask_bias = [jnp.where(r8 >= j, 0.0, NEG) for j in range(n_tok)]
    row0 = lax.broadcasted_iota(I32, b_all.shape, 0) == 0
    dec_cols = lax.dot_general(jnp.where(row0, jnp.exp2(b_end_all), 0.0), jnp.ones((tp, LANES), F32),
                               (((0,), (0,)), ((), ())), precision=lax.Precision.HIGHEST,
                               preferred_element_type=F32)
    xs = []
    for h in range(nh):
        c = slice(h * LANES, (h + 1) * LANES)
        q, k, b = q_ref[:, c], k_ref[:, c], b_all[:, c]
        for s in range(n_tok):
            xs.append(q * k[s:s + 1] * jnp.exp2(b - b[s:s + 1] + mask_bias[s]))
    a_all = jnp.dot(jnp.concatenate(xs, axis=0).astype(BF16), jnp.ones((LANES, LANES), BF16),
                    preferred_element_type=F32)
    for h in range(nh):
        c = slice(h * LANES, (h + 1) * LANES)
        q, k, v, b = q_ref[:, c], k_ref[:, c], v_ref[:, c], b_all[:, c]
        st0 = s0_ref[h]
        o = jnp.dot((q * jnp.exp2(b)).astype(BF16), st0.astype(BF16), preferred_element_type=F32)
        for s in range(n_tok):
            o = o + a_all[(h * n_tok + s) * tp:(h * n_tok + s + 1) * tp] * v[s:s + 1]
        kd = (k * jnp.exp2(b_end_all[:, c] - b)).astype(BF16)
        upd = lax.dot_general(kd, v.astype(BF16), (((0,), (0,)), ((), ())), preferred_element_type=F32)
        s_ref[h] = dec_cols[h * LANES:(h + 1) * LANES] * st0 + upd
        y_ref[:, c] = _gla_out(o, sg_ref[:, c], gain_ref[:, c]).astype(y_ref.dtype)


def _gla_sample(qs, kk, vv, lf, sg, gain, state_pool, seq0, n_tok):
    bs, tp, d = qs.shape
    nh = state_pool.shape[1]
    tok = pl.BlockSpec((None, tp, d), lambda b: (b, 0, 0))
    st_in = pl.BlockSpec((None, nh, LANES, LANES), lambda b: (seq0 + b, 0, 0, 0))
    st = pl.BlockSpec((None, nh, LANES, LANES), lambda b: (b, 0, 0, 0))
    return pl.pallas_call(
        functools.partial(_gla_sample_kernel, n_tok=n_tok),
        grid=(bs,),
        in_specs=[tok, tok, tok, tok, tok, pl.BlockSpec((1, d), lambda b: (0, 0)), st_in],
        out_specs=[tok, st],
        out_shape=[jax.ShapeDtypeStruct((bs, tp, d), BF16), jax.ShapeDtypeStruct((bs, nh, LANES, LANES), F32)],
        compiler_params=_cparams("parallel"),
        name="gla_sample",
    )(qs, kk, vv, lf, sg, gain, state_pool)


def _rope_tables(pos, d):
    inv = ROPE_THETA ** (-jnp.arange(0, d, 2, dtype=F32) / d)
    ang = pos.astype(F32)[:, None] * inv[None, :]
    cos, sin = jnp.cos(ang), jnp.sin(ang)
    reps = LANES // d
    a = jnp.tile(jnp.concatenate([cos, cos], axis=1), (1, reps))
    b = jnp.tile(jnp.concatenate([-sin, sin], axis=1), (1, reps))
    return a, b


def _kiwi_tables(pos):
    a64, b64 = _rope_tables(pos, IDX_DIM)
    p = pos.shape[0]
    lane = jnp.arange(IDX_DIM)
    wi_scale = jnp.where(lane < N_IDX_HEADS, N_IDX_HEADS ** -0.5, 0.0).astype(F32)
    a = jnp.concatenate([a64[:, :IDX_DIM], jnp.broadcast_to(wi_scale, (p, IDX_DIM))], axis=1)
    b = jnp.concatenate([b64[:, :IDX_DIM], jnp.zeros((p, IDX_DIM), F32)], axis=1)
    return a, b


def _attn_projections(h, w_in, w_kiwi, q_gain, k_gain, pos, tm, stack_qi):
    dh = LANES
    qw = N_HEADS * dh
    kvw = N_KV_HEADS * dh
    qiw = N_IDX_HEADS * IDX_DIM
    a128, b128 = _rope_tables(pos, dh)
    a64, b64 = _rope_tables(pos, IDX_DIM)
    akw, bkw = _kiwi_tables(pos)
    tn = 512
    (q,) = _proj_rope(h, w_in, 0, qw, q_gain, a128, b128, [BF16], norm=True, half=dh // 2, tm=tm, tn=tn)
    k32, k16 = _proj_rope(h, w_in, qw, kvw, k_gain, a128, b128, [F32, BF16], norm=True, half=dh // 2, tm=tm, tn=tn,
                          layout="kv_rows" if stack_qi else "cols")
    tm_plain = 2 * tm if h.shape[0] % (2 * tm) == 0 else tm
    v32, v16 = _proj_plain(h, w_in, qw + kvw, kvw, [F32, BF16], act=None, tm=tm_plain, tn=tn, kv_rows=stack_qi)
    if stack_qi:
        (qi,) = _proj_rope(h, w_in, qw + 2 * kvw, qiw, q_gain, a64, b64, [BF16], norm=False, half=IDX_DIM // 2,
                           tm=tm, tn=qiw, layout="stack")
    else:
        (qi,) = _proj_rope(h, w_in, qw + 2 * kvw, qiw, q_gain, a64, b64, [BF16], norm=False, half=IDX_DIM // 2,
                           tm=tm, tn=tn)
    if stack_qi:
        kw32, kiab = _proj_rope(h, w_kiwi, 0, LANES, q_gain, akw, bkw, [F32, BF16], norm=False,
                                half=IDX_DIM // 2, tm=tm, tn=LANES, layout="ki_split")
    else:
        (kw32,), kiab = _proj_rope(h, w_kiwi, 0, LANES, q_gain, akw, bkw, [F32], norm=False,
                                   half=IDX_DIM // 2, tm=tm, tn=LANES), None
    return q, qi, k32, k16, v32, v16, kw32, kiab


def _hgrn_projections(h, w_in, log_lb, log_1mlb, tm):
    d = h.shape[1]
    tn = 512
    (qs,) = _proj_plain(h, w_in, 0, d, [F32], act="silu", tm=tm, tn=tn)
    lf, kk = _proj_forget(h, w_in, d, d, log_lb, log_1mlb, tm=tm, tn=tn)
    (vv,) = _proj_plain(h, w_in, 2 * d, d, [F32], act=None, tm=tm, tn=tn)
    (sg,) = _proj_plain(h, w_in, 3 * d, d, [F32], act="silu", tm=tm, tn=tn)
    return qs, kk, vv, lf, sg


def _stack(arrs):
    return arrs[0][None] if len(arrs) == 1 else jnp.stack(arrs)


def kernel(x_prompt, x_sample, c_prompt, c_sample, cache_k, cache_v, cache_kidx, state_hgrn, page_table, norm_gain, w_ada, b_ada, w_ff_up, w_ff_down, w_attn_in, w_attn_out, q_norm, k_norm, w_rec_in, w_rec_out, rec_out_norm, lb_logits):
    batch, seq, d = x_prompt.shape
    bs, n_tok, _ = x_sample.shape
    depth = w_ada.shape[0]
    n_pages = page_table.shape[1]
    past = n_pages * PAGE
    dh = LANES
    kvw = N_KV_HEADS * dh

    c_all = jnp.concatenate([c_prompt, jnp.zeros((SUBLANES - batch, d), F32), c_sample], axis=0)
    mod = _ada_mod(c_all, w_ada, b_ada)
    tm_p = min(1024, seq)
    mod_p = _Mod(mod[:, :batch].reshape(depth, batch, 9, 1, d), False, seq // tm_p)
    mod_s = _Mod(mod[:, SUBLANES:].reshape(depth, bs, 9, d).transpose(0, 2, 1, 3), True, None)

    gains = norm_gain.reshape(depth, 3, 1, d)
    p = jnp.exp(lb_logits - jnp.max(lb_logits, axis=0, keepdims=True))
    p = p / jnp.sum(p, axis=0, keepdims=True)
    lb_all = jnp.cumsum(p, axis=0) - p[0]

    xp = x_prompt.reshape(batch * seq, d)
    xs = x_sample.transpose(1, 0, 2).reshape(n_tok * bs, d)
    pos_p = jnp.arange(seq)
    pos_s = jnp.repeat(past + jnp.arange(n_tok), bs)
    ms = n_tok * bs

    outs_p, outs_s = {}, {}
    for i in range(depth):
        j = i // 2
        xp = _ffn(xp, mod_p, i, 0, gains, w_ff_up, w_ff_down, 0, tm_p, 512)
        xs = _ffn(xs, mod_s, i, 0, gains, w_ff_up, w_ff_down, 0, ms, 512)
        hp = _prenorm(xp, mod_p, i, 1, gains, tm_p)
        hs = _prenorm(xs, mod_s, i, 1, gains, bs)
        if i % 2 == 0:
            w_in = w_attn_in[j]
            col = N_HEADS * dh + 2 * kvw + N_IDX_HEADS * IDX_DIM
            w_kiwi = jnp.pad(w_in[:, col:], ((0, 0), (0, LANES - (w_in.shape[1] - col))))
            qg = q_norm[j].reshape(1, dh)
            kg = k_norm[j].reshape(1, dh)
            q, qi, k32, k16, v32, v16, kw32, kiab = _attn_projections(hp, w_in, w_kiwi, qg, kg, pos_p, tm_p // 2, True)
            o = _attn_prompt(q, qi, kw32, k16, v16, kiab, batch, seq)
            xp = _mm_res(o, w_attn_out[j], xp, mod_p, i, 5, tm=tm_p, tn=512)
            outs_p[i] = (k32, v32, kw32[:, :IDX_DIM])
            q, qi, k32, k16, v32, v16, kw32, _ = _attn_projections(hs, w_in, w_kiwi, qg, kg, pos_s, bs, False)

            def bmajor(a):
                return a.reshape(n_tok, bs, a.shape[1]).transpose(1, 0, 2)

            qi_b = bmajor(qi).reshape(bs, n_tok, N_IDX_HEADS, IDX_DIM)
            qi_ht = qi_b.transpose(0, 2, 1, 3).reshape(bs, N_IDX_HEADS * n_tok, IDX_DIM)
            wi_b = bmajor(kw32)[:, :, IDX_DIM:IDX_DIM + N_IDX_HEADS] * (IDX_DIM ** -0.5)
            amat = (wi_b[:, :, :, None] * jnp.eye(n_tok, dtype=F32)[None, :, None, :]).reshape(bs, n_tok, -1)
            amat = jnp.concatenate([amat, amat], axis=1)
            rows_pad = 2 * n_tok

            def pad_rows(a):
                return jnp.pad(a, ((0, 0), (0, rows_pad - a.shape[1]), (0, 0)))

            ki_new_t = jnp.pad(bmajor(kw32)[:, :, :IDX_DIM].astype(BF16).transpose(0, 2, 1),
                               ((0, 0), (0, 0), (0, PAGE - n_tok)))
            n_pool = cache_k.shape[1]
            pool_kidx_t = cache_kidx.reshape((-1,) + cache_kidx.shape[2:]).transpose(0, 2, 1)
            keys = _idx_sample(page_table, qi_ht, amat, ki_new_t, pool_kidx_t, j * n_pool)
            width = keys.shape[2]
            topk = min(INDEX_TOPK, (past + n_tok) // 4)
            bias = _topk_bias(keys.reshape(bs * rows_pad, width), topk, min(256, bs * rows_pad)).reshape(bs, rows_pad, width)
            q64 = bmajor(q).reshape(bs, n_tok, N_HEADS, dh).transpose(0, 2, 1, 3).reshape(bs, N_HEADS * n_tok, dh)
            o64 = _attn_sample(page_table, q64, bias, pad_rows(bmajor(k16)), pad_rows(bmajor(v16)),
                               cache_k.reshape(-1, PAGE * N_KV_HEADS, dh), cache_v.reshape(-1, PAGE * N_KV_HEADS, dh),
                               j * n_pool)
            o = o64.reshape(bs, N_HEADS, n_tok, dh).transpose(2, 0, 1, 3).reshape(ms, N_HEADS * dh)
            xs = _mm_res(o, w_attn_out[j], xs, mod_s, i, 5, tm=ms, tn=512)
            outs_s[i] = (bmajor(k32), bmajor(v32), bmajor(kw32)[:, :, :IDX_DIM])
        else:
            lb = lb_all[i].reshape(1, d)
            log_lb, log_1mlb = jnp.log(lb), jnp.log1p(-lb)
            gain = rec_out_norm[j].reshape(1, d)
            qs, kk, vv, lf, sg = _hgrn_projections(hp, w_rec_in[j], log_lb, log_1mlb, tm_p)
            y, st = _gla_prompt(qs, kk, vv, lf, sg, gain, batch, seq, min(512, seq))
            xp = _mm_res(y, w_rec_out[j], xp, mod_p, i, 5, tm=tm_p, tn=512)
            outs_p[i] = (st,)
            qs, kk, vv, lf, sg = _hgrn_projections(hs, w_rec_in[j], log_lb, log_1mlb, bs)

            def bpad(a):
                a = a.reshape(n_tok, bs, d).transpose(1, 0, 2)
                return jnp.pad(a, ((0, 0), (0, SUBLANES - n_tok), (0, 0)))

            y, st = _gla_sample(bpad(qs), bpad(kk), bpad(vv), bpad(lf), bpad(sg), gain,
                                state_hgrn.reshape((-1,) + state_hgrn.shape[2:]), j * bs, n_tok)
            y = y[:, :n_tok].transpose(1, 0, 2).reshape(ms, d)
            xs = _mm_res(y, w_rec_out[j], xs, mod_s, i, 5, tm=ms, tn=512)
            outs_s[i] = (st,)
        xp = _ffn(xp, mod_p, i, 2, gains, w_ff_up, w_ff_down, 1, tm_p, 512)
        xs = _ffn(xs, mod_s, i, 2, gains, w_ff_up, w_ff_down, 1, ms, 512)

    attn_layers = [i for i in range(depth) if i % 2 == 0]
    rec_layers = [i for i in range(depth) if i % 2 == 1]
    na, pages_p = len(attn_layers), seq // PAGE
    y_prompt = xp.reshape(batch, seq, d)
    y_sample = xs.reshape(n_tok, bs, d).transpose(1, 0, 2)
    k_prompt = _stack([outs_p[i][0] for i in attn_layers]).reshape(na, batch, pages_p, PAGE, N_KV_HEADS, dh)
    v_prompt = _stack([outs_p[i][1] for i in attn_layers]).reshape(na, batch, pages_p, PAGE, N_KV_HEADS, dh)
    kidx_prompt = _stack([outs_p[i][2] for i in attn_layers]).reshape(na, batch, pages_p, PAGE, IDX_DIM)
    state_prompt = _stack([outs_p[i][0] for i in rec_layers]).astype(state_hgrn.dtype)
    k_sample = _stack([outs_s[i][0] for i in attn_layers]).reshape(na, bs, n_tok, N_KV_HEADS, dh)
    v_sample = _stack([outs_s[i][1] for i in attn_layers]).reshape(na, bs, n_tok, N_KV_HEADS, dh)
    kidx_sample = _stack([outs_s[i][2] for i in attn_layers])
    state_sample = _stack([outs_s[i][0] for i in rec_layers]).astype(state_hgrn.dtype)
    return (y_prompt, y_sample, k_prompt, v_prompt, kidx_prompt, state_prompt, k_sample, v_sample, kidx_sample, state_sample)
```
